```python
import math
import jax
import jax.numpy as jnp
from jax import lax
import numpy as np

D_MODEL = 2048
BATCH = 2
SEQ = 16384
DEPTH = 2

CTX_LEN = 256
GRID_W = 64
HEAD_DIM = 128
ROPE_THETA = 10000.0
Q_BLOCK = 128
EPS = 1e-6

A_HEADS = 8
A_KV_HEADS = 2
A_GROUP = A_HEADS // A_KV_HEADS
A_IN = (A_HEADS + 2 * A_KV_HEADS) * HEAD_DIM
SSM_HEADS = 16
SSM_HEAD_DIM = 64
SSM_INNER = SSM_HEADS * SSM_HEAD_DIM
SSM_GROUPS = 2
SSM_GROUP_HEADS = SSM_HEADS // SSM_GROUPS
SSM_STATE = 128
SSM_CONV = 5
SSM_CONV_DIM = SSM_INNER + 2 * SSM_GROUPS * SSM_STATE
SSM_CHUNK = 128
B_IN = SSM_INNER + SSM_CONV_DIM + 2 * SSM_HEADS
EVEN_IN = A_IN + B_IN
C_HEADS = 8
C_KV_HEADS = 2
C_GROUP = C_HEADS // C_KV_HEADS
C_IN = (C_HEADS + 2 * C_KV_HEADS) * HEAD_DIM
WINDOW = 128
NA_HEADS = 8
NA_ROWS = 8
NA_COLS = 16
NA_IN = 3 * NA_HEADS * HEAD_DIM
ODD_IN = C_IN + NA_IN
MIX_WIDTH = A_HEADS * HEAD_DIM + SSM_INNER
N_EXPERTS = 64
TOP_K = 8
N_GROUPS = 8
TOPK_GROUPS = 4
D_EXPERT = 512
D_SHARED = 512
ROUTE_SCALE = 2.5
MOE_BLOCK = 256

kernel_name = 'hybrid_diffusion_ctx_prefix_moe_block'


def rmsnorm(x, g):
    xf = x.astype(jnp.float32)
    y = xf * lax.rsqrt(jnp.mean(xf * xf, axis=-1, keepdims=True) + EPS)
    return (y * g.astype(jnp.float32)).astype(x.dtype)


def modulate(x, g, shift, scale):
    return rmsnorm(x, g) * (1.0 + scale) + shift


def softmax_f32(logits, sink=None):
    logits = logits.astype(jnp.float32)
    if sink is None:
        return jax.nn.softmax(logits, axis=-1)
    sink_col = jnp.broadcast_to(sink.astype(jnp.float32), logits.shape[:-1] + (1,))
    return jax.nn.softmax(jnp.concatenate([logits, sink_col], axis=-1), axis=-1)[..., :-1]


def axial_rope_tables(n_tokens):
    t = jnp.arange(n_tokens, dtype=jnp.int32)
    row = (t // GRID_W).astype(jnp.float32)
    col = (t % GRID_W).astype(jnp.float32)
    n_freq = HEAD_DIM // 4
    inv = ROPE_THETA ** (-jnp.arange(n_freq, dtype=jnp.float32) / n_freq)
    ang = jnp.stack([row[:, None] * inv, col[:, None] * inv], axis=1)
    return jnp.cos(ang), jnp.sin(ang)


def apply_axial_rope(x, cos, sin):
    b, s, h, _ = x.shape
    xr = x.astype(jnp.float32).reshape(b, s, h, 2, 2, HEAD_DIM // 4)
    x1, x2 = xr[..., 0, :], xr[..., 1, :]
    cc, ss = cos[None, :, None], sin[None, :, None]
    out = jnp.stack([x1 * cc - x2 * ss, x2 * cc + x1 * ss], axis=-2)
    return out.reshape(b, s, h, HEAD_DIM).astype(x.dtype)


def split_heads(p, n_q, n_kv):
    b, l, _ = p.shape
    nq, nk = n_q * HEAD_DIM, n_kv * HEAD_DIM
    q = p[..., :nq].reshape(b, l, n_q, HEAD_DIM)
    k = p[..., nq:nq + nk].reshape(b, l, n_kv, HEAD_DIM)
    v = p[..., nq + nk:nq + 2 * nk].reshape(b, l, n_kv, HEAD_DIM)
    return q, k, v


def group_q(q, n_kv):
    b, l, h, dh = q.shape
    return q.reshape(b, l, n_kv, h // n_kv, dh)


def gqa_scores(q, k):
    return jnp.einsum('bqkrd,bskd->bkrqs', q, k).astype(jnp.float32) * (HEAD_DIM ** -0.5)


def gqa_values(p, v):
    return jnp.einsum('bkrqs,bskd->bqkrd', p.astype(v.dtype), v)


def context_attention(q, k, v, sink=None):
    return gqa_values(softmax_f32(gqa_scores(q, k), sink), v)


def mixer_a(p_lat, p_ctx, gq, gk, cos, sin, need_ctx):
    bsz, seq, _ = p_lat.shape
    q, k, v = split_heads(p_lat, A_HEADS, A_KV_HEADS)
    q = group_q(apply_axial_rope(rmsnorm(q, gq), cos, sin), A_KV_HEADS)
    k = apply_axial_rope(rmsnorm(k, gk), cos, sin)
    qc, kc, vc = split_heads(p_ctx, A_HEADS, A_KV_HEADS)
    kc = rmsnorm(kc, gk)
    k_all = jnp.concatenate([kc, k], axis=1)
    v_all = jnp.concatenate([vc, v], axis=1)

    def one_block(n):
        qb = lax.dynamic_slice_in_dim(q, n * Q_BLOCK, Q_BLOCK, axis=1)
        return gqa_values(softmax_f32(gqa_scores(qb, k_all)), v_all)

    out = lax.map(one_block, jnp.arange(seq // Q_BLOCK))
    y_lat = jnp.moveaxis(out, 0, 1).reshape(bsz, seq, A_HEADS * HEAD_DIM)
    y_ctx = None
    if need_ctx:
        qc = group_q(rmsnorm(qc, gq), A_KV_HEADS)
        y_ctx = context_attention(qc, kc, vc).reshape(bsz, -1, A_HEADS * HEAD_DIM)
    return y_lat, y_ctx


def depthwise_conv_centred(x, w, b):
    k, ch = w.shape
    y = lax.conv_general_dilated(x, w[:, None, :].astype(x.dtype), window_strides=(1,),
                                 padding=[(k // 2, k // 2)], dimension_numbers=('NWC', 'WIO', 'NWC'),
                                 feature_group_count=ch)
    return y + b.astype(x.dtype)


def ssd_chunked(xs, dt, a, bm, cm, h0):
    bsz, l, g, r, p = xs.shape
    n = bm.shape[-1]
    nc = l // SSM_CHUNK
    f32 = jnp.float32
    xc = xs.astype(f32).reshape(bsz, nc, SSM_CHUNK, g, r, p)
    bc = bm.astype(f32).reshape(bsz, nc, SSM_CHUNK, g, n)
    cc = cm.astype(f32).reshape(bsz, nc, SSM_CHUNK, g, n)
    dtc = dt.reshape(bsz, nc, SSM_CHUNK, g, r)
    a_cum = jnp.cumsum(dtc * a, axis=2)
    pos = jnp.arange(SSM_CHUNK)
    upto = (pos[:, None] >= pos[None, :])[None, None, :, :, None, None]
    seg = a_cum[:, :, :, None] - a_cum[:, :, None, :]
    decay = jnp.exp(jnp.where(upto, seg, -jnp.inf))
    cb = jnp.einsum('bcign,bcjgn->bcijg', cc, bc)
    mix = cb[..., None] * decay * dtc[:, :, None]
    y_diag = jnp.einsum('bcijgr,bcjgrp->bcigrp', mix, xc)
    xw = xc * (jnp.exp(a_cum[:, :, -1:] - a_cum) * dtc)[..., None]
    s_chunk = jnp.einsum('bcjgn,bcjgrp->bcgrnp', bc, xw)
    chunk_decay = jnp.exp(a_cum[:, :, -1])

    def step(h, inp):
        dec, s = inp
        return dec[..., None, None] * h + s, h

    h_final, h_prev = lax.scan(step, h0, (jnp.moveaxis(chunk_decay, 1, 0), jnp.moveaxis(s_chunk, 1, 0)))
    h_prev = jnp.moveaxis(h_prev, 0, 1)
    y_off = jnp.einsum('bcign,bcgrnp->bcigrp', cc, h_prev) * jnp.exp(a_cum)[..., None]
    return (y_diag + y_off).reshape(bsz, l, g, r, p), h_final


def mixer_b(p_lat, p_ctx, conv_w, conv_b, a_log, dt_bias, d_skip, g_ssm, need_ctx):
    f32 = jnp.float32
    g, r, p, n = SSM_GROUPS, SSM_GROUP_HEADS, SSM_HEAD_DIM, SSM_STATE
    a = -jnp.exp(a_log.astype(f32)).reshape(2, g, r)
    dtb = dt_bias.astype(f32).reshape(2, g, r)
    dsk = d_skip.astype(f32).reshape(g, r, 1)

    def prep(pp):
        bsz, l, _ = pp.shape
        z = pp[..., :SSM_INNER]
        xbc = jax.nn.silu(depthwise_conv_centred(pp[..., SSM_INNER:SSM_INNER + SSM_CONV_DIM], conv_w, conv_b))
        xs = xbc[..., :SSM_INNER].reshape(bsz, l, g, r, p)
        bm = xbc[..., SSM_INNER:SSM_INNER + g * n].reshape(bsz, l, g, n)
        cm = xbc[..., SSM_INNER + g * n:].reshape(bsz, l, g, n)
        dt = jax.nn.softplus(pp[..., SSM_INNER + SSM_CONV_DIM:].astype(f32).reshape(bsz, l, 2, g, r) + dtb)
        return z, xs, bm, cm, dt

    def finish(y, z, xs):
        bsz, l = y.shape[:2]
        y = (y + dsk * xs.astype(f32)).reshape(bsz, l, SSM_INNER) * jax.nn.silu(z.astype(f32))
        y = rmsnorm(y.reshape(bsz, l, g, SSM_INNER // g), g_ssm.reshape(g, SSM_INNER // g))
        return y.reshape(bsz, l, SSM_INNER).astype(z.dtype)

    def flip(t):
        return jnp.flip(t, axis=1)

    zc, xc, bc, cc, dtc = prep(p_ctx)
    h0 = jnp.zeros((p_ctx.shape[0], g, r, n, p), f32)
    yc_f, hc_f = ssd_chunked(xc, dtc[:, :, 0], a[0], bc, cc, h0)
    yc_b, hc_b = ssd_chunked(flip(xc), flip(dtc[:, :, 1]), a[1], flip(bc), flip(cc), h0)
    zl, xl, bl, cl, dtl = prep(p_lat)
    yl_f, _ = ssd_chunked(xl, dtl[:, :, 0], a[0], bl, cl, hc_f)
    yl_b, _ = ssd_chunked(flip(xl), flip(dtl[:, :, 1]), a[1], flip(bl), flip(cl), hc_b)
    y_lat = finish(yl_f + flip(yl_b), zl, xl)
    y_ctx = finish(yc_f + flip(yc_b), zc, xc) if need_ctx else None
    return y_lat, y_ctx


def mixer_c(p_lat, p_ctx, gq, gk, sink, cos, sin, need_ctx):
    bsz, seq, _ = p_lat.shape
    q, k, v = split_heads(p_lat, C_HEADS, C_KV_HEADS)
    q = group_q(apply_axial_rope(rmsnorm(q, gq), cos, sin), C_KV_HEADS)
    k = apply_axial_rope(rmsnorm(k, gk), cos, sin)
    qc, kc, vc = split_heads(p_ctx, C_HEADS, C_KV_HEADS)
    kc = rmsnorm(kc, gk)
    n_ctx = kc.shape[1]
    sink_b = sink.reshape(C_KV_HEADS, C_GROUP, 1, 1)
    band = Q_BLOCK + 2 * WINDOW
    pad = ((0, 0), (WINDOW, WINDOW), (0, 0), (0, 0))
    k_pad, v_pad = jnp.pad(k, pad), jnp.pad(v, pad)
    rel = jnp.arange(Q_BLOCK)[:, None] + WINDOW - jnp.arange(band)[None, :]

    def one_block(nb):
        q0 = nb * Q_BLOCK
        qb = lax.dynamic_slice_in_dim(q, q0, Q_BLOCK, axis=1)
        kb = lax.dynamic_slice_in_dim(k_pad, q0, band, axis=1)
        vb = lax.dynamic_slice_in_dim(v_pad, q0, band, axis=1)
        kpos = q0 - WINDOW + jnp.arange(band)
        valid = (jnp.abs(rel) <= WINDOW) & ((kpos >= 0) & (kpos < seq))[None, :]
        s_win = jnp.where(valid, gqa_scores(qb, kb), -jnp.inf)
        prob = softmax_f32(jnp.concatenate([gqa_scores(qb, kc), s_win], axis=-1), sink_b)
        return gqa_values(prob[..., :n_ctx], vc) + gqa_values(prob[..., n_ctx:], vb)

    out = lax.map(one_block, jnp.arange(seq // Q_BLOCK))
    y_lat = jnp.moveaxis(out, 0, 1).reshape(bsz, seq, C_HEADS * HEAD_DIM)
    y_ctx = None
    if need_ctx:
        qc = group_q(rmsnorm(qc, gq), C_KV_HEADS)
        y_ctx = context_attention(qc, kc, vc, sink_b).reshape(bsz, -1, C_HEADS * HEAD_DIM)
    return y_lat, y_ctx


def mixer_d(p_lat, p_ctx, gq, gk, rpb, need_ctx):
    bsz, seq, _ = p_lat.shape
    hd = NA_HEADS * HEAD_DIM

    def split(pp):
        l = pp.shape[1]
        q = rmsnorm(pp[..., :hd].reshape(bsz, l, NA_HEADS, HEAD_DIM), gq)
        k = rmsnorm(pp[..., hd:2 * hd].reshape(bsz, l, NA_HEADS, HEAD_DIM), gk)
        v = pp[..., 2 * hd:].reshape(bsz, l, NA_HEADS, HEAD_DIM)
        return q, k, v

    q, k, v = split(p_lat)
    qc, kc, vc = split(p_ctx)
    n_ctx = kc.shape[1]
    rows = seq // GRID_W
    wr = min(NA_ROWS, rows)
    qg = q.reshape(bsz, rows, GRID_W, NA_HEADS, HEAD_DIM)
    kg = k.reshape(bsz, rows, GRID_W, NA_HEADS, HEAD_DIM)
    vg = v.reshape(bsz, rows, GRID_W, NA_HEADS, HEAD_DIM)
    col = jnp.arange(GRID_W)
    col_idx = jnp.clip(col - NA_COLS // 2, 0, GRID_W - NA_COLS)[:, None] + jnp.arange(NA_COLS)[None, :]
    col_bias = rpb[:, :, col_idx - col[:, None] + NA_COLS - 1]
    scale = HEAD_DIM ** -0.5

    def one_row(r):
        r0 = jnp.clip(r - NA_ROWS // 2, 0, rows - wr)
        qr = lax.dynamic_index_in_dim(qg, r, axis=1, keepdims=False)
        kr = lax.dynamic_slice_in_dim(kg, r0, wr, axis=1)[:, :, col_idx]
        vr = lax.dynamic_slice_in_dim(vg, r0, wr, axis=1)[:, :, col_idx]
        bias = jnp.take(col_bias, r0 + jnp.arange(wr) - r + NA_ROWS - 1, axis=1)
        s_nb = jnp.einsum('bqhd,bwqjhd->bhqwj', qr, kr).astype(jnp.float32) * scale
        s_nb = (s_nb + jnp.transpose(bias, (0, 2, 1, 3)).astype(jnp.float32)[None])
        s_nb = s_nb.reshape(bsz, NA_HEADS, GRID_W, wr * NA_COLS)
        s_cx = jnp.einsum('bqhd,bshd->bhqs', qr, kc).astype(jnp.float32) * scale
        prob = softmax_f32(jnp.concatenate([s_cx, s_nb], axis=-1))
        p_cx = prob[..., :n_ctx].astype(v.dtype)
        p_nb = prob[..., n_ctx:].reshape(bsz, NA_HEADS, GRID_W, wr, NA_COLS).astype(v.dtype)
        return jnp.einsum('bhqs,bshd->bqhd', p_cx, vc) + jnp.einsum('bhqwj,bwqjhd->bqhd', p_nb, vr)

    out = lax.map(one_row, jnp.arange(rows))
    y_lat = jnp.moveaxis(out, 0, 1).reshape(bsz, seq, hd)
    y_ctx = None
    if need_ctx:
        y_ctx = context_attention(qc[:, :, :, None], kc, vc).reshape(bsz, -1, hd)
    return y_lat, y_ctx


def moe_ffn(h, w_router, b_router, w_e1, w_e3, w_e2, w_s1, w_s3, w_s2):
    n_tok, d = h.shape
    f32 = jnp.float32
    scores = jax.nn.sigmoid((h @ w_router).astype(f32))
    sel = scores + b_router.astype(f32)
    per_group = N_EXPERTS // N_GROUPS
    grp_score = lax.top_k(sel.reshape(n_tok, N_GROUPS, per_group), 2)[0].sum(-1)
    grp_idx = lax.top_k(grp_score, TOPK_GROUPS)[1]
    grp_keep = jnp.any(grp_idx[:, :, None] == jnp.arange(N_GROUPS)[None, None, :], axis=1)
    sel = jnp.where(jnp.repeat(grp_keep, per_group, axis=1), sel, -jnp.inf)
    top_idx = lax.top_k(sel, TOP_K)[1]
    top_w = jnp.take_along_axis(scores, top_idx, axis=1)
    top_w = top_w / jnp.sum(top_w, axis=-1, keepdims=True) * ROUTE_SCALE
    n_assign = n_tok * TOP_K
    e_flat = top_idx.reshape(-1)
    order = jnp.argsort(e_flat)
    e_sorted = e_flat[order]
    tok_sorted = (order // TOP_K).astype(jnp.int32)
    w_sorted = top_w.reshape(-1)[order]
    counts = jnp.bincount(e_flat, length=N_EXPERTS)
    padded = (counts + MOE_BLOCK - 1) // MOE_BLOCK * MOE_BLOCK
    pad_end = jnp.cumsum(padded)
    dest = (pad_end - padded)[e_sorted] + jnp.arange(n_assign) - (jnp.cumsum(counts) - counts)[e_sorted]
    n_blocks = -(-n_assign // MOE_BLOCK) + N_EXPERTS
    n_slots = n_blocks * MOE_BLOCK
    slot_tok = jnp.full((n_slots,), n_tok, jnp.int32).at[dest].set(tok_sorted)
    slot_w = jnp.zeros((n_slots,), f32).at[dest].set(w_sorted)
    blk_expert = jnp.minimum(jnp.searchsorted(pad_end, jnp.arange(n_blocks) * MOE_BLOCK, side='right'),
                             N_EXPERTS - 1)
    h_pad = jnp.concatenate([h, jnp.zeros((1, d), h.dtype)], axis=0)

    def expert_block(acc, blk):
        e, toks, w = blk
        xb = h_pad[toks]
        y = (jax.nn.silu(xb @ w_e1[e]) * (xb @ w_e3[e])) @ w_e2[e]
        return acc.at[toks].add(y.astype(f32) * w[:, None]), None

    acc, _ = lax.scan(expert_block, jnp.zeros((n_tok + 1, d), f32),
                      (blk_expert, slot_tok.reshape(n_blocks, MOE_BLOCK), slot_w.reshape(n_blocks, MOE_BLOCK)))
    shared = (jax.nn.silu(h @ w_s1) * (h @ w_s3)) @ w_s2
    return acc[:n_tok].astype(h.dtype) + shared


def setup_inputs(seed: int = 0) -> dict:
    key = jax.random.key(seed)
    keys = iter(jax.random.split(key, 48))

    def nrm(shape, std):
        return jax.random.normal(next(keys), shape, jnp.float32) * std

    d = D_MODEL
    n_even = (DEPTH + 1) // 2
    n_odd = DEPTH // 2
    dt0 = jnp.exp(jax.random.uniform(next(keys), (n_even, 2, SSM_HEADS), jnp.float32,
                                     math.log(1e-3), math.log(1e-1)))
    return {
        'x': nrm((BATCH, SEQ, d), 1.0),
        'c': nrm((BATCH, d), 1.0),
        'ctx': nrm((BATCH, CTX_LEN, d), 1.0),
        'c_ctx': nrm((d,), 1.0),
        'w_ada': nrm((DEPTH, d, 6 * d), 0.5 * d ** -0.5),
        'b_ada': nrm((DEPTH, 6 * d), 0.01),
        'g_mix': 1.0 + nrm((DEPTH, d), 0.02),
        'g_ffn': 1.0 + nrm((DEPTH, d), 0.02),
        'w_in_even': nrm((n_even, d, EVEN_IN), d ** -0.5),
        'gq_a': 1.0 + nrm((n_even, HEAD_DIM), 0.02),
        'gk_a': 1.0 + nrm((n_even, HEAD_DIM), 0.02),
        'conv_w': nrm((n_even, SSM_CONV, SSM_CONV_DIM), SSM_CONV ** -0.5),
        'conv_b': nrm((n_even, SSM_CONV_DIM), 0.01),
        'a_log': jnp.log(jax.random.uniform(next(keys), (n_even, 2, SSM_HEADS), jnp.float32, 1.0, 16.0)),
        'dt_bias': dt0 + jnp.log(-jnp.expm1(-dt0)),
        'd_skip': 1.0 + nrm((n_even, SSM_HEADS), 0.1),
        'g_ssm': 1.0 + nrm((n_even, SSM_INNER), 0.02),
        'w_in_odd': nrm((n_odd, d, ODD_IN), d ** -0.5),
        'gq_c': 1.0 + nrm((n_odd, HEAD_DIM), 0.02),
        'gk_c': 1.0 + nrm((n_odd, HEAD_DIM), 0.02),
        'sink_c': nrm((n_odd, C_HEADS), 0.5),
        'gq_d': 1.0 + nrm((n_odd, HEAD_DIM), 0.02),
        'gk_d': 1.0 + nrm((n_odd, HEAD_DIM), 0.02),
        'rpb_d': nrm((n_odd, NA_HEADS, 2 * NA_ROWS - 1, 2 * NA_COLS - 1), 0.05),
        'w_out': nrm((DEPTH, MIX_WIDTH, d), MIX_WIDTH ** -0.5),
        'w_router': nrm((DEPTH, d, N_EXPERTS), d ** -0.5),
        'b_router': nrm((DEPTH, N_EXPERTS), 0.01),
        'w_e1': nrm((DEPTH, N_EXPERTS, d, D_EXPERT), d ** -0.5),
        'w_e3': nrm((DEPTH, N_EXPERTS, d, D_EXPERT), d ** -0.5),
        'w_e2': nrm((DEPTH, N_EXPERTS, D_EXPERT, d), D_EXPERT ** -0.5),
        'w_s1': nrm((DEPTH, d, D_SHARED), d ** -0.5),
        'w_s3': nrm((DEPTH, d, D_SHARED), d ** -0.5),
        'w_s2': nrm((DEPTH, D_SHARED, d), D_SHARED ** -0.5),
    }


def reference(x, c, ctx, c_ctx, w_ada, b_ada, g_mix, g_ffn,
              w_in_even, gq_a, gk_a, conv_w, conv_b, a_log, dt_bias, d_skip, g_ssm,
              w_in_odd, gq_c, gk_c, sink_c, gq_d, gk_d, rpb_d,
              w_out, w_router, b_router, w_e1, w_e3, w_e2, w_s1, w_s3, w_s2):
    bsz, seq, d = x.shape
    n_ctx = ctx.shape[1]
    cos, sin = axial_rope_tables(seq)
    x_lat, x_ctx = x, ctx
    for layer in range(DEPTH):
        last = layer == DEPTH - 1
        i = layer // 2
        mod_lat = (jax.nn.silu(c) @ w_ada[layer] + b_ada[layer]).reshape(bsz, 6, 1, d)
        mod_ctx = (jax.nn.silu(c_ctx) @ w_ada[layer] + b_ada[layer]).reshape(6, 1, d)
        h_lat = modulate(x_lat, g_mix[layer], mod_lat[:, 0], mod_lat[:, 1])
        h_ctx = modulate(x_ctx, g_mix[layer], mod_ctx[0], mod_ctx[1])
        if layer % 2 == 0:
            p_lat = h_lat @ w_in_even[i]
            p_ctx = h_ctx @ w_in_even[i]
            y1_lat, y1_ctx = mixer_a(p_lat[..., :A_IN], p_ctx[..., :A_IN], gq_a[i], gk_a[i], cos, sin, not last)
            y2_lat, y2_ctx = mixer_b(p_lat[..., A_IN:], p_ctx[..., A_IN:], conv_w[i], conv_b[i], a_log[i],
                                     dt_bias[i], d_skip[i], g_ssm[i], not last)
        else:
            p_lat = h_lat @ w_in_odd[i]
            p_ctx = h_ctx @ w_in_odd[i]
            y1_lat, y1_ctx = mixer_c(p_lat[..., :C_IN], p_ctx[..., :C_IN], gq_c[i], gk_c[i], sink_c[i],
                                     cos, sin, not last)
            y2_lat, y2_ctx = mixer_d(p_lat[..., C_IN:], p_ctx[..., C_IN:], gq_d[i], gk_d[i], rpb_d[i], not last)
        x_lat = x_lat + mod_lat[:, 2] * (jnp.concatenate([y1_lat, y2_lat], axis=-1) @ w_out[layer])
        moe_w = (w_router[layer], b_router[layer], w_e1[layer], w_e3[layer], w_e2[layer],
                 w_s1[layer], w_s3[layer], w_s2[layer])
        f_lat_in = modulate(x_lat, g_ffn[layer], mod_lat[:, 3], mod_lat[:, 4]).reshape(bsz * seq, d)
        if last:
            f = moe_ffn(f_lat_in, *moe_w)
            x_lat = x_lat + mod_lat[:, 5] * f.reshape(bsz, seq, d)
        else:
            x_ctx = x_ctx + mod_ctx[2] * (jnp.concatenate([y1_ctx, y2_ctx], axis=-1) @ w_out[layer])
            f_ctx_in = modulate(x_ctx, g_ffn[layer], mod_ctx[3], mod_ctx[4]).reshape(bsz * n_ctx, d)
            f = moe_ffn(jnp.concatenate([f_lat_in, f_ctx_in], axis=0), *moe_w)
            x_lat = x_lat + mod_lat[:, 5] * f[:bsz * seq].reshape(bsz, seq, d)
            x_ctx = x_ctx + mod_ctx[5] * f[bsz * seq:].reshape(bsz, n_ctx, d)
    return x_lat
```

```python
import functools
import math

import jax
import jax.numpy as jnp
from jax import lax
from jax.experimental import pallas as pl
from jax.experimental.pallas import tpu as pltpu

F32 = jnp.float32
BF16 = jnp.bfloat16
I32 = jnp.int32
U32 = jnp.uint32

EPS = 1e-6
HEAD_DIM = 128
GRID_W = 64
ROPE_THETA = 10000.0
WINDOW = 128
NA_ROWS = 8
NA_COLS = 16
NA_TILE_ROWS = 8
NA_WIN_ROWS = 16
A_HEADS = 8
A_KV_HEADS = 2
SSM_HEADS = 16
SSM_HEAD_DIM = 64
SSM_INNER = SSM_HEADS * SSM_HEAD_DIM
SSM_GROUPS = 2
SSM_STATE = 128
SSM_CONV = 5
SSM_CONV_DIM = SSM_INNER + 2 * SSM_GROUPS * SSM_STATE
SSM_CHUNK = 128
A_IN = (A_HEADS + 2 * A_KV_HEADS) * HEAD_DIM
NA_HEADS = 8
N_EXPERTS = 64
TOP_K = 8
N_GROUPS = 8
TOPK_GROUPS = 4
ROUTE_SCALE = 2.5
MOE_BLOCK = 256

VMEM_LIMIT_BYTES = 56 * 1024 * 1024
ROW_TILE = 256
Q_TILE = 256
HI = lax.Precision.HIGHEST


def _cparams(*sem):
    return pltpu.CompilerParams(dimension_semantics=sem, vmem_limit_bytes=VMEM_LIMIT_BYTES)


def _silu(x):
    return x * jax.nn.sigmoid(x)


def _hi_dot(a, b):
    return jnp.dot(a, b, precision=HI, preferred_element_type=F32)


def _dot(a, b):
    return jnp.dot(a, b, preferred_element_type=F32)


def _dot_nt(a, b):
    return lax.dot_general(a, b, (((1,), (1,)), ((), ())), preferred_element_type=F32)


def _largest_tile(n, cap, mult=128):
    best = None
    for t in range(mult, cap + 1, mult):
        if n % t == 0:
            best = t
    assert best is not None, (n, cap)
    return best


def _adaln_kernel(c_ref, w_ref, b_ref, o_ref):
    o_ref[...] = _hi_dot(_silu(c_ref[...]), w_ref[...]) + b_ref[...]


def _adaln(cvec, w, b):
    d, n = w.shape
    tn = 1024
    return pl.pallas_call(
        _adaln_kernel,
        grid=(n // tn,),
        in_specs=[pl.BlockSpec((8, d), lambda j: (0, 0)),
                  pl.BlockSpec((d, tn), lambda j: (0, j)),
                  pl.BlockSpec((1, tn), lambda j: (0, j))],
        out_specs=pl.BlockSpec((8, tn), lambda j: (0, j)),
        out_shape=jax.ShapeDtypeStruct((8, n), F32),
        compiler_params=_cparams("arbitrary"),
        name="adaln",
    )(cvec, w, b)


def _norm_mod(x, g, shift, scale):
    ms = jnp.mean(x * x, axis=-1, keepdims=True)
    return (x * lax.rsqrt(ms + EPS) * g) * (1.0 + scale) + shift


def _nmm_kernel(x_ref, g_ref, mod_ref, w_ref, o_ref, xn_ref):
    @pl.when(pl.program_id(1) == 0)
    def _():
        xn_ref[...] = _norm_mod(x_ref[...], g_ref[...], mod_ref[0, 0:1, :], mod_ref[0, 1:2, :]).astype(BF16)

    o_ref[...] = _dot(xn_ref[...], w_ref[...])


def _group_of_tile(i, tm, n_lat, seq, bsz):
    return jnp.where(i * tm < n_lat, (i * tm) // seq, bsz)


def _norm_mod_matmul(x, g, mod, w_bf16, *, n_lat, seq, bsz, tn):
    t, d = x.shape
    n = w_bf16.shape[1]
    tm = 2 * ROW_TILE
    assert t % tm == 0 and n % tn == 0 and n_lat % tm == 0
    return pl.pallas_call(
        _nmm_kernel,
        grid=(t // tm, n // tn),
        in_specs=[pl.BlockSpec((tm, d), lambda i, j: (i, 0)),
                  pl.BlockSpec((1, d), lambda i, j: (0, 0)),
                  pl.BlockSpec((1, 6, d), lambda i, j: (_group_of_tile(i, tm, n_lat, seq, bsz), 0, 0)),
                  pl.BlockSpec((d, tn), lambda i, j: (0, j))],
        out_specs=pl.BlockSpec((tm, tn), lambda i, j: (i, j)),
        out_shape=jax.ShapeDtypeStruct((t, n), F32),
        scratch_shapes=[pltpu.VMEM((tm, d), BF16)],
        compiler_params=_cparams("arbitrary", "arbitrary"),
        name="norm_mod_inproj",
    )(x, g, mod, w_bf16)


def _head_norm(x, g):
    ms = jnp.mean(x * x, axis=-1, keepdims=True)
    return x * lax.rsqrt(ms + EPS) * g


def _prep_rope_kernel(p_ref, cos_ref, sin_ref, gq_ref, gk_ref, qt_ref, k_ref, vt_ref, *, n_q, n_kv, tq):
    cos = cos_ref[...]
    sin = sin_ref[...]
    lane = lax.broadcasted_iota(I32, cos.shape, 1)
    first_half = (lane % 64) < 32

    def norm_rope(x, g):
        y = _head_norm(x, g)
        swapped = jnp.where(first_half, pltpu.roll(y, 96, 1), pltpu.roll(y, 32, 1))
        return y * cos + swapped * sin

    group = n_q // n_kv
    for h in range(n_q):
        q = norm_rope(p_ref[:, h * HEAD_DIM:(h + 1) * HEAD_DIM], gq_ref[...]) * (HEAD_DIM ** -0.5)
        kvh, hh = divmod(h, group)
        qt_ref[0, kvh, 0, :, hh * tq:(hh + 1) * tq] = q.T.astype(BF16)
    for h in range(n_kv):
        c0 = (n_q + h) * HEAD_DIM
        k_ref[0, h] = norm_rope(p_ref[:, c0:c0 + HEAD_DIM], gk_ref[...]).astype(BF16)
        c1 = (n_q + n_kv + h) * HEAD_DIM
        vt_ref[0, h] = p_ref[:, c1:c1 + HEAD_DIM].T.astype(BF16)


def _pos_maps(tm, n_lat, seq, n_ctx):
    lat_tiles, seq_tiles, ctx_tiles = n_lat // tm, seq // tm, n_ctx // tm

    def batch_of(i):
        return jnp.where(i < lat_tiles, i // seq_tiles, (i - lat_tiles) // ctx_tiles)

    def pos_of(i):
        return jnp.where(i < lat_tiles, ctx_tiles + i % seq_tiles, (i - lat_tiles) % ctx_tiles)

    return batch_of, pos_of


def _prep_rope(p, cos, sin, gq, gk, *, n_q, n_kv, n_lat, seq, n_ctx, bsz):
    t = p.shape[0]
    tm = Q_TILE
    pos_len = n_ctx + seq
    group = n_q // n_kv
    width = (n_q + 2 * n_kv) * HEAD_DIM
    batch_of, pos_of = _pos_maps(tm, n_lat, seq, n_ctx)
    kern = functools.partial(_prep_rope_kernel, n_q=n_q, n_kv=n_kv, tq=tm)
    return pl.pallas_call(
        kern,
        grid=(t // tm,),
        in_specs=[pl.BlockSpec((tm, width), lambda i: (i, 0)),
                  pl.BlockSpec((tm, HEAD_DIM), lambda i: (pos_of(i), 0)),
                  pl.BlockSpec((tm, HEAD_DIM), lambda i: (pos_of(i), 0)),
                  pl.BlockSpec((1, HEAD_DIM), lambda i: (0, 0)),
                  pl.BlockSpec((1, HEAD_DIM), lambda i: (0, 0))],
        out_specs=[pl.BlockSpec((1, n_kv, 1, HEAD_DIM, group * tm), lambda i: (batch_of(i), 0, pos_of(i), 0, 0)),
                   pl.BlockSpec((1, n_kv, tm, HEAD_DIM), lambda i: (batch_of(i), 0, pos_of(i), 0)),
                   pl.BlockSpec((1, n_kv, HEAD_DIM, tm), lambda i: (batch_of(i), 0, 0, pos_of(i)))],
        out_shape=[jax.ShapeDtypeStruct((bsz, n_kv, pos_len // tm, HEAD_DIM, group * tm), BF16),
                   jax.ShapeDtypeStruct((bsz, n_kv, pos_len, HEAD_DIM), BF16),
                   jax.ShapeDtypeStruct((bsz, n_kv, HEAD_DIM, pos_len), BF16)],
        compiler_params=_cparams("arbitrary"),
        name="prep_rope",
    )(p, cos, sin, gq, gk)


def _rope_tables(seq, n_ctx):
    t = jnp.arange(seq, dtype=I32)
    row = (t // GRID_W).astype(F32)
    col = (t % GRID_W).astype(F32)
    n_freq = HEAD_DIM // 4
    inv = ROPE_THETA ** (-jnp.arange(n_freq, dtype=F32) / n_freq)
    ar, ac = row[:, None] * inv, col[:, None] * inv
    cos = jnp.concatenate([jnp.cos(ar), jnp.cos(ar), jnp.cos(ac), jnp.cos(ac)], axis=1)
    sin = jnp.concatenate([-jnp.sin(ar), jnp.sin(ar), -jnp.sin(ac), jnp.sin(ac)], axis=1)
    cos = jnp.concatenate([jnp.ones((n_ctx, HEAD_DIM), F32), cos], axis=0)
    sin = jnp.concatenate([jnp.zeros((n_ctx, HEAD_DIM), F32), sin], axis=0)
    return cos, sin


def _flash_t_kernel(qt_ref, k_ref, vt_ref, o_ref, m_ref, l_ref, acc_ref, *, tk, n_k, tq, group):
    qt = qt_ref[0, 0, 0]
    m_ref[...] = jnp.full(m_ref.shape, -jnp.inf, F32)
    l_ref[...] = jnp.zeros(l_ref.shape, F32)
    acc_ref[...] = jnp.zeros(acc_ref.shape, F32)

    def body(j, carry):
        off = pl.multiple_of(j * tk, tk)
        s = _dot(k_ref[0, 0, pl.ds(off, tk), :], qt)
        m_old = m_ref[...]
        m_new = jnp.maximum(m_old, jnp.max(s, axis=0, keepdims=True))
        alpha = jnp.exp(m_old - m_new)
        p = jnp.exp(s - m_new)
        l_ref[...] = alpha * l_ref[...] + jnp.sum(p, axis=0, keepdims=True)
        acc_ref[...] = alpha * acc_ref[...] + _dot(vt_ref[0, 0, :, pl.ds(off, tk)], p.astype(BF16))
        m_ref[...] = m_new
        return carry

    lax.fori_loop(0, n_k, body, 0)
    out = acc_ref[...] / l_ref[...]
    for h in range(group):
        o_ref[:, h * HEAD_DIM:(h + 1) * HEAD_DIM] = out[:, h * tq:(h + 1) * tq].T.astype(o_ref.dtype)


def _attention_a(qt, k, vt, *, n_lat, seq, n_ctx, bsz, need_ctx):
    n_kv, group, tq = k.shape[1], qt.shape[-1] // Q_TILE, Q_TILE
    pos_len = n_ctx + seq
    t = n_lat + bsz * n_ctx
    width = n_kv * group * HEAD_DIM
    scratch = [pltpu.VMEM((1, group * tq), F32), pltpu.VMEM((1, group * tq), F32),
               pltpu.VMEM((HEAD_DIM, group * tq), F32)]
    tk = _largest_tile(pos_len, 768)
    y = pl.pallas_call(
        functools.partial(_flash_t_kernel, tk=tk, n_k=pos_len // tk, tq=tq, group=group),
        grid=(bsz, n_kv, seq // tq),
        in_specs=[pl.BlockSpec((1, 1, 1, HEAD_DIM, group * tq), lambda b, h, i: (b, h, n_ctx // tq + i, 0, 0)),
                  pl.BlockSpec((1, 1, pos_len, HEAD_DIM), lambda b, h, i: (b, h, 0, 0)),
                  pl.BlockSpec((1, 1, HEAD_DIM, pos_len), lambda b, h, i: (b, h, 0, 0))],
        out_specs=pl.BlockSpec((tq, group * HEAD_DIM), lambda b, h, i: (b * (seq // tq) + i, h)),
        out_shape=jax.ShapeDtypeStruct((n_lat, width), BF16),
        scratch_shapes=scratch,
        compiler_params=_cparams("arbitrary", "arbitrary", "arbitrary"),
        name="attn_a_latent",
    )(qt, k, vt)
    if not need_ctx:
        return y
    tkc = _largest_tile(n_ctx, 768)
    y_ctx = pl.pallas_call(
        functools.partial(_flash_t_kernel, tk=tkc, n_k=n_ctx // tkc, tq=tq, group=group),
        grid=(bsz, n_kv, n_ctx // tq),
        in_specs=[pl.BlockSpec((1, 1, 1, HEAD_DIM, group * tq), lambda b, h, i: (b, h, i, 0, 0)),
                  pl.BlockSpec((1, 1, n_ctx, HEAD_DIM), lambda b, h, i: (b, h, 0, 0)),
                  pl.BlockSpec((1, 1, HEAD_DIM, n_ctx), lambda b, h, i: (b, h, 0, 0))],
        out_specs=pl.BlockSpec((tq, group * HEAD_DIM), lambda b, h, i: (b * (n_ctx // tq) + i, h)),
        out_shape=jax.ShapeDtypeStruct((bsz * n_ctx, width), BF16),
        scratch_shapes=scratch,
        compiler_params=_cparams("arbitrary", "arbitrary", "arbitrary"),
        name="attn_a_context",
    )(qt, k, vt)
    return jnp.concatenate([y, y_ctx], axis=0)


def _segment_edges(i, tm, n_lat, seq, n_ctx):
    lat_tiles, seq_tiles, ctx_tiles = n_lat // tm, seq // tm, n_ctx // tm
    is_lat = i < lat_tiles
    pos = jnp.where(is_lat, i % seq_tiles, (i - lat_tiles) % ctx_tiles)
    last = jnp.where(is_lat, seq_tiles, ctx_tiles) - 1
    return pos == 0, pos == last


def _conv_kernel(cur_ref, prev_ref, next_ref, w_ref, b_ref, o_ref, ext_ref, *, tm, n_lat, seq, n_ctx):
    first, last = _segment_edges(pl.program_id(0), tm, n_lat, seq, n_ctx)
    ext_ref[0:8, :] = jnp.where(first, 0.0, prev_ref[...])
    ext_ref[8:8 + tm, :] = cur_ref[...]
    ext_ref[8 + tm:16 + tm, :] = jnp.where(last, 0.0, next_ref[...])
    acc = jnp.broadcast_to(b_ref[...], (tm, b_ref.shape[1]))
    half = SSM_CONV // 2
    for kk in range(SSM_CONV):
        acc = acc + w_ref[kk:kk + 1, :] * ext_ref[pl.ds(8 - half + kk, tm), :]
    o_ref[...] = _silu(acc)


def _ssm_conv(p, conv_w, conv_b, *, col0, n_lat, seq, n_ctx):
    t = p.shape[0]
    tm, tc = ROW_TILE, 512
    assert col0 % tc == 0 and SSM_CONV_DIM % tc == 0
    cb0 = col0 // tc
    r8 = tm // 8
    kern = functools.partial(_conv_kernel, tm=tm, n_lat=n_lat, seq=seq, n_ctx=n_ctx)
    return pl.pallas_call(
        kern,
        grid=(t // tm, SSM_CONV_DIM // tc),
        in_specs=[pl.BlockSpec((tm, tc), lambda i, j: (i, cb0 + j)),
                  pl.BlockSpec((8, tc), lambda i, j: (jnp.maximum(i * r8 - 1, 0), cb0 + j)),
                  pl.BlockSpec((8, tc), lambda i, j: (jnp.minimum((i + 1) * r8, t // 8 - 1), cb0 + j)),
                  pl.BlockSpec((SSM_CONV, tc), lambda i, j: (0, j)),
                  pl.BlockSpec((1, tc), lambda i, j: (0, j))],
        out_specs=pl.BlockSpec((tm, tc), lambda i, j: (i, j)),
        out_shape=jax.ShapeDtypeStruct((t, SSM_CONV_DIM), F32),
        scratch_shapes=[pltpu.VMEM((tm + 16, tc), F32)],
        compiler_params=_cparams("arbitrary", "arbitrary"),
        name="ssm_conv",
    )(p, p, p, conv_w, conv_b)


def _dt_kernel(p_ref, b_ref, o_ref):
    x = p_ref[...] + b_ref[...]
    sp = jnp.maximum(x, 0.0) + jnp.log1p(jnp.exp(-jnp.abs(x)))
    o_ref[0] = sp
    o_ref[1] = pltpu.roll(sp, HEAD_DIM - SSM_HEADS, 1)


def _ssm_dt(p, bias_row, *, col0):
    t = p.shape[0]
    tm = 2 * ROW_TILE
    assert col0 % 128 == 0
    return pl.pallas_call(
        _dt_kernel,
        grid=(t // tm,),
        in_specs=[pl.BlockSpec((tm, 128), lambda i: (i, col0 // 128)),
                  pl.BlockSpec((1, 128), lambda i: (0, 0))],
        out_specs=pl.BlockSpec((2, tm, 128), lambda i: (0, i, 0)),
        out_shape=jax.ShapeDtypeStruct((2, t, 128), F32),
        compiler_params=_cparams("arbitrary"),
        name="ssm_dt",
    )(p, bias_row)


def _ssd_kernel(x_ref, bc_ref, dt_ref, a_ref, e_ref, y_ref, h_ref):
    d = pl.program_id(1)
    lc = SSM_CHUNK
    gw = SSM_INNER // SSM_GROUPS
    hpg = SSM_HEADS // SSM_GROUPS

    @pl.when(pl.program_id(2) == 0)
    def _():
        h_ref[...] = jnp.zeros(h_ref.shape, F32)

    dt = dt_ref[0]
    w = dt * a_ref[0]
    row = lax.broadcasted_iota(I32, (lc, lc), 0)
    col = lax.broadcasted_iota(I32, (lc, lc), 1)
    fwd = d == 0
    sgn = jnp.where(fwd, 1, -1)
    allowed = (row - col) * sgn >= 0
    allowed_t = (col - row) * sgn >= 0
    cs = _hi_dot(allowed.astype(F32), w)
    cs_t = _hi_dot(w.T, allowed_t.astype(F32))
    dt_t = dt.T
    tot = jnp.where(fwd, cs[lc - 1:lc, :], cs[0:1, :])
    e = e_ref[...]
    x = x_ref[...]
    xw = (x * _hi_dot(jnp.exp(tot - cs) * dt, e)).astype(BF16)
    off_scale = _hi_dot(jnp.exp(cs), e)
    state_decay = _hi_dot(jnp.exp(jnp.broadcast_to(tot, (lc, 128))), e)
    xb = x.astype(BF16)
    for g in range(SSM_GROUPS):
        bm = bc_ref[:, g * SSM_STATE:(g + 1) * SSM_STATE]
        cm = bc_ref[:, (SSM_GROUPS + g) * SSM_STATE:(SSM_GROUPS + g + 1) * SSM_STATE].astype(BF16)
        cb = _dot_nt(cm, bm.astype(BF16))
        h_prev = h_ref[g]
        y_off = _dot(cm, h_prev.astype(BF16))
        s_chunk = _dot(bm.T.astype(BF16), xw[:, g * gw:(g + 1) * gw])
        h_ref[g] = state_decay[:, g * gw:(g + 1) * gw] * h_prev + s_chunk
        pieces = []
        for r in range(hpg):
            hh = g * hpg + r
            seg = cs[:, hh:hh + 1] - cs_t[hh:hh + 1, :]
            dec = jnp.exp(jnp.where(allowed, seg, -jnp.inf))
            mix = (cb * dec * dt_t[hh:hh + 1, :]).astype(BF16)
            pieces.append(_dot(mix, xb[:, hh * SSM_HEAD_DIM:(hh + 1) * SSM_HEAD_DIM]))
        y_ref[0, :, g * gw:(g + 1) * gw] = (jnp.concatenate(pieces, axis=1)
                                           + y_off * off_scale[:, g * gw:(g + 1) * gw])


def _ssd_scan(xbc, dt2, a_rows, expand, *, n_lat, seq, n_ctx, bsz):
    t = xbc.shape[0]
    lc = SSM_CHUNK
    ctx_chunks, lat_chunks = n_ctx // lc, seq // lc
    gw = SSM_INNER // SSM_GROUPS

    def rowblk(b, d, s):
        in_ctx = s < ctx_chunks
        j_ctx = jnp.where(d == 0, s, ctx_chunks - 1 - s)
        sl = s - ctx_chunks
        j_lat = jnp.where(d == 0, sl, lat_chunks - 1 - sl)
        return jnp.where(in_ctx, (n_lat + b * n_ctx) // lc + j_ctx, b * lat_chunks + j_lat)

    return pl.pallas_call(
        _ssd_kernel,
        grid=(bsz, 2, ctx_chunks + lat_chunks),
        in_specs=[pl.BlockSpec((lc, SSM_INNER), lambda b, d, s: (rowblk(b, d, s), 0)),
                  pl.BlockSpec((lc, 2 * SSM_GROUPS * SSM_STATE),
                               lambda b, d, s: (rowblk(b, d, s), SSM_INNER // (2 * SSM_GROUPS * SSM_STATE))),
                  pl.BlockSpec((1, lc, 128), lambda b, d, s: (d, rowblk(b, d, s), 0)),
                  pl.BlockSpec((1, 1, 128), lambda b, d, s: (d, 0, 0)),
                  pl.BlockSpec((128, SSM_INNER), lambda b, d, s: (0, 0))],
        out_specs=pl.BlockSpec((1, lc, SSM_INNER), lambda b, d, s: (d, rowblk(b, d, s), 0)),
        out_shape=jax.ShapeDtypeStruct((2, t, SSM_INNER), F32),
        scratch_shapes=[pltpu.VMEM((SSM_GROUPS, SSM_STATE, gw), F32)],
        compiler_params=_cparams("arbitrary", "arbitrary", "arbitrary"),
        name="ssd_scan",
    )(xbc, xbc, dt2, a_rows, expand)


def _ssd_finish_kernel(yf_ref, yb_ref, xs_ref, z0_ref, z1_ref, dsk_ref, g_ref, o_ref):
    gw = SSM_INNER // SSM_GROUPS
    zs = (z0_ref, z1_ref)
    for g in range(SSM_GROUPS):
        sl = slice(g * gw, (g + 1) * gw)
        y = yf_ref[0, :, sl] + yb_ref[0, :, sl] + dsk_ref[:, sl] * xs_ref[:, sl]
        y = y * _silu(zs[g][...])
        o_ref[:, sl] = _head_norm(y, g_ref[:, sl]).astype(o_ref.dtype)


def _ssd_finish(ydir, xbc, p, dsk_row, g_row, *, z_col0):
    t = xbc.shape[0]
    tm = ROW_TILE
    gw = SSM_INNER // SSM_GROUPS
    assert z_col0 % gw == 0 and SSM_GROUPS == 2
    zb = z_col0 // gw
    return pl.pallas_call(
        _ssd_finish_kernel,
        grid=(t // tm,),
        in_specs=[pl.BlockSpec((1, tm, SSM_INNER), lambda i: (0, i, 0)),
                  pl.BlockSpec((1, tm, SSM_INNER), lambda i: (1, i, 0)),
                  pl.BlockSpec((tm, SSM_INNER), lambda i: (i, 0)),
                  pl.BlockSpec((tm, gw), lambda i: (i, zb)),
                  pl.BlockSpec((tm, gw), lambda i: (i, zb + 1)),
                  pl.BlockSpec((1, SSM_INNER), lambda i: (0, 0)),
                  pl.BlockSpec((1, SSM_INNER), lambda i: (0, 0))],
        out_specs=pl.BlockSpec((tm, SSM_INNER), lambda i: (i, 0)),
        out_shape=jax.ShapeDtypeStruct((t, SSM_INNER), BF16),
        compiler_params=_cparams("arbitrary"),
        name="ssd_finish",
    )(ydir, ydir, xbc, p, p, dsk_row, g_row)


def _window_kernel(qt_ref, k_ref, vt_ref, sink_ref, o_ref, *, tq, group, n_ctx, seq):
    band = tq + 2 * WINDOW
    pos_len = n_ctx + seq
    q0 = pl.program_id(2) * tq
    start = pl.multiple_of(jnp.minimum(n_ctx + q0 - WINDOW, pos_len - band), 128)
    qt = qt_ref[0, 0, 0]
    n = group * tq
    s_cx = _dot(k_ref[0, 0, 0:n_ctx, :], qt)
    s_w = _dot(k_ref[0, 0, pl.ds(start, band), :], qt)
    kpos = start - n_ctx + lax.broadcasted_iota(I32, (band, n), 0)
    qpos = q0 + lax.broadcasted_iota(I32, (band, n), 1) % tq
    valid = (jnp.abs(qpos - kpos) <= WINDOW) & (kpos >= 0)
    s_w = jnp.where(valid, s_w, -jnp.inf)
    sink = sink_ref[0]
    m = jnp.maximum(jnp.maximum(jnp.max(s_cx, axis=0, keepdims=True), jnp.max(s_w, axis=0, keepdims=True)), sink)
    p_cx = jnp.exp(s_cx - m)
    p_w = jnp.exp(s_w - m)
    l = jnp.sum(p_cx, axis=0, keepdims=True) + jnp.sum(p_w, axis=0, keepdims=True) + jnp.exp(sink - m)
    acc = _dot(vt_ref[0, 0, :, 0:n_ctx], p_cx.astype(BF16)) + _dot(vt_ref[0, 0, :, pl.ds(start, band)], p_w.astype(BF16))
    out = acc / l
    for h in range(group):
        o_ref[:, h * HEAD_DIM:(h + 1) * HEAD_DIM] = out[:, h * tq:(h + 1) * tq].T.astype(o_ref.dtype)


def _attention_c(qt, k, vt, sink_rows, *, n_lat, seq, n_ctx, bsz):
    n_kv, group, tq = k.shape[1], qt.shape[-1] // Q_TILE, Q_TILE
    pos_len = n_ctx + seq
    assert n_ctx >= WINDOW and n_ctx % 128 == 0 and pos_len >= tq + 2 * WINDOW
    return pl.pallas_call(
        functools.partial(_window_kernel, tq=tq, group=group, n_ctx=n_ctx, seq=seq),
        grid=(bsz, n_kv, seq // tq),
        in_specs=[pl.BlockSpec((1, 1, 1, HEAD_DIM, group * tq), lambda b, h, i: (b, h, n_ctx // tq + i, 0, 0)),
                  pl.BlockSpec((1, 1, pos_len, HEAD_DIM), lambda b, h, i: (b, h, 0, 0)),
                  pl.BlockSpec((1, 1, HEAD_DIM, pos_len), lambda b, h, i: (b, h, 0, 0)),
                  pl.BlockSpec((1, 1, group * tq), lambda b, h, i: (h, 0, 0))],
        out_specs=pl.BlockSpec((tq, group * HEAD_DIM), lambda b, h, i: (b * (seq // tq) + i, h)),
        out_shape=jax.ShapeDtypeStruct((n_lat, n_kv * group * HEAD_DIM), BF16),
        compiler_params=_cparams("arbitrary", "arbitrary", "arbitrary"),
        name="attn_c_window",
    )(qt, k, vt, sink_rows)


def _prep_na(p, gq, gk, *, col0, n_lat, seq, n_ctx, bsz):
    t = p.shape[0]
    tm = ROW_TILE
    hw = NA_HEADS * HEAD_DIM
    assert col0 % hw == 0 or (2 * col0) % hw == 0
    pos_len = n_ctx + seq
    batch_of, pos_of = _pos_maps(tm, n_lat, seq, n_ctx)
    lat_tiles, seq_tiles, ctx_tiles = n_lat // tm, seq // tm, n_ctx // tm

    def qpos_of(i):
        return jnp.where(i < lat_tiles, i % seq_tiles, seq_tiles + (i - lat_tiles) % ctx_tiles)

    cw = hw // 2
    cb = col0 // cw
    shp = jax.ShapeDtypeStruct((bsz, NA_HEADS, pos_len, HEAD_DIM), BF16)
    kv_spec = pl.BlockSpec((1, NA_HEADS, tm, HEAD_DIM), lambda i: (batch_of(i), 0, pos_of(i), 0))
    q_spec = pl.BlockSpec((1, NA_HEADS, tm, HEAD_DIM), lambda i: (batch_of(i), 0, qpos_of(i), 0))

    def kern(q0, q1, k0, k1, v0, v1, gq_ref, gk_ref, q_ref, k_ref, v_ref):
        halves = NA_HEADS // 2
        for h in range(NA_HEADS):
            sl = slice((h % halves) * HEAD_DIM, (h % halves + 1) * HEAD_DIM)
            pq, pk, pv = ((q0, k0, v0) if h < halves else (q1, k1, v1))
            q_ref[0, h] = (_head_norm(pq[:, sl], gq_ref[...]) * (HEAD_DIM ** -0.5)).astype(BF16)
            k_ref[0, h] = _head_norm(pk[:, sl], gk_ref[...]).astype(BF16)
            v_ref[0, h] = pv[:, sl].astype(BF16)

    def col_spec(j):
        return pl.BlockSpec((tm, cw), lambda i: (i, cb + j))

    return pl.pallas_call(
        kern,
        grid=(t // tm,),
        in_specs=[col_spec(j) for j in range(6)] + [pl.BlockSpec((1, HEAD_DIM), lambda i: (0, 0)),
                                                    pl.BlockSpec((1, HEAD_DIM), lambda i: (0, 0))],
        out_specs=[q_spec, kv_spec, kv_spec],
        out_shape=[shp, shp, shp],
        compiler_params=_cparams("arbitrary"),
        name="prep_na",
    )(p, p, p, p, p, p, gq, gk)


def _na_kernel(q_ref, k_ref, v_ref, bias_ref, o_ref, *, n_ctx, rows):
    tq = NA_TILE_ROWS * GRID_W
    win = NA_WIN_ROWS * GRID_W
    ti = pl.program_id(2)
    w0 = jnp.clip(ti * NA_TILE_ROWS - NA_ROWS // 2, 0, rows - NA_WIN_ROWS)
    start = pl.multiple_of(n_ctx + w0 * GRID_W, GRID_W)
    q = q_ref[0, 0]
    s_cx = _dot_nt(q, k_ref[0, 0, 0:n_ctx, :])
    s_nb = _dot_nt(q, k_ref[0, 0, pl.ds(start, win), :]) + bias_ref[0, 0]
    m = jnp.maximum(jnp.max(s_cx, axis=1, keepdims=True), jnp.max(s_nb, axis=1, keepdims=True))
    p_cx = jnp.exp(s_cx - m)
    p_nb = jnp.exp(s_nb - m)
    l = jnp.sum(p_cx, axis=1, keepdims=True) + jnp.sum(p_nb, axis=1, keepdims=True)
    acc = _dot(p_cx.astype(BF16), v_ref[0, 0, 0:n_ctx, :]) + _dot(p_nb.astype(BF16), v_ref[0, 0, pl.ds(start, win), :])
    o_ref[...] = (acc / l).astype(o_ref.dtype)


def _na_bias_tables(rpb, rows):
    n_tiles = rows // NA_TILE_ROWS
    tables = []
    for ti in (0, 1, n_tiles - 1):
        w0 = min(max(ti * NA_TILE_ROWS - NA_ROWS // 2, 0), rows - NA_WIN_ROWS)
        qr = ti * NA_TILE_ROWS + jnp.arange(NA_TILE_ROWS)
        r0 = jnp.clip(qr - NA_ROWS // 2, 0, rows - NA_ROWS)
        kr = w0 + jnp.arange(NA_WIN_ROWS)
        row_ok = (kr[None, :] >= r0[:, None]) & (kr[None, :] < r0[:, None] + NA_ROWS)
        dr = jnp.clip(kr[None, :] - qr[:, None] + NA_ROWS - 1, 0, 2 * NA_ROWS - 2)
        qc = jnp.arange(GRID_W)
        c0 = jnp.clip(qc - NA_COLS // 2, 0, GRID_W - NA_COLS)
        kc = jnp.arange(GRID_W)
        col_ok = (kc[None, :] >= c0[:, None]) & (kc[None, :] < c0[:, None] + NA_COLS)
        dc = jnp.clip(kc[None, :] - qc[:, None] + NA_COLS - 1, 0, 2 * NA_COLS - 2)
        b = rpb[:, dr[:, None, :, None], dc[None, :, None, :]]
        ok = row_ok[:, None, :, None] & col_ok[None, :, None, :]
        b = jnp.where(ok[None], b, -jnp.inf)
        tables.append(b.reshape(rpb.shape[0], NA_TILE_ROWS * GRID_W, NA_WIN_ROWS * GRID_W))
    return jnp.stack(tables, axis=0).astype(F32)


def _attention_d(q, k, v, bias, *, n_lat, seq, n_ctx, bsz):
    rows = seq // GRID_W
    tq = NA_TILE_ROWS * GRID_W
    win = NA_WIN_ROWS * GRID_W
    n_tiles = rows // NA_TILE_ROWS
    pos_len = n_ctx + seq
    assert rows % NA_TILE_ROWS == 0 and rows >= NA_WIN_ROWS and n_tiles >= 3

    def kind(i):
        return jnp.where(i == 0, 0, jnp.where(i == n_tiles - 1, 2, 1))

    return pl.pallas_call(
        functools.partial(_na_kernel, n_ctx=n_ctx, rows=rows),
        grid=(bsz, NA_HEADS, n_tiles),
        in_specs=[pl.BlockSpec((1, 1, tq, HEAD_DIM), lambda b, h, i: (b, h, i, 0)),
                  pl.BlockSpec((1, 1, pos_len, HEAD_DIM), lambda b, h, i: (b, h, 0, 0)),
                  pl.BlockSpec((1, 1, pos_len, HEAD_DIM), lambda b, h, i: (b, h, 0, 0)),
                  pl.BlockSpec((1, 1, tq, win), lambda b, h, i: (kind(i), h, 0, 0))],
        out_specs=pl.BlockSpec((tq, HEAD_DIM), lambda b, h, i: (b * n_tiles + i, h)),
        out_shape=jax.ShapeDtypeStruct((n_lat, NA_HEADS * HEAD_DIM), BF16),
        compiler_params=_cparams("arbitrary", "arbitrary", "arbitrary"),
        name="attn_d_neighbourhood",
    )(q, k, v, bias)


def _pack_bf16_pairs(f):
    half = f.shape[1] // 2
    bits = lax.bitcast_convert_type(f.astype(BF16).astype(F32), U32)
    lo = lax.shift_right_logical(bits[:, :half], jnp.uint32(16))
    hi = bits[:, half:] & jnp.uint32(0xFFFF0000)
    return hi | lo


def _unpack_bf16_pairs(bits):
    lo = lax.bitcast_convert_type(lax.shift_left(bits, jnp.uint32(16)), F32).astype(BF16)
    hi = lax.bitcast_convert_type(bits & jnp.uint32(0xFFFF0000), F32).astype(BF16)
    return lo, hi


def _outproj_kernel(x_ref, y1_ref, y2_ref, w_ref, mod_ref, g_ref, wr_ref, xo_ref, fpk_ref, lg_ref):
    half = w_ref.shape[0] // 2
    delta = _dot(y1_ref[...], w_ref[0:half, :]) + _dot(y2_ref[...], w_ref[half:, :])
    xn = x_ref[...] + mod_ref[0, 2:3, :] * delta
    xo_ref[...] = xn
    f = _norm_mod(xn, g_ref[...], mod_ref[0, 3:4, :], mod_ref[0, 4:5, :])
    lg_ref[...] = lax.dot_general(wr_ref[...], f, (((1,), (1,)), ((), ())), precision=HI,
                                  preferred_element_type=F32)
    fpk_ref[...] = _pack_bf16_pairs(f)


def _outproj(x, y1, y2, w_bf16, mod, g, wr_t, *, n_rows, n_lat, seq, bsz):
    d = x.shape[1]
    tm = ROW_TILE
    hw = y1.shape[1]
    return pl.pallas_call(
        _outproj_kernel,
        grid=(n_rows // tm,),
        in_specs=[pl.BlockSpec((tm, d), lambda i: (i, 0)),
                  pl.BlockSpec((tm, hw), lambda i: (i, 0)),
                  pl.BlockSpec((tm, hw), lambda i: (i, 0)),
                  pl.BlockSpec((2 * hw, d), lambda i: (0, 0)),
                  pl.BlockSpec((1, 6, d), lambda i: (_group_of_tile(i, tm, n_lat, seq, bsz), 0, 0)),
                  pl.BlockSpec((1, d), lambda i: (0, 0)),
                  pl.BlockSpec((N_EXPERTS, d), lambda i: (0, 0))],
        out_specs=[pl.BlockSpec((tm, d), lambda i: (i, 0)),
                   pl.BlockSpec((tm, d // 2), lambda i: (i, 0)),
                   pl.BlockSpec((N_EXPERTS, tm), lambda i: (0, i))],
        out_shape=[jax.ShapeDtypeStruct((n_rows, d), F32),
                   jax.ShapeDtypeStruct((n_rows, d // 2), U32),
                   jax.ShapeDtypeStruct((N_EXPERTS, n_rows), F32)],
        compiler_params=_cparams("arbitrary"),
        name="outproj_residual_moe_in",
    )(x, y1, y2, w_bf16, mod, g, wr_t)


def _route_kernel(lg_ref, b_ref, e_ref, r_ref, w_ref, cnt_ref, run_ref):
    tr = lg_ref.shape[1]
    per = N_EXPERTS // N_GROUPS
    neg = -jnp.inf

    @pl.when(pl.program_id(0) == 0)
    def _():
        run_ref[...] = jnp.zeros(run_ref.shape, F32)

    scores = jax.nn.sigmoid(lg_ref[...])
    sel = scores + b_ref[...]
    sel3 = sel.reshape(N_GROUPS, per, tr)
    mem = lax.broadcasted_iota(I32, (N_GROUPS, per, tr), 1)
    m1 = jnp.max(sel3, axis=1, keepdims=True)
    i1 = jnp.min(jnp.where(sel3 == m1, mem, per), axis=1, keepdims=True)
    m2 = jnp.max(jnp.where(mem == i1, neg, sel3), axis=1, keepdims=True)
    gs = (m1 + m2).reshape(N_GROUPS, tr)
    gid = lax.broadcasted_iota(I32, (N_GROUPS, tr), 0)
    keep = jnp.zeros((N_GROUPS, tr), F32)
    for _ in range(TOPK_GROUPS):
        gm = jnp.max(gs, axis=0, keepdims=True)
        gi = jnp.min(jnp.where(gs == gm, gid, N_GROUPS), axis=0, keepdims=True)
        hit = gid == gi
        keep = jnp.where(hit, 1.0, keep)
        gs = jnp.where(hit, neg, gs)
    keep3 = jnp.broadcast_to(keep.reshape(N_GROUPS, 1, tr), (N_GROUPS, per, tr))
    cand = jnp.where(keep3 > 0.5, sel3, neg).reshape(N_EXPERTS, tr)
    eid = lax.broadcasted_iota(I32, (N_EXPERTS, tr), 0)
    idxs, ws = [], []
    sel_f = jnp.zeros((N_EXPERTS, tr), F32)
    for _ in range(TOP_K):
        cm = jnp.max(cand, axis=0, keepdims=True)
        ci = jnp.min(jnp.where(cand == cm, eid, N_EXPERTS), axis=0, keepdims=True)
        hit = eid == ci
        idxs.append(ci)
        ws.append(jnp.sum(jnp.where(hit, scores, 0.0), axis=0, keepdims=True))
        sel_f = jnp.where(hit, 1.0, sel_f)
        cand = jnp.where(hit, neg, cand)
    before = (lax.broadcasted_iota(I32, (tr, tr), 0) < lax.broadcasted_iota(I32, (tr, tr), 1))
    rank = _dot(sel_f.astype(BF16), jnp.where(before, 1.0, 0.0).astype(BF16)) + run_ref[...]
    run_ref[...] = run_ref[...] + jnp.sum(sel_f, axis=1, keepdims=True)
    cnt_ref[...] = run_ref[...]
    wsum = ws[0]
    for kk in range(1, TOP_K):
        wsum = wsum + ws[kk]
    for kk in range(TOP_K):
        e_ref[kk:kk + 1, :] = idxs[kk]
        r_ref[kk:kk + 1, :] = jnp.sum(jnp.where(eid == idxs[kk], rank, 0.0), axis=0, keepdims=True).astype(I32)
        w_ref[kk:kk + 1, :] = ws[kk] / wsum * ROUTE_SCALE


def _route(logits_t, b_col):
    n_e, t = logits_t.shape
    tr = ROW_TILE
    return pl.pallas_call(
        _route_kernel,
        grid=(t // tr,),
        in_specs=[pl.BlockSpec((n_e, tr), lambda i: (0, i)),
                  pl.BlockSpec((n_e, 1), lambda i: (0, 0))],
        out_specs=[pl.BlockSpec((TOP_K, tr), lambda i: (0, i)),
                   pl.BlockSpec((TOP_K, tr), lambda i: (0, i)),
                   pl.BlockSpec((TOP_K, tr), lambda i: (0, i)),
                   pl.BlockSpec((n_e, 1), lambda i: (0, 0))],
        out_shape=[jax.ShapeDtypeStruct((TOP_K, t), I32),
                   jax.ShapeDtypeStruct((TOP_K, t), I32),
                   jax.ShapeDtypeStruct((TOP_K, t), F32),
                   jax.ShapeDtypeStruct((n_e, 1), F32)],
        scratch_shapes=[pltpu.VMEM((n_e, 1), F32)],
        compiler_params=_cparams("arbitrary"),
        name="moe_route",
    )(logits_t, b_col)


def _scatter_kernel(dest_ref, f_ref, xs_ref, sem):
    tm = f_ref.shape[0]

    def copy(t, kk):
        return pltpu.make_async_copy(f_ref.at[pl.ds(t, 1)], xs_ref.at[pl.ds(dest_ref[0, kk, t], 1)], sem)

    def issue(t, c):
        for kk in range(TOP_K):
            copy(t, kk).start()
        return c

    def drain(t, c):
        for kk in range(TOP_K):
            copy(t, kk).wait()
        return c

    lax.fori_loop(0, tm, issue, 0)
    lax.fori_loop(0, tm, drain, 0)


def _scatter_rows(dest3, fpk, n_slots):
    t, hw = fpk.shape
    tm = ROW_TILE
    return pl.pallas_call(
        _scatter_kernel,
        grid=(t // tm,),
        in_specs=[pl.BlockSpec((1, TOP_K, tm), lambda i: (i, 0, 0), memory_space=pltpu.SMEM),
                  pl.BlockSpec((tm, hw), lambda i: (i, 0))],
        out_specs=pl.BlockSpec(memory_space=pl.ANY),
        out_shape=jax.ShapeDtypeStruct((n_slots, hw), U32),
        scratch_shapes=[pltpu.SemaphoreType.DMA(())],
        compiler_params=_cparams("arbitrary"),
        name="moe_scatter_rows",
    )(dest3, fpk)


def _ffn(bits, w1, w3, w2):
    half = bits.shape[1]
    lo, hi = _unpack_bf16_pairs(bits)
    h1 = _dot(lo, w1[0:half, :]) + _dot(hi, w1[half:, :])
    h3 = _dot(lo, w3[0:half, :]) + _dot(hi, w3[half:, :])
    return _dot((_silu(h1) * h3).astype(BF16), w2[...])


def _expert_kernel(be_ref, bv_ref, nu_ref, xs_ref, w1_ref, w3_ref, w2_ref, y_ref, w1b, w3b, w2b):
    i = pl.program_id(0)

    @pl.when(i < nu_ref[0])
    def _():
        e = be_ref[i]
        prev = be_ref[jnp.maximum(i - 1, 0)]

        @pl.when((i == 0) | (e != prev))
        def _():
            w1b[...] = w1_ref[0].astype(BF16)
            w3b[...] = w3_ref[0].astype(BF16)
            w2b[...] = w2_ref[0].astype(BF16)

        rows = lax.broadcasted_iota(I32, xs_ref.shape, 0)
        bits = jnp.where(rows < bv_ref[i], xs_ref[...], jnp.uint32(0))
        y_ref[...] = _ffn(bits, w1b, w3b, w2b)


def _experts(blk_expert, blk_valid, n_used, xs, w1, w3, w2):
    n_slots, hw = xs.shape
    d, de = w1.shape[1], w1.shape[2]
    tm = MOE_BLOCK
    grid_spec = pltpu.PrefetchScalarGridSpec(
        num_scalar_prefetch=3,
        grid=(n_slots // tm,),
        in_specs=[pl.BlockSpec((tm, hw), lambda i, be, bv, nu: (i, 0)),
                  pl.BlockSpec((1, d, de), lambda i, be, bv, nu: (be[i], 0, 0)),
                  pl.BlockSpec((1, d, de), lambda i, be, bv, nu: (be[i], 0, 0)),
                  pl.BlockSpec((1, de, d), lambda i, be, bv, nu: (be[i], 0, 0))],
        out_specs=pl.BlockSpec((tm, d), lambda i, be, bv, nu: (i, 0)),
        scratch_shapes=[pltpu.VMEM((d, de), BF16), pltpu.VMEM((d, de), BF16), pltpu.VMEM((de, d), BF16)],
    )
    return pl.pallas_call(
        _expert_kernel,
        grid_spec=grid_spec,
        out_shape=jax.ShapeDtypeStruct((n_slots, d), F32),
        compiler_params=_cparams("arbitrary"),
        name="moe_experts",
    )(blk_expert, blk_valid, n_used, xs, w1, w3, w2)


def _shared_kernel(f_ref, w1_ref, w3_ref, w2_ref, y_ref):
    y_ref[...] = _ffn(f_ref[...], w1_ref, w3_ref, w2_ref)


def _shared_expert(fpk, w1, w3, w2):
    t, hw = fpk.shape
    d, de = w1.shape
    tm = 2 * ROW_TILE
    return pl.pallas_call(
        _shared_kernel,
        grid=(t // tm,),
        in_specs=[pl.BlockSpec((tm, hw), lambda i: (i, 0)),
                  pl.BlockSpec((d, de), lambda i: (0, 0)),
                  pl.BlockSpec((d, de), lambda i: (0, 0)),
                  pl.BlockSpec((de, d), lambda i: (0, 0))],
        out_specs=pl.BlockSpec((tm, d), lambda i: (i, 0)),
        out_shape=jax.ShapeDtypeStruct((t, d), F32),
        compiler_params=_cparams("arbitrary"),
        name="moe_shared_expert",
    )(fpk, w1, w3, w2)


def _combine_kernel(dest_ref, x_ref, ysh_ref, w_ref, mod_ref, y_ref, o_ref, ybuf, sem):
    tm = x_ref.shape[0]

    def copy(t, kk):
        return pltpu.make_async_copy(y_ref.at[pl.ds(dest_ref[0, kk, t], 1)], ybuf.at[kk, pl.ds(t, 1)], sem)

    def issue(t, c):
        for kk in range(TOP_K):
            copy(t, kk).start()
        return c

    def drain(t, c):
        for kk in range(TOP_K):
            copy(t, kk).wait()
        return c

    lax.fori_loop(0, tm, issue, 0)
    lax.fori_loop(0, tm, drain, 0)
    acc = w_ref[:, 0:1] * ybuf[0]
    for kk in range(1, TOP_K):
        acc = acc + w_ref[:, kk:kk + 1] * ybuf[kk]
    o_ref[...] = x_ref[...] + mod_ref[0, 5:6, :] * (acc + ysh_ref[...])


def _combine(dest3, x, ysh, w_tk, mod, y, *, n_lat, seq, bsz):
    t, d = x.shape
    tm = dest3.shape[2]
    return pl.pallas_call(
        _combine_kernel,
        grid=(t // tm,),
        in_specs=[pl.BlockSpec((1, TOP_K, tm), lambda i: (i, 0, 0), memory_space=pltpu.SMEM),
                  pl.BlockSpec((tm, d), lambda i: (i, 0)),
                  pl.BlockSpec((tm, d), lambda i: (i, 0)),
                  pl.BlockSpec((tm, TOP_K), lambda i: (i, 0)),
                  pl.BlockSpec((1, 6, d), lambda i: (_group_of_tile(i, tm, n_lat, seq, bsz), 0, 0)),
                  pl.BlockSpec(memory_space=pl.ANY)],
        out_specs=pl.BlockSpec((tm, d), lambda i: (i, 0)),
        out_shape=jax.ShapeDtypeStruct((t, d), F32),
        scratch_shapes=[pltpu.VMEM((TOP_K, tm, d), F32), pltpu.SemaphoreType.DMA(())],
        compiler_params=_cparams("arbitrary"),
        name="moe_combine",
    )(dest3, x, ysh, w_tk, mod, y)


def _moe(x_new, fpk, logits_t, mod, b_router, w_e1, w_e3, w_e2, w_s1, w_s3, w_s2, *, n_lat, seq, bsz):
    t = x_new.shape[0]
    top_e, top_r, top_w, counts = _route(logits_t, b_router.reshape(N_EXPERTS, 1))
    counts = counts[:, 0].astype(I32)
    blocks = (counts + MOE_BLOCK - 1) // MOE_BLOCK
    blk_end = jnp.cumsum(blocks)
    blk_start = blk_end - blocks
    n_blocks = -(-(t * TOP_K) // MOE_BLOCK) + N_EXPERTS
    n_slots = n_blocks * MOE_BLOCK
    slot_start = blk_start * MOE_BLOCK
    dest = jnp.sum(jnp.where(top_e[None] == jnp.arange(N_EXPERTS, dtype=I32)[:, None, None],
                             slot_start[:, None, None], 0), axis=0) + top_r
    bi = jnp.arange(n_blocks, dtype=I32)
    blk_expert = jnp.minimum(jnp.sum((bi[:, None] >= blk_end[None, :]).astype(I32), axis=1), N_EXPERTS - 1)
    blk_valid = jnp.clip(counts[blk_expert] - (bi - blk_start[blk_expert]) * MOE_BLOCK, 0, MOE_BLOCK)
    n_used = blk_end[-1:].astype(I32)

    tm_s = ROW_TILE
    dest_s = dest.reshape(TOP_K, t // tm_s, tm_s).transpose(1, 0, 2)
    xs = _scatter_rows(dest_s, fpk, n_slots)
    y = _experts(blk_expert.astype(I32), blk_valid.astype(I32), n_used, xs, w_e1, w_e3, w_e2)
    ysh = _shared_expert(fpk, w_s1.astype(BF16), w_s3.astype(BF16), w_s2.astype(BF16))
    tm_c = ROW_TILE // 2
    dest_c = dest.reshape(TOP_K, t // tm_c, tm_c).transpose(1, 0, 2)
    return _combine(dest_c, x_new, ysh, top_w.T, mod, y, n_lat=n_lat, seq=seq, bsz=bsz)


def _pad_cols(w, n):
    return jnp.pad(w, ((0, 0), (0, n - w.shape[1])))


def kernel(x, c, ctx, c_ctx, w_ada, b_ada, g_mix, g_ffn, w_in_even, gq_a, gk_a, conv_w, conv_b, a_log, dt_bias,
           d_skip, g_ssm, w_in_odd, gq_c, gk_c, sink_c, gq_d, gk_d, rpb_d, w_out, w_router, b_router, w_e1, w_e3,
           w_e2, w_s1, w_s3, w_s2):
    bsz, seq, d = x.shape
    n_ctx = ctx.shape[1]
    depth = w_ada.shape[0]
    n_lat = bsz * seq
    geo = dict(n_lat=n_lat, seq=seq, n_ctx=n_ctx, bsz=bsz)
    xs = jnp.concatenate([x.reshape(n_lat, d), ctx.reshape(bsz * n_ctx, d)], axis=0)
    cvec = jnp.zeros((8, d), F32).at[:bsz].set(c).at[bsz].set(c_ctx)
    cos, sin = _rope_tables(seq, n_ctx)
    expand = jnp.zeros((128, SSM_INNER), F32).at[:SSM_HEADS].set(
        jnp.repeat(jnp.eye(SSM_HEADS, dtype=F32), SSM_HEAD_DIM, axis=1))

    for layer in range(depth):
        last = layer == depth - 1
        i = layer // 2
        mod = _adaln(cvec, w_ada[layer], b_ada[layer].reshape(1, -1))[:bsz + 1].reshape(bsz + 1, 6, d)
        g1 = g_mix[layer].reshape(1, d)
        if layer % 2 == 0:
            n_in = w_in_even.shape[2]
            n_pad = -(-n_in // 384) * 384
            p = _norm_mod_matmul(xs, g1, mod, _pad_cols(w_in_even[i], n_pad).astype(BF16),
                                 n_lat=n_lat, seq=seq, bsz=bsz, tn=n_pad // 3)
            qt, k, vt = _prep_rope(p, cos, sin, gq_a[i].reshape(1, -1), gk_a[i].reshape(1, -1),
                                   n_q=A_HEADS, n_kv=A_KV_HEADS, **geo)
            y1 = _attention_a(qt, k, vt, need_ctx=not last, **geo)
            xbc = _ssm_conv(p, conv_w[i], conv_b[i].reshape(1, -1), col0=A_IN + SSM_INNER,
                            n_lat=n_lat, seq=seq, n_ctx=n_ctx)
            dt_row = jnp.zeros((1, 128), F32).at[0, :2 * SSM_HEADS].set(dt_bias[i].reshape(-1))
            dt2 = _ssm_dt(p, dt_row, col0=A_IN + SSM_INNER + SSM_CONV_DIM)
            a_rows = jnp.zeros((2, 1, 128), F32).at[:, 0, :SSM_HEADS].set(-jnp.exp(a_log[i]))
            ydir = _ssd_scan(xbc, dt2, a_rows, expand, **geo)
            y2 = _ssd_finish(ydir, xbc, p, jnp.repeat(d_skip[i], SSM_HEAD_DIM).reshape(1, -1),
                             g_ssm[i].reshape(1, -1), z_col0=A_IN)
        else:
            p = _norm_mod_matmul(xs, g1, mod, w_in_odd[i].astype(BF16), n_lat=n_lat, seq=seq, bsz=bsz,
                                 tn=w_in_odd.shape[2] // 3)
            qt, k, vt = _prep_rope(p, cos, sin, gq_c[i].reshape(1, -1), gk_c[i].reshape(1, -1),
                                   n_q=A_HEADS, n_kv=A_KV_HEADS, **geo)
            group = A_HEADS // A_KV_HEADS
            sink_rows = jnp.repeat(sink_c[i].reshape(A_KV_HEADS, group), Q_TILE, axis=1).reshape(
                A_KV_HEADS, 1, group * Q_TILE)
            y1 = _attention_c(qt, k, vt, sink_rows, **geo)
            qd, kd, vd = _prep_na(p, gq_d[i].reshape(1, -1), gk_d[i].reshape(1, -1), col0=A_IN, **geo)
            y2 = _attention_d(qd, kd, vd, _na_bias_tables(rpb_d[i], seq // GRID_W), **geo)
        n_rows = n_lat if last else xs.shape[0]
        x_new, fpk, logits_t = _outproj(xs, y1, y2, w_out[layer].astype(BF16), mod, g_ffn[layer].reshape(1, d),
                                        w_router[layer].T, n_rows=n_rows, n_lat=n_lat, seq=seq, bsz=bsz)
        xs = _moe(x_new, fpk, logits_t, mod, b_router[layer], w_e1[layer], w_e3[layer], w_e2[layer],
                  w_s1[layer], w_s3[layer], w_s2[layer], n_lat=n_lat, seq=seq, bsz=bsz)
    return xs[:n_lat].reshape(bsz, seq, d)
```

```python
import functools
import math

import jax
import jax.numpy as jnp
from jax import lax
from jax.experimental import pallas as pl
from jax.experimental.pallas import tpu as pltpu

F32 = jnp.float32
BF16 = jnp.bfloat16
I32 = jnp.int32
U32 = jnp.uint32

EPS = 1e-6
HEAD_DIM = 128
GRID_W = 64
ROPE_THETA = 10000.0
WINDOW = 128
NA_ROWS = 8
NA_COLS = 16
NA_TILE_ROWS = 8
NA_WIN_ROWS = 16
A_HEADS = 8
A_KV_HEADS = 2
SSM_HEADS = 16
SSM_HEAD_DIM = 64
SSM_INNER = SSM_HEADS * SSM_HEAD_DIM
SSM_GROUPS = 2
SSM_STATE = 128
SSM_CONV = 5
SSM_CONV_DIM = SSM_INNER + 2 * SSM_GROUPS * SSM_STATE
SSM_CHUNK = 128
A_IN = (A_HEADS + 2 * A_KV_HEADS) * HEAD_DIM
NA_HEADS = 8
N_EXPERTS = 64
TOP_K = 8
N_GROUPS = 8
TOPK_GROUPS = 4
ROUTE_SCALE = 2.5
MOE_BLOCK = 256

VMEM_LIMIT_BYTES = 56 * 1024 * 1024
ROW_TILE = 256
Q_TILE = 256
HI = lax.Precision.HIGHEST


def _cparams(*sem):
    return pltpu.CompilerParams(dimension_semantics=sem, vmem_limit_bytes=VMEM_LIMIT_BYTES)


def _silu(x):
    return x * jax.nn.sigmoid(x)


def _hi_dot(a, b):
    return jnp.dot(a, b, precision=HI, preferred_element_type=F32)


def _dot(a, b):
    return jnp.dot(a, b, preferred_element_type=F32)


def _dot_nt(a, b):
    return lax.dot_general(a, b, (((1,), (1,)), ((), ())), preferred_element_type=F32)


def _largest_tile(n, cap, mult=128):
    best = None
    for t in range(mult, cap + 1, mult):
        if n % t == 0:
            best = t
    assert best is not None, (n, cap)
    return best


def _adaln_kernel(c_ref, w_ref, b_ref, o_ref):
    o_ref[...] = _hi_dot(_silu(c_ref[...]), w_ref[0]) + b_ref[...]


def _adaln(cvec, w, b, layer):
    _, d, n = w.shape
    tn = 1024
    return pl.pallas_call(
        _adaln_kernel,
        grid=(n // tn,),
        in_specs=[pl.BlockSpec((8, d), lambda j: (0, 0)),
                  pl.BlockSpec((1, d, tn), lambda j: (layer, 0, j)),
                  pl.BlockSpec((1, tn), lambda j: (0, j))],
        out_specs=pl.BlockSpec((8, tn), lambda j: (0, j)),
        out_shape=jax.ShapeDtypeStruct((8, n), F32),
        compiler_params=_cparams("arbitrary"),
        name="adaln",
    )(cvec, w, b)


def _norm_mod(x, g, shift, scale):
    ms = jnp.mean(x * x, axis=-1, keepdims=True)
    return (x * lax.rsqrt(ms + EPS) * g) * (1.0 + scale) + shift


def _nmm_kernel(x_ref, g_ref, mod_ref, w_ref, o_ref, xn_ref):
    @pl.when(pl.program_id(1) == 0)
    def _():
        xn_ref[...] = _norm_mod(x_ref[...], g_ref[...], mod_ref[0, 0:1, :], mod_ref[0, 1:2, :]).astype(BF16)

    o_ref[...] = _dot(xn_ref[...], w_ref[...])


def _group_of_tile(i, tm, n_lat, seq, bsz):
    return jnp.where(i * tm < n_lat, (i * tm) // seq, bsz)


def _norm_mod_matmul(x, g, mod, w_bf16, *, n_lat, seq, bsz, tn):
    t, d = x.shape
    n = w_bf16.shape[1]
    tm = 2 * ROW_TILE
    assert t % tm == 0 and n % tn == 0 and n_lat % tm == 0
    return pl.pallas_call(
        _nmm_kernel,
        grid=(t // tm, n // tn),
        in_specs=[pl.BlockSpec((tm, d), lambda i, j: (i, 0)),
                  pl.BlockSpec((1, d), lambda i, j: (0, 0)),
                  pl.BlockSpec((1, 6, d), lambda i, j: (_group_of_tile(i, tm, n_lat, seq, bsz), 0, 0)),
                  pl.BlockSpec((d, tn), lambda i, j: (0, j))],
        out_specs=pl.BlockSpec((tm, tn), lambda i, j: (i, j)),
        out_shape=jax.ShapeDtypeStruct((t, n), F32),
        scratch_shapes=[pltpu.VMEM((tm, d), BF16)],
        compiler_params=_cparams("arbitrary", "arbitrary"),
        name="norm_mod_inproj",
    )(x, g, mod, w_bf16)


def _head_norm(x, g):
    ms = jnp.mean(x * x, axis=-1, keepdims=True)
    return x * lax.rsqrt(ms + EPS) * g


def _prep_rope_kernel(p_ref, cos_ref, sin_ref, gq_ref, gk_ref, qt_ref, k_ref, vt_ref, *, n_q, n_kv, tq):
    cos = cos_ref[...]
    sin = sin_ref[...]
    lane = lax.broadcasted_iota(I32, cos.shape, 1)
    first_half = (lane % 64) < 32

    def norm_rope(x, g):
        y = _head_norm(x, g)
        swapped = jnp.where(first_half, pltpu.roll(y, 96, 1), pltpu.roll(y, 32, 1))
        return y * cos + swapped * sin

    group = n_q // n_kv
    for h in range(n_q):
        q = norm_rope(p_ref[:, h * HEAD_DIM:(h + 1) * HEAD_DIM], gq_ref[...]) * (HEAD_DIM ** -0.5)
        kvh, hh = divmod(h, group)
        qt_ref[0, kvh, 0, :, hh * tq:(hh + 1) * tq] = q.T.astype(BF16)
    for h in range(n_kv):
        c0 = (n_q + h) * HEAD_DIM
        k_ref[0, h] = norm_rope(p_ref[:, c0:c0 + HEAD_DIM], gk_ref[...]).astype(BF16)
        c1 = (n_q + n_kv + h) * HEAD_DIM
        vt_ref[0, h] = p_ref[:, c1:c1 + HEAD_DIM].T.astype(BF16)


def _pos_maps(tm, n_lat, seq, n_ctx):
    lat_tiles, seq_tiles, ctx_tiles = n_lat // tm, seq // tm, n_ctx // tm

    def batch_of(i):
        return jnp.where(i < lat_tiles, i // seq_tiles, (i - lat_tiles) // ctx_tiles)

    def pos_of(i):
        return jnp.where(i < lat_tiles, ctx_tiles + i % seq_tiles, (i - lat_tiles) % ctx_tiles)

    return batch_of, pos_of


def _prep_rope(p, cos, sin, gq, gk, *, n_q, n_kv, n_lat, seq, n_ctx, bsz):
    t = p.shape[0]
    tm = Q_TILE
    pos_len = n_ctx + seq
    group = n_q // n_kv
    width = (n_q + 2 * n_kv) * HEAD_DIM
    batch_of, pos_of = _pos_maps(tm, n_lat, seq, n_ctx)
    kern = functools.partial(_prep_rope_kernel, n_q=n_q, n_kv=n_kv, tq=tm)
    return pl.pallas_call(
        kern,
        grid=(t // tm,),
        in_specs=[pl.BlockSpec((tm, width), lambda i: (i, 0)),
                  pl.BlockSpec((tm, HEAD_DIM), lambda i: (pos_of(i), 0)),
                  pl.BlockSpec((tm, HEAD_DIM), lambda i: (pos_of(i), 0)),
                  pl.BlockSpec((1, HEAD_DIM), lambda i: (0, 0)),
                  pl.BlockSpec((1, HEAD_DIM), lambda i: (0, 0))],
        out_specs=[pl.BlockSpec((1, n_kv, 1, HEAD_DIM, group * tm), lambda i: (batch_of(i), 0, pos_of(i), 0, 0)),
                   pl.BlockSpec((1, n_kv, tm, HEAD_DIM), lambda i: (batch_of(i), 0, pos_of(i), 0)),
                   pl.BlockSpec((1, n_kv, HEAD_DIM, tm), lambda i: (batch_of(i), 0, 0, pos_of(i)))],
        out_shape=[jax.ShapeDtypeStruct((bsz, n_kv, pos_len // tm, HEAD_DIM, group * tm), BF16),
                   jax.ShapeDtypeStruct((bsz, n_kv, pos_len, HEAD_DIM), BF16),
                   jax.ShapeDtypeStruct((bsz, n_kv, HEAD_DIM, pos_len), BF16)],
        compiler_params=_cparams("arbitrary"),
        name="prep_rope",
    )(p, cos, sin, gq, gk)


def _rope_tables(seq, n_ctx):
    t = jnp.arange(seq, dtype=I32)
    row = (t // GRID_W).astype(F32)
    col = (t % GRID_W).astype(F32)
    n_freq = HEAD_DIM // 4
    inv = ROPE_THETA ** (-jnp.arange(n_freq, dtype=F32) / n_freq)
    ar, ac = row[:, None] * inv, col[:, None] * inv
    cos = jnp.concatenate([jnp.cos(ar), jnp.cos(ar), jnp.cos(ac), jnp.cos(ac)], axis=1)
    sin = jnp.concatenate([-jnp.sin(ar), jnp.sin(ar), -jnp.sin(ac), jnp.sin(ac)], axis=1)
    cos = jnp.concatenate([jnp.ones((n_ctx, HEAD_DIM), F32), cos], axis=0)
    sin = jnp.concatenate([jnp.zeros((n_ctx, HEAD_DIM), F32), sin], axis=0)
    return cos, sin


def _flash_t_kernel(qt_ref, k_ref, vt_ref, o_ref, m_ref, l_ref, acc_ref, *, tk, n_k, tq, group):
    qt = qt_ref[0, 0, 0]
    m_ref[...] = jnp.full(m_ref.shape, -jnp.inf, F32)
    l_ref[...] = jnp.zeros(l_ref.shape, F32)
    acc_ref[...] = jnp.zeros(acc_ref.shape, F32)

    def body(j, carry):
        off = pl.multiple_of(j * tk, tk)
        s = _dot(k_ref[0, 0, pl.ds(off, tk), :], qt)
        m_old = m_ref[...]
        m_new = jnp.maximum(m_old, jnp.max(s, axis=0, keepdims=True))
        alpha = jnp.exp(m_old - m_new)
        p = jnp.exp(s - m_new)
        l_ref[...] = alpha * l_ref[...] + jnp.sum(p, axis=0, keepdims=True)
        acc_ref[...] = alpha * acc_ref[...] + _dot(vt_ref[0, 0, :, pl.ds(off, tk)], p.astype(BF16))
        m_ref[...] = m_new
        return carry

    lax.fori_loop(0, n_k, body, 0)
    out = acc_ref[...] / l_ref[...]
    for h in range(group):
        o_ref[:, h * HEAD_DIM:(h + 1) * HEAD_DIM] = out[:, h * tq:(h + 1) * tq].T.astype(o_ref.dtype)


def _attention_a(qt, k, vt, *, n_lat, seq, n_ctx, bsz, need_ctx):
    n_kv, group, tq = k.shape[1], qt.shape[-1] // Q_TILE, Q_TILE
    pos_len = n_ctx + seq
    t = n_lat + bsz * n_ctx
    width = n_kv * group * HEAD_DIM
    scratch = [pltpu.VMEM((1, group * tq), F32), pltpu.VMEM((1, group * tq), F32),
               pltpu.VMEM((HEAD_DIM, group * tq), F32)]
    tk = _largest_tile(pos_len, 768)
    y = pl.pallas_call(
        functools.partial(_flash_t_kernel, tk=tk, n_k=pos_len // tk, tq=tq, group=group),
        grid=(bsz, n_kv, seq // tq),
        in_specs=[pl.BlockSpec((1, 1, 1, HEAD_DIM, group * tq), lambda b, h, i: (b, h, n_ctx // tq + i, 0, 0)),
                  pl.BlockSpec((1, 1, pos_len, HEAD_DIM), lambda b, h, i: (b, h, 0, 0)),
                  pl.BlockSpec((1, 1, HEAD_DIM, pos_len), lambda b, h, i: (b, h, 0, 0))],
        out_specs=pl.BlockSpec((tq, group * HEAD_DIM), lambda b, h, i: (b * (seq // tq) + i, h)),
        out_shape=jax.ShapeDtypeStruct((n_lat, width), BF16),
        scratch_shapes=scratch,
        compiler_params=_cparams("arbitrary", "arbitrary", "arbitrary"),
        name="attn_a_latent",
    )(qt, k, vt)
    if not need_ctx:
        return y
    tkc = _largest_tile(n_ctx, 768)
    y_ctx = pl.pallas_call(
        functools.partial(_flash_t_kernel, tk=tkc, n_k=n_ctx // tkc, tq=tq, group=group),
        grid=(bsz, n_kv, n_ctx // tq),
        in_specs=[pl.BlockSpec((1, 1, 1, HEAD_DIM, group * tq), lambda b, h, i: (b, h, i, 0, 0)),
                  pl.BlockSpec((1, 1, n_ctx, HEAD_DIM), lambda b, h, i: (b, h, 0, 0)),
                  pl.BlockSpec((1, 1, HEAD_DIM, n_ctx), lambda b, h, i: (b, h, 0, 0))],
        out_specs=pl.BlockSpec((tq, group * HEAD_DIM), lambda b, h, i: (b * (n_ctx // tq) + i, h)),
        out_shape=jax.ShapeDtypeStruct((bsz * n_ctx, width), BF16),
        scratch_shapes=scratch,
        compiler_params=_cparams("arbitrary", "arbitrary", "arbitrary"),
        name="attn_a_context",
    )(qt, k, vt)
    return jnp.concatenate([y, y_ctx], axis=0)


def _segment_edges(i, tm, n_lat, seq, n_ctx):
    lat_tiles, seq_tiles, ctx_tiles = n_lat // tm, seq // tm, n_ctx // tm
    is_lat = i < lat_tiles
    pos = jnp.where(is_lat, i % seq_tiles, (i - lat_tiles) % ctx_tiles)
    last = jnp.where(is_lat, seq_tiles, ctx_tiles) - 1
    return pos == 0, pos == last


def _conv_kernel(cur_ref, prev_ref, next_ref, w_ref, b_ref, o_ref, ext_ref, *, tm, n_lat, seq, n_ctx):
    first, last = _segment_edges(pl.program_id(0), tm, n_lat, seq, n_ctx)
    ext_ref[0:8, :] = jnp.where(first, 0.0, prev_ref[...])
    ext_ref[8:8 + tm, :] = cur_ref[...]
    ext_ref[8 + tm:16 + tm, :] = jnp.where(last, 0.0, next_ref[...])
    acc = jnp.broadcast_to(b_ref[...], (tm, b_ref.shape[1]))
    half = SSM_CONV // 2
    for kk in range(SSM_CONV):
        acc = acc + w_ref[kk:kk + 1, :] * ext_ref[pl.ds(8 - half + kk, tm), :]
    o_ref[...] = _silu(acc)


def _ssm_conv(p, conv_w, conv_b, *, col0, n_lat, seq, n_ctx):
    t = p.shape[0]
    tm, tc = ROW_TILE, 512
    assert col0 % tc == 0 and SSM_CONV_DIM % tc == 0
    cb0 = col0 // tc
    r8 = tm // 8
    kern = functools.partial(_conv_kernel, tm=tm, n_lat=n_lat, seq=seq, n_ctx=n_ctx)
    return pl.pallas_call(
        kern,
        grid=(t // tm, SSM_CONV_DIM // tc),
        in_specs=[pl.BlockSpec((tm, tc), lambda i, j: (i, cb0 + j)),
                  pl.BlockSpec((8, tc), lambda i, j: (jnp.maximum(i * r8 - 1, 0), cb0 + j)),
                  pl.BlockSpec((8, tc), lambda i, j: (jnp.minimum((i + 1) * r8, t // 8 - 1), cb0 + j)),
                  pl.BlockSpec((SSM_CONV, tc), lambda i, j: (0, j)),
                  pl.BlockSpec((1, tc), lambda i, j: (0, j))],
        out_specs=pl.BlockSpec((tm, tc), lambda i, j: (i, j)),
        out_shape=jax.ShapeDtypeStruct((t, SSM_CONV_DIM), F32),
        scratch_shapes=[pltpu.VMEM((tm + 16, tc), F32)],
        compiler_params=_cparams("arbitrary", "arbitrary"),
        name="ssm_conv",
    )(p, p, p, conv_w, conv_b)


def _dt_kernel(p_ref, b_ref, o_ref):
    x = p_ref[...] + b_ref[...]
    sp = jnp.maximum(x, 0.0) + jnp.log1p(jnp.exp(-jnp.abs(x)))
    o_ref[0] = sp
    o_ref[1] = pltpu.roll(sp, HEAD_DIM - SSM_HEADS, 1)


def _ssm_dt(p, bias_row, *, col0):
    t = p.shape[0]
    tm = 2 * ROW_TILE
    assert col0 % 128 == 0
    return pl.pallas_call(
        _dt_kernel,
        grid=(t // tm,),
        in_specs=[pl.BlockSpec((tm, 128), lambda i: (i, col0 // 128)),
                  pl.BlockSpec((1, 128), lambda i: (0, 0))],
        out_specs=pl.BlockSpec((2, tm, 128), lambda i: (0, i, 0)),
        out_shape=jax.ShapeDtypeStruct((2, t, 128), F32),
        compiler_params=_cparams("arbitrary"),
        name="ssm_dt",
    )(p, bias_row)


def _ssd_kernel(x_ref, bc_ref, dt_ref, a_ref, e_ref, y_ref, h_ref):
    d = pl.program_id(1)
    lc = SSM_CHUNK
    gw = SSM_INNER // SSM_GROUPS
    hpg = SSM_HEADS // SSM_GROUPS

    @pl.when(pl.program_id(2) == 0)
    def _():
        h_ref[...] = jnp.zeros(h_ref.shape, F32)

    dt = dt_ref[0]
    w = dt * a_ref[0]
    row = lax.broadcasted_iota(I32, (lc, lc), 0)
    col = lax.broadcasted_iota(I32, (lc, lc), 1)
    fwd = d == 0
    sgn = jnp.where(fwd, 1, -1)
    allowed = (row - col) * sgn >= 0
    allowed_t = (col - row) * sgn >= 0
    cs = _hi_dot(allowed.astype(F32), w)
    cs_t = _hi_dot(w.T, allowed_t.astype(F32))
    dt_t = dt.T
    tot = jnp.where(fwd, cs[lc - 1:lc, :], cs[0:1, :])
    e = e_ref[...]
    x = x_ref[...]
    xw = (x * _hi_dot(jnp.exp(tot - cs) * dt, e)).astype(BF16)
    off_scale = _hi_dot(jnp.exp(cs), e)
    state_decay = _hi_dot(jnp.exp(jnp.broadcast_to(tot, (lc, 128))), e)
    xb = x.astype(BF16)
    for g in range(SSM_GROUPS):
        bm = bc_ref[:, g * SSM_STATE:(g + 1) * SSM_STATE]
        cm = bc_ref[:, (SSM_GROUPS + g) * SSM_STATE:(SSM_GROUPS + g + 1) * SSM_STATE].astype(BF16)
        cb = _dot_nt(cm, bm.astype(BF16))
        h_prev = h_ref[g]
        y_off = _dot(cm, h_prev.astype(BF16))
        s_chunk = _dot(bm.T.astype(BF16), xw[:, g * gw:(g + 1) * gw])
        h_ref[g] = state_decay[:, g * gw:(g + 1) * gw] * h_prev + s_chunk
        pieces = []
        for r in range(hpg):
            hh = g * hpg + r
            seg = cs[:, hh:hh + 1] - cs_t[hh:hh + 1, :]
            dec = jnp.exp(jnp.where(allowed, seg, -jnp.inf))
            mix = (cb * dec * dt_t[hh:hh + 1, :]).astype(BF16)
            pieces.append(_dot(mix, xb[:, hh * SSM_HEAD_DIM:(hh + 1) * SSM_HEAD_DIM]))
        y_ref[0, :, g * gw:(g + 1) * gw] = (jnp.concatenate(pieces, axis=1)
                                           + y_off * off_scale[:, g * gw:(g + 1) * gw])


def _ssd_scan(xbc, dt2, a_rows, expand, *, n_lat, seq, n_ctx, bsz):
    t = xbc.shape[0]
    lc = SSM_CHUNK
    ctx_chunks, lat_chunks = n_ctx // lc, seq // lc
    gw = SSM_INNER // SSM_GROUPS

    def rowblk(b, d, s):
        in_ctx = s < ctx_chunks
        j_ctx = jnp.where(d == 0, s, ctx_chunks - 1 - s)
        sl = s - ctx_chunks
        j_lat = jnp.where(d == 0, sl, lat_chunks - 1 - sl)
        return jnp.where(in_ctx, (n_lat + b * n_ctx) // lc + j_ctx, b * lat_chunks + j_lat)

    return pl.pallas_call(
        _ssd_kernel,
        grid=(bsz, 2, ctx_chunks + lat_chunks),
        in_specs=[pl.BlockSpec((lc, SSM_INNER), lambda b, d, s: (rowblk(b, d, s), 0)),
                  pl.BlockSpec((lc, 2 * SSM_GROUPS * SSM_STATE),
                               lambda b, d, s: (rowblk(b, d, s), SSM_INNER // (2 * SSM_GROUPS * SSM_STATE))),
                  pl.BlockSpec((1, lc, 128), lambda b, d, s: (d, rowblk(b, d, s), 0)),
                  pl.BlockSpec((1, 1, 128), lambda b, d, s: (d, 0, 0)),
                  pl.BlockSpec((128, SSM_INNER), lambda b, d, s: (0, 0))],
        out_specs=pl.BlockSpec((1, lc, SSM_INNER), lambda b, d, s: (d, rowblk(b, d, s), 0)),
        out_shape=jax.ShapeDtypeStruct((2, t, SSM_INNER), F32),
        scratch_shapes=[pltpu.VMEM((SSM_GROUPS, SSM_STATE, gw), F32)],
        compiler_params=_cparams("arbitrary", "arbitrary", "arbitrary"),
        name="ssd_scan",
    )(xbc, xbc, dt2, a_rows, expand)


def _ssd_finish_kernel(yf_ref, yb_ref, xs_ref, z0_ref, z1_ref, dsk_ref, g_ref, o_ref):
    gw = SSM_INNER // SSM_GROUPS
    zs = (z0_ref, z1_ref)
    for g in range(SSM_GROUPS):
        sl = slice(g * gw, (g + 1) * gw)
        y = yf_ref[0, :, sl] + yb_ref[0, :, sl] + dsk_ref[:, sl] * xs_ref[:, sl]
        y = y * _silu(zs[g][...])
        o_ref[:, sl] = _head_norm(y, g_ref[:, sl]).astype(o_ref.dtype)


def _ssd_finish(ydir, xbc, p, dsk_row, g_row, *, z_col0):
    t = xbc.shape[0]
    tm = ROW_TILE
    gw = SSM_INNER // SSM_GROUPS
    assert z_col0 % gw == 0 and SSM_GROUPS == 2
    zb = z_col0 // gw
    return pl.pallas_call(
        _ssd_finish_kernel,
        grid=(t // tm,),
        in_specs=[pl.BlockSpec((1, tm, SSM_INNER), lambda i: (0, i, 0)),
                  pl.BlockSpec((1, tm, SSM_INNER), lambda i: (1, i, 0)),
                  pl.BlockSpec((tm, SSM_INNER), lambda i: (i, 0)),
                  pl.BlockSpec((tm, gw), lambda i: (i, zb)),
                  pl.BlockSpec((tm, gw), lambda i: (i, zb + 1)),
                  pl.BlockSpec((1, SSM_INNER), lambda i: (0, 0)),
                  pl.BlockSpec((1, SSM_INNER), lambda i: (0, 0))],
        out_specs=pl.BlockSpec((tm, SSM_INNER), lambda i: (i, 0)),
        out_shape=jax.ShapeDtypeStruct((t, SSM_INNER), BF16),
        compiler_params=_cparams("arbitrary"),
        name="ssd_finish",
    )(ydir, ydir, xbc, p, p, dsk_row, g_row)


def _window_kernel(qt_ref, k_ref, vt_ref, sink_ref, o_ref, *, tq, group, n_ctx, seq):
    band = tq + 2 * WINDOW
    pos_len = n_ctx + seq
    q0 = pl.program_id(2) * tq
    start = pl.multiple_of(jnp.minimum(n_ctx + q0 - WINDOW, pos_len - band), 128)
    qt = qt_ref[0, 0, 0]
    n = group * tq
    s_cx = _dot(k_ref[0, 0, 0:n_ctx, :], qt)
    s_w = _dot(k_ref[0, 0, pl.ds(start, band), :], qt)
    kpos = start - n_ctx + lax.broadcasted_iota(I32, (band, n), 0)
    qpos = q0 + lax.broadcasted_iota(I32, (band, n), 1) % tq
    valid = (jnp.abs(qpos - kpos) <= WINDOW) & (kpos >= 0)
    s_w = jnp.where(valid, s_w, -jnp.inf)
    sink = sink_ref[0]
    m = jnp.maximum(jnp.maximum(jnp.max(s_cx, axis=0, keepdims=True), jnp.max(s_w, axis=0, keepdims=True)), sink)
    p_cx = jnp.exp(s_cx - m)
    p_w = jnp.exp(s_w - m)
    l = jnp.sum(p_cx, axis=0, keepdims=True) + jnp.sum(p_w, axis=0, keepdims=True) + jnp.exp(sink - m)
    acc = _dot(vt_ref[0, 0, :, 0:n_ctx], p_cx.astype(BF16)) + _dot(vt_ref[0, 0, :, pl.ds(start, band)], p_w.astype(BF16))
    out = acc / l
    for h in range(group):
        o_ref[:, h * HEAD_DIM:(h + 1) * HEAD_DIM] = out[:, h * tq:(h + 1) * tq].T.astype(o_ref.dtype)


def _attention_c(qt, k, vt, sink_rows, *, n_lat, seq, n_ctx, bsz):
    n_kv, group, tq = k.shape[1], qt.shape[-1] // Q_TILE, Q_TILE
    pos_len = n_ctx + seq
    assert n_ctx >= WINDOW and n_ctx % 128 == 0 and pos_len >= tq + 2 * WINDOW
    return pl.pallas_call(
        functools.partial(_window_kernel, tq=tq, group=group, n_ctx=n_ctx, seq=seq),
        grid=(bsz, n_kv, seq // tq),
        in_specs=[pl.BlockSpec((1, 1, 1, HEAD_DIM, group * tq), lambda b, h, i: (b, h, n_ctx // tq + i, 0, 0)),
                  pl.BlockSpec((1, 1, pos_len, HEAD_DIM), lambda b, h, i: (b, h, 0, 0)),
                  pl.BlockSpec((1, 1, HEAD_DIM, pos_len), lambda b, h, i: (b, h, 0, 0)),
                  pl.BlockSpec((1, 1, group * tq), lambda b, h, i: (h, 0, 0))],
        out_specs=pl.BlockSpec((tq, group * HEAD_DIM), lambda b, h, i: (b * (seq // tq) + i, h)),
        out_shape=jax.ShapeDtypeStruct((n_lat, n_kv * group * HEAD_DIM), BF16),
        compiler_params=_cparams("arbitrary", "arbitrary", "arbitrary"),
        name="attn_c_window",
    )(qt, k, vt, sink_rows)


def _prep_na(p, gq, gk, *, col0, n_lat, seq, n_ctx, bsz):
    t = p.shape[0]
    tm = ROW_TILE
    hw = NA_HEADS * HEAD_DIM
    assert col0 % hw == 0 or (2 * col0) % hw == 0
    pos_len = n_ctx + seq
    batch_of, pos_of = _pos_maps(tm, n_lat, seq, n_ctx)
    lat_tiles, seq_tiles, ctx_tiles = n_lat // tm, seq // tm, n_ctx // tm

    def qpos_of(i):
        return jnp.where(i < lat_tiles, i % seq_tiles, seq_tiles + (i - lat_tiles) % ctx_tiles)

    cw = hw // 2
    cb = col0 // cw
    shp = jax.ShapeDtypeStruct((bsz, NA_HEADS, pos_len, HEAD_DIM), BF16)
    kv_spec = pl.BlockSpec((1, NA_HEADS, tm, HEAD_DIM), lambda i: (batch_of(i), 0, pos_of(i), 0))
    q_spec = pl.BlockSpec((1, NA_HEADS, tm, HEAD_DIM), lambda i: (batch_of(i), 0, qpos_of(i), 0))

    def kern(q0, q1, k0, k1, v0, v1, gq_ref, gk_ref, q_ref, k_ref, v_ref):
        halves = NA_HEADS // 2
        for h in range(NA_HEADS):
            sl = slice((h % halves) * HEAD_DIM, (h % halves + 1) * HEAD_DIM)
            pq, pk, pv = ((q0, k0, v0) if h < halves else (q1, k1, v1))
            q_ref[0, h] = (_head_norm(pq[:, sl], gq_ref[...]) * (HEAD_DIM ** -0.5)).astype(BF16)
            k_ref[0, h] = _head_norm(pk[:, sl], gk_ref[...]).astype(BF16)
            v_ref[0, h] = pv[:, sl].astype(BF16)

    def col_spec(j):
        return pl.BlockSpec((tm, cw), lambda i: (i, cb + j))

    return pl.pallas_call(
        kern,
        grid=(t // tm,),
        in_specs=[col_spec(j) for j in range(6)] + [pl.BlockSpec((1, HEAD_DIM), lambda i: (0, 0)),
                                                    pl.BlockSpec((1, HEAD_DIM), lambda i: (0, 0))],
        out_specs=[q_spec, kv_spec, kv_spec],
        out_shape=[shp, shp, shp],
        compiler_params=_cparams("arbitrary"),
        name="prep_na",
    )(p, p, p, p, p, p, gq, gk)


def _na_kernel(q_ref, k_ref, v_ref, bias_ref, o_ref, *, n_ctx, rows):
    tq = NA_TILE_ROWS * GRID_W
    win = NA_WIN_ROWS * GRID_W
    ti = pl.program_id(2)
    w0 = jnp.clip(ti * NA_TILE_ROWS - NA_ROWS // 2, 0, rows - NA_WIN_ROWS)
    start = pl.multiple_of(n_ctx + w0 * GRID_W, GRID_W)
    q = q_ref[0, 0]
    s_cx = _dot_nt(q, k_ref[0, 0, 0:n_ctx, :])
    s_nb = _dot_nt(q, k_ref[0, 0, pl.ds(start, win), :]) + bias_ref[0, 0]
    m = jnp.maximum(jnp.max(s_cx, axis=1, keepdims=True), jnp.max(s_nb, axis=1, keepdims=True))
    p_cx = jnp.exp(s_cx - m)
    p_nb = jnp.exp(s_nb - m)
    l = jnp.sum(p_cx, axis=1, keepdims=True) + jnp.sum(p_nb, axis=1, keepdims=True)
    acc = _dot(p_cx.astype(BF16), v_ref[0, 0, 0:n_ctx, :]) + _dot(p_nb.astype(BF16), v_ref[0, 0, pl.ds(start, win), :])
    o_ref[...] = (acc / l).astype(o_ref.dtype)


def _na_bias_tables(rpb, rows):
    n_tiles = rows // NA_TILE_ROWS
    tables = []
    for ti in (0, 1, n_tiles - 1):
        w0 = min(max(ti * NA_TILE_ROWS - NA_ROWS // 2, 0), rows - NA_WIN_ROWS)
        qr = ti * NA_TILE_ROWS + jnp.arange(NA_TILE_ROWS)
        r0 = jnp.clip(qr - NA_ROWS // 2, 0, rows - NA_ROWS)
        kr = w0 + jnp.arange(NA_WIN_ROWS)
        row_ok = (kr[None, :] >= r0[:, None]) & (kr[None, :] < r0[:, None] + NA_ROWS)
        dr = jnp.clip(kr[None, :] - qr[:, None] + NA_ROWS - 1, 0, 2 * NA_ROWS - 2)
        qc = jnp.arange(GRID_W)
        c0 = jnp.clip(qc - NA_COLS // 2, 0, GRID_W - NA_COLS)
        kc = jnp.arange(GRID_W)
        col_ok = (kc[None, :] >= c0[:, None]) & (kc[None, :] < c0[:, None] + NA_COLS)
        dc = jnp.clip(kc[None, :] - qc[:, None] + NA_COLS - 1, 0, 2 * NA_COLS - 2)
        sel_r = ((dr[:, :, None] == jnp.arange(2 * NA_ROWS - 1)) & row_ok[:, :, None]).astype(F32)
        sel_c = ((dc[:, :, None] == jnp.arange(2 * NA_COLS - 1)) & col_ok[:, :, None]).astype(F32)
        b = jnp.einsum("rki,hij,cqj->hrckq", sel_r, rpb.astype(F32), sel_c, precision=HI)
        ok = row_ok[:, None, :, None] & col_ok[None, :, None, :]
        b = jnp.where(ok[None], b, -jnp.inf)
        tables.append(b.reshape(rpb.shape[0], NA_TILE_ROWS * GRID_W, NA_WIN_ROWS * GRID_W))
    return jnp.stack(tables, axis=0).astype(F32)


def _attention_d(q, k, v, bias, *, n_lat, seq, n_ctx, bsz):
    rows = seq // GRID_W
    tq = NA_TILE_ROWS * GRID_W
    win = NA_WIN_ROWS * GRID_W
    n_tiles = rows // NA_TILE_ROWS
    pos_len = n_ctx + seq
    assert rows % NA_TILE_ROWS == 0 and rows >= NA_WIN_ROWS and n_tiles >= 3

    def kind(i):
        return jnp.where(i == 0, 0, jnp.where(i == n_tiles - 1, 2, 1))

    return pl.pallas_call(
        functools.partial(_na_kernel, n_ctx=n_ctx, rows=rows),
        grid=(bsz, NA_HEADS, n_tiles),
        in_specs=[pl.BlockSpec((1, 1, tq, HEAD_DIM), lambda b, h, i: (b, h, i, 0)),
                  pl.BlockSpec((1, 1, pos_len, HEAD_DIM), lambda b, h, i: (b, h, 0, 0)),
                  pl.BlockSpec((1, 1, pos_len, HEAD_DIM), lambda b, h, i: (b, h, 0, 0)),
                  pl.BlockSpec((1, 1, tq, win), lambda b, h, i: (kind(i), h, 0, 0))],
        out_specs=pl.BlockSpec((tq, HEAD_DIM), lambda b, h, i: (b * n_tiles + i, h)),
        out_shape=jax.ShapeDtypeStruct((n_lat, NA_HEADS * HEAD_DIM), BF16),
        compiler_params=_cparams("arbitrary", "arbitrary", "arbitrary"),
        name="attn_d_neighbourhood",
    )(q, k, v, bias)


def _pack_bf16_pairs(f):
    half = f.shape[1] // 2
    bits = lax.bitcast_convert_type(f.astype(BF16).astype(F32), U32)
    lo = lax.shift_right_logical(bits[:, :half], jnp.uint32(16))
    hi = bits[:, half:] & jnp.uint32(0xFFFF0000)
    return hi | lo


def _unpack_bf16_pairs(bits):
    lo = lax.bitcast_convert_type(lax.shift_left(bits, jnp.uint32(16)), F32).astype(BF16)
    hi = lax.bitcast_convert_type(bits & jnp.uint32(0xFFFF0000), F32).astype(BF16)
    return lo, hi


def _outproj_kernel(x_ref, y1_ref, y2_ref, w_ref, mod_ref, g_ref, wr_ref, xo_ref, fpk_ref, lg_ref):
    half = w_ref.shape[0] // 2
    delta = _dot(y1_ref[...], w_ref[0:half, :]) + _dot(y2_ref[...], w_ref[half:, :])
    xn = x_ref[...] + mod_ref[0, 2:3, :] * delta
    xo_ref[...] = xn
    f = _norm_mod(xn, g_ref[...], mod_ref[0, 3:4, :], mod_ref[0, 4:5, :])
    lg_ref[...] = lax.dot_general(wr_ref[...], f, (((1,), (1,)), ((), ())), precision=HI,
                                  preferred_element_type=F32)
    fpk_ref[...] = _pack_bf16_pairs(f)


def _outproj(x, y1, y2, w_bf16, mod, g, wr_t, *, n_rows, n_lat, seq, bsz):
    d = x.shape[1]
    tm = ROW_TILE
    hw = y1.shape[1]
    return pl.pallas_call(
        _outproj_kernel,
        grid=(n_rows // tm,),
        in_specs=[pl.BlockSpec((tm, d), lambda i: (i, 0)),
                  pl.BlockSpec((tm, hw), lambda i: (i, 0)),
                  pl.BlockSpec((tm, hw), lambda i: (i, 0)),
                  pl.BlockSpec((2 * hw, d), lambda i: (0, 0)),
                  pl.BlockSpec((1, 6, d), lambda i: (_group_of_tile(i, tm, n_lat, seq, bsz), 0, 0)),
                  pl.BlockSpec((1, d), lambda i: (0, 0)),
                  pl.BlockSpec((N_EXPERTS, d), lambda i: (0, 0))],
        out_specs=[pl.BlockSpec((tm, d), lambda i: (i, 0)),
                   pl.BlockSpec((tm, d // 2), lambda i: (i, 0)),
                   pl.BlockSpec((N_EXPERTS, tm), lambda i: (0, i))],
        out_shape=[jax.ShapeDtypeStruct((n_rows, d), F32),
                   jax.ShapeDtypeStruct((n_rows, d // 2), U32),
                   jax.ShapeDtypeStruct((N_EXPERTS, n_rows), F32)],
        compiler_params=_cparams("arbitrary"),
        name="outproj_residual_moe_in",
    )(x, y1, y2, w_bf16, mod, g, wr_t)


def _route_kernel(lg_ref, b_ref, e_ref, r_ref, w_ref, cnt_ref, run_ref):
    tr = lg_ref.shape[1]
    per = N_EXPERTS // N_GROUPS
    neg = -jnp.inf

    @pl.when(pl.program_id(0) == 0)
    def _():
        run_ref[...] = jnp.zeros(run_ref.shape, F32)

    scores = jax.nn.sigmoid(lg_ref[...])
    sel = scores + b_ref[...]
    sel3 = sel.reshape(N_GROUPS, per, tr)
    mem = lax.broadcasted_iota(I32, (N_GROUPS, per, tr), 1)
    m1 = jnp.max(sel3, axis=1, keepdims=True)
    i1 = jnp.min(jnp.where(sel3 == m1, mem, per), axis=1, keepdims=True)
    m2 = jnp.max(jnp.where(mem == i1, neg, sel3), axis=1, keepdims=True)
    gs = (m1 + m2).reshape(N_GROUPS, tr)
    gid = lax.broadcasted_iota(I32, (N_GROUPS, tr), 0)
    keep = jnp.zeros((N_GROUPS, tr), F32)
    for _ in range(TOPK_GROUPS):
        gm = jnp.max(gs, axis=0, keepdims=True)
        gi = jnp.min(jnp.where(gs == gm, gid, N_GROUPS), axis=0, keepdims=True)
        hit = gid == gi
        keep = jnp.where(hit, 1.0, keep)
        gs = jnp.where(hit, neg, gs)
    keep3 = jnp.broadcast_to(keep.reshape(N_GROUPS, 1, tr), (N_GROUPS, per, tr))
    cand = jnp.where(keep3 > 0.5, sel3, neg).reshape(N_EXPERTS, tr)
    eid = lax.broadcasted_iota(I32, (N_EXPERTS, tr), 0)
    idxs, ws = [], []
    sel_f = jnp.zeros((N_EXPERTS, tr), F32)
    for _ in range(TOP_K):
        cm = jnp.max(cand, axis=0, keepdims=True)
        ci = jnp.min(jnp.where(cand == cm, eid, N_EXPERTS), axis=0, keepdims=True)
        hit = eid == ci
        idxs.append(ci)
        ws.append(jnp.sum(jnp.where(hit, scores, 0.0), axis=0, keepdims=True))
        sel_f = jnp.where(hit, 1.0, sel_f)
        cand = jnp.where(hit, neg, cand)
    before = (lax.broadcasted_iota(I32, (tr, tr), 0) < lax.broadcasted_iota(I32, (tr, tr), 1))
    rank = _dot(sel_f.astype(BF16), jnp.where(before, 1.0, 0.0).astype(BF16)) + run_ref[...]
    run_ref[...] = run_ref[...] + jnp.sum(sel_f, axis=1, keepdims=True)
    cnt_ref[...] = run_ref[...]
    wsum = ws[0]
    for kk in range(1, TOP_K):
        wsum = wsum + ws[kk]
    for kk in range(TOP_K):
        e_ref[kk:kk + 1, :] = idxs[kk]
        r_ref[kk:kk + 1, :] = jnp.sum(jnp.where(eid == idxs[kk], rank, 0.0), axis=0, keepdims=True).astype(I32)
        w_ref[kk:kk + 1, :] = ws[kk] / wsum * ROUTE_SCALE


def _route(logits_t, b_col):
    n_e, t = logits_t.shape
    tr = ROW_TILE
    return pl.pallas_call(
        _route_kernel,
        grid=(t // tr,),
        in_specs=[pl.BlockSpec((n_e, tr), lambda i: (0, i)),
                  pl.BlockSpec((n_e, 1), lambda i: (0, 0))],
        out_specs=[pl.BlockSpec((TOP_K, tr), lambda i: (0, i)),
                   pl.BlockSpec((TOP_K, tr), lambda i: (0, i)),
                   pl.BlockSpec((TOP_K, tr), lambda i: (0, i)),
                   pl.BlockSpec((n_e, 1), lambda i: (0, 0))],
        out_shape=[jax.ShapeDtypeStruct((TOP_K, t), I32),
                   jax.ShapeDtypeStruct((TOP_K, t), I32),
                   jax.ShapeDtypeStruct((TOP_K, t), F32),
                   jax.ShapeDtypeStruct((n_e, 1), F32)],
        scratch_shapes=[pltpu.VMEM((n_e, 1), F32)],
        compiler_params=_cparams("arbitrary"),
        name="moe_route",
    )(logits_t, b_col)


def _scatter_kernel(dest_ref, f_ref, xs_ref, sem):
    tm = f_ref.shape[0]

    def copy(t, kk):
        return pltpu.make_async_copy(f_ref.at[pl.ds(t, 1)], xs_ref.at[pl.ds(dest_ref[0, kk, t], 1)], sem)

    def issue(t, c):
        for kk in range(TOP_K):
            copy(t, kk).start()
        return c

    def drain(t, c):
        for kk in range(TOP_K):
            copy(t, kk).wait()
        return c

    lax.fori_loop(0, tm, issue, 0)
    lax.fori_loop(0, tm, drain, 0)


def _scatter_rows(dest3, fpk, n_slots):
    t, hw = fpk.shape
    tm = ROW_TILE
    return pl.pallas_call(
        _scatter_kernel,
        grid=(t // tm,),
        in_specs=[pl.BlockSpec((1, TOP_K, tm), lambda i: (i, 0, 0), memory_space=pltpu.SMEM),
                  pl.BlockSpec((tm, hw), lambda i: (i, 0))],
        out_specs=pl.BlockSpec(memory_space=pl.ANY),
        out_shape=jax.ShapeDtypeStruct((n_slots, hw), U32),
        scratch_shapes=[pltpu.SemaphoreType.DMA(())],
        compiler_params=_cparams("arbitrary"),
        name="moe_scatter_rows",
    )(dest3, fpk)


def _ffn(bits, w1, w3, w2):
    half = bits.shape[1]
    lo, hi = _unpack_bf16_pairs(bits)
    h1 = _dot(lo, w1[0:half, :]) + _dot(hi, w1[half:, :])
    h3 = _dot(lo, w3[0:half, :]) + _dot(hi, w3[half:, :])
    return _dot((_silu(h1) * h3).astype(BF16), w2[...])


def _expert_kernel(be_ref, bv_ref, nu_ref, xs_ref, w1_ref, w3_ref, w2_ref, y_ref, w1b, w3b, w2b):
    i = pl.program_id(0)

    @pl.when(i < nu_ref[0])
    def _():
        e = be_ref[i]
        prev = be_ref[jnp.maximum(i - 1, 0)]

        @pl.when((i == 0) | (e != prev))
        def _():
            w1b[...] = w1_ref[0, 0].astype(BF16)
            w3b[...] = w3_ref[0, 0].astype(BF16)
            w2b[...] = w2_ref[0, 0].astype(BF16)

        rows = lax.broadcasted_iota(I32, xs_ref.shape, 0)
        bits = jnp.where(rows < bv_ref[i], xs_ref[...], jnp.uint32(0))
        y_ref[...] = _ffn(bits, w1b, w3b, w2b)


def _experts(blk_expert, blk_valid, n_used, xs, w1, w3, w2, layer):
    n_slots, hw = xs.shape
    d, de = w1.shape[2], w1.shape[3]
    tm = MOE_BLOCK
    grid_spec = pltpu.PrefetchScalarGridSpec(
        num_scalar_prefetch=3,
        grid=(n_slots // tm,),
        in_specs=[pl.BlockSpec((tm, hw), lambda i, be, bv, nu: (i, 0)),
                  pl.BlockSpec((1, 1, d, de), lambda i, be, bv, nu: (layer, be[i], 0, 0)),
                  pl.BlockSpec((1, 1, d, de), lambda i, be, bv, nu: (layer, be[i], 0, 0)),
                  pl.BlockSpec((1, 1, de, d), lambda i, be, bv, nu: (layer, be[i], 0, 0))],
        out_specs=pl.BlockSpec((tm, d), lambda i, be, bv, nu: (i, 0)),
        scratch_shapes=[pltpu.VMEM((d, de), BF16), pltpu.VMEM((d, de), BF16), pltpu.VMEM((de, d), BF16)],
    )
    return pl.pallas_call(
        _expert_kernel,
        grid_spec=grid_spec,
        out_shape=jax.ShapeDtypeStruct((n_slots, d), F32),
        compiler_params=_cparams("arbitrary"),
        name="moe_experts",
    )(blk_expert, blk_valid, n_used, xs, w1, w3, w2)


def _shared_kernel(f_ref, w1_ref, w3_ref, w2_ref, y_ref):
    y_ref[...] = _ffn(f_ref[...], w1_ref, w3_ref, w2_ref)


def _shared_expert(fpk, w1, w3, w2):
    t, hw = fpk.shape
    d, de = w1.shape
    tm = 2 * ROW_TILE
    return pl.pallas_call(
        _shared_kernel,
        grid=(t // tm,),
        in_specs=[pl.BlockSpec((tm, hw), lambda i: (i, 0)),
                  pl.BlockSpec((d, de), lambda i: (0, 0)),
                  pl.BlockSpec((d, de), lambda i: (0, 0)),
                  pl.BlockSpec((de, d), lambda i: (0, 0))],
        out_specs=pl.BlockSpec((tm, d), lambda i: (i, 0)),
        out_shape=jax.ShapeDtypeStruct((t, d), F32),
        compiler_params=_cparams("arbitrary"),
        name="moe_shared_expert",
    )(fpk, w1, w3, w2)


def _combine_kernel(dest_ref, x_ref, ysh_ref, w_ref, mod_ref, y_ref, o_ref, ybuf, sem):
    tm = x_ref.shape[0]

    def copy(t, kk):
        return pltpu.make_async_copy(y_ref.at[pl.ds(dest_ref[0, kk, t], 1)], ybuf.at[kk, pl.ds(t, 1)], sem)

    def issue(t, c):
        for kk in range(TOP_K):
            copy(t, kk).start()
        return c

    def drain(t, c):
        for kk in range(TOP_K):
            copy(t, kk).wait()
        return c

    lax.fori_loop(0, tm, issue, 0)
    lax.fori_loop(0, tm, drain, 0)
    acc = w_ref[:, 0:1] * ybuf[0]
    for kk in range(1, TOP_K):
        acc = acc + w_ref[:, kk:kk + 1] * ybuf[kk]
    o_ref[...] = x_ref[...] + mod_ref[0, 5:6, :] * (acc + ysh_ref[...])


def _combine(dest3, x, ysh, w_tk, mod, y, *, n_lat, seq, bsz):
    t, d = x.shape
    tm = dest3.shape[2]
    return pl.pallas_call(
        _combine_kernel,
        grid=(t // tm,),
        in_specs=[pl.BlockSpec((1, TOP_K, tm), lambda i: (i, 0, 0), memory_space=pltpu.SMEM),
                  pl.BlockSpec((tm, d), lambda i: (i, 0)),
                  pl.BlockSpec((tm, d), lambda i: (i, 0)),
                  pl.BlockSpec((tm, TOP_K), lambda i: (i, 0)),
                  pl.BlockSpec((1, 6, d), lambda i: (_group_of_tile(i, tm, n_lat, seq, bsz), 0, 0)),
                  pl.BlockSpec(memory_space=pl.ANY)],
        out_specs=pl.BlockSpec((tm, d), lambda i: (i, 0)),
        out_shape=jax.ShapeDtypeStruct((t, d), F32),
        scratch_shapes=[pltpu.VMEM((TOP_K, tm, d), F32), pltpu.SemaphoreType.DMA(())],
        compiler_params=_cparams("arbitrary"),
        name="moe_combine",
    )(dest3, x, ysh, w_tk, mod, y)


def _moe(x_new, fpk, logits_t, mod, b_router, w_e1, w_e3, w_e2, w_s1, w_s3, w_s2, *, layer, n_lat, seq, bsz):
    t = x_new.shape[0]
    top_e, top_r, top_w, counts = _route(logits_t, b_router.reshape(N_EXPERTS, 1))
    counts = counts[:, 0].astype(I32)
    blocks = (counts + MOE_BLOCK - 1) // MOE_BLOCK
    blk_end = jnp.cumsum(blocks)
    blk_start = blk_end - blocks
    n_blocks = -(-(t * TOP_K) // MOE_BLOCK) + N_EXPERTS
    n_slots = n_blocks * MOE_BLOCK
    slot_start = blk_start * MOE_BLOCK
    dest = jnp.sum(jnp.where(top_e[None] == jnp.arange(N_EXPERTS, dtype=I32)[:, None, None],
                             slot_start[:, None, None], 0), axis=0) + top_r
    bi = jnp.arange(n_blocks, dtype=I32)
    blk_expert = jnp.minimum(jnp.sum((bi[:, None] >= blk_end[None, :]).astype(I32), axis=1), N_EXPERTS - 1)
    blk_valid = jnp.clip(counts[blk_expert] - (bi - blk_start[blk_expert]) * MOE_BLOCK, 0, MOE_BLOCK)
    n_used = blk_end[-1:].astype(I32)

    tm_s = ROW_TILE
    dest_s = dest.reshape(TOP_K, t // tm_s, tm_s).transpose(1, 0, 2)
    xs = _scatter_rows(dest_s, fpk, n_slots)
    y = _experts(blk_expert.astype(I32), blk_valid.astype(I32), n_used, xs, w_e1, w_e3, w_e2, layer)
    ysh = _shared_expert(fpk, w_s1.astype(BF16), w_s3.astype(BF16), w_s2.astype(BF16))
    tm_c = ROW_TILE // 2
    dest_c = dest.reshape(TOP_K, t // tm_c, tm_c).transpose(1, 0, 2)
    return _combine(dest_c, x_new, ysh, top_w.T, mod, y, n_lat=n_lat, seq=seq, bsz=bsz)


def _pad_cols(w, n):
    return jnp.pad(w, ((0, 0), (0, n - w.shape[1])))


def kernel(x, c, ctx, c_ctx, w_ada, b_ada, g_mix, g_ffn, w_in_even, gq_a, gk_a, conv_w, conv_b, a_log, dt_bias,
           d_skip, g_ssm, w_in_odd, gq_c, gk_c, sink_c, gq_d, gk_d, rpb_d, w_out, w_router, b_router, w_e1, w_e3,
           w_e2, w_s1, w_s3, w_s2):
    bsz, seq, d = x.shape
    n_ctx = ctx.shape[1]
    depth = w_ada.shape[0]
    n_lat = bsz * seq
    geo = dict(n_lat=n_lat, seq=seq, n_ctx=n_ctx, bsz=bsz)
    xs = jnp.concatenate([x.reshape(n_lat, d), ctx.reshape(bsz * n_ctx, d)], axis=0)
    cvec = jnp.zeros((8, d), F32).at[:bsz].set(c).at[bsz].set(c_ctx)
    cos, sin = _rope_tables(seq, n_ctx)
    expand = jnp.zeros((128, SSM_INNER), F32).at[:SSM_HEADS].set(
        jnp.repeat(jnp.eye(SSM_HEADS, dtype=F32), SSM_HEAD_DIM, axis=1))

    for layer in range(depth):
        last = layer == depth - 1
        i = layer // 2
        mod = _adaln(cvec, w_ada, b_ada[layer].reshape(1, -1), layer)[:bsz + 1].reshape(bsz + 1, 6, d)
        g1 = g_mix[layer].reshape(1, d)
        if layer % 2 == 0:
            n_in = w_in_even.shape[2]
            n_pad = -(-n_in // 384) * 384
            p = _norm_mod_matmul(xs, g1, mod, _pad_cols(w_in_even[i], n_pad).astype(BF16),
                                 n_lat=n_lat, seq=seq, bsz=bsz, tn=n_pad // 3)
            qt, k, vt = _prep_rope(p, cos, sin, gq_a[i].reshape(1, -1), gk_a[i].reshape(1, -1),
                                   n_q=A_HEADS, n_kv=A_KV_HEADS, **geo)
            y1 = _attention_a(qt, k, vt, need_ctx=not last, **geo)
            xbc = _ssm_conv(p, conv_w[i], conv_b[i].reshape(1, -1), col0=A_IN + SSM_INNER,
                            n_lat=n_lat, seq=seq, n_ctx=n_ctx)
            dt_row = jnp.zeros((1, 128), F32).at[0, :2 * SSM_HEADS].set(dt_bias[i].reshape(-1))
            dt2 = _ssm_dt(p, dt_row, col0=A_IN + SSM_INNER + SSM_CONV_DIM)
            a_rows = jnp.zeros((2, 1, 128), F32).at[:, 0, :SSM_HEADS].set(-jnp.exp(a_log[i]))
            ydir = _ssd_scan(xbc, dt2, a_rows, expand, **geo)
            y2 = _ssd_finish(ydir, xbc, p, jnp.repeat(d_skip[i], SSM_HEAD_DIM).reshape(1, -1),
                             g_ssm[i].reshape(1, -1), z_col0=A_IN)
        else:
            p = _norm_mod_matmul(xs, g1, mod, w_in_odd[i].astype(BF16), n_lat=n_lat, seq=seq, bsz=bsz,
                                 tn=w_in_odd.shape[2] // 3)
            qt, k, vt = _prep_rope(p, cos, sin, gq_c[i].reshape(1, -1), gk_c[i].reshape(1, -1),
                                   n_q=A_HEADS, n_kv=A_KV_HEADS, **geo)
            group = A_HEADS // A_KV_HEADS
            sink_rows = jnp.repeat(sink_c[i].reshape(A_KV_HEADS, group), Q_TILE, axis=1).reshape(
                A_KV_HEADS, 1, group * Q_TILE)
            y1 = _attention_c(qt, k, vt, sink_rows, **geo)
            qd, kd, vd = _prep_na(p, gq_d[i].reshape(1, -1), gk_d[i].reshape(1, -1), col0=A_IN, **geo)
            y2 = _attention_d(qd, kd, vd, _na_bias_tables(rpb_d[i], seq // GRID_W), **geo)
        n_rows = n_lat if last else xs.shape[0]
        x_new, fpk, logits_t = _outproj(xs, y1, y2, w_out[layer].astype(BF16), mod, g_ffn[layer].reshape(1, d),
                                        w_router[layer].T, n_rows=n_rows, n_lat=n_lat, seq=seq, bsz=bsz)
        xs = _moe(x_new, fpk, logits_t, mod, b_router[layer], w_e1, w_e3, w_e2,
                  w_s1[layer], w_s3[layer], w_s2[layer], layer=layer, n_lat=n_lat, seq=seq, bsz=bsz)
    return xs[:n_lat].reshape(bsz, seq, d)
```

```python
import functools
import math

import jax
import jax.numpy as jnp
from jax import lax
from jax.experimental import pallas as pl
from jax.experimental.pallas import tpu as pltpu

F32 = jnp.float32
BF16 = jnp.bfloat16
I32 = jnp.int32
U32 = jnp.uint32

EPS = 1e-6
HEAD_DIM = 128
GRID_W = 64
ROPE_THETA = 10000.0
WINDOW = 128
NA_ROWS = 8
NA_COLS = 16
NA_TILE_ROWS = 8
NA_WIN_ROWS = 16
A_HEADS = 8
A_KV_HEADS = 2
SSM_HEADS = 16
SSM_HEAD_DIM = 64
SSM_INNER = SSM_HEADS * SSM_HEAD_DIM
SSM_GROUPS = 2
SSM_STATE = 128
SSM_CONV = 5
SSM_CONV_DIM = SSM_INNER + 2 * SSM_GROUPS * SSM_STATE
SSM_CHUNK = 128
A_IN = (A_HEADS + 2 * A_KV_HEADS) * HEAD_DIM
NA_HEADS = 8
N_EXPERTS = 64
TOP_K = 8
N_GROUPS = 8
TOPK_GROUPS = 4
ROUTE_SCALE = 2.5
MOE_BLOCK = 256

VMEM_LIMIT_BYTES = 56 * 1024 * 1024
ROW_TILE = 256
Q_TILE = 256
VT_ROWS = HEAD_DIM + 16
LOG2E = 1.4426950408889634
HI = lax.Precision.HIGHEST


def _cparams(*sem):
    return pltpu.CompilerParams(dimension_semantics=sem, vmem_limit_bytes=VMEM_LIMIT_BYTES)


def _silu(x):
    return x * jax.nn.sigmoid(x)


def _hi_dot(a, b):
    return jnp.dot(a, b, precision=HI, preferred_element_type=F32)


def _dot(a, b):
    return jnp.dot(a, b, preferred_element_type=F32)


def _dot_nt(a, b):
    return lax.dot_general(a, b, (((1,), (1,)), ((), ())), preferred_element_type=F32)


def _largest_tile(n, cap, mult=128):
    best = None
    for t in range(mult, cap + 1, mult):
        if n % t == 0:
            best = t
    assert best is not None, (n, cap)
    return best


def _adaln_kernel(c_ref, w_ref, b_ref, o_ref):
    o_ref[...] = _hi_dot(_silu(c_ref[...]), w_ref[0]) + b_ref[...]


def _adaln(cvec, w, b, layer):
    _, d, n = w.shape
    tn = 1024
    return pl.pallas_call(
        _adaln_kernel,
        grid=(n // tn,),
        in_specs=[pl.BlockSpec((8, d), lambda j: (0, 0)),
                  pl.BlockSpec((1, d, tn), lambda j: (layer, 0, j)),
                  pl.BlockSpec((1, tn), lambda j: (0, j))],
        out_specs=pl.BlockSpec((8, tn), lambda j: (0, j)),
        out_shape=jax.ShapeDtypeStruct((8, n), F32),
        compiler_params=_cparams("arbitrary"),
        name="adaln",
    )(cvec, w, b)


def _norm_mod(x, g, shift, scale):
    ms = jnp.mean(x * x, axis=-1, keepdims=True)
    return (x * lax.rsqrt(ms + EPS) * g) * (1.0 + scale) + shift


def _nmm_kernel(x_ref, g_ref, mod_ref, w_ref, o_ref, xn_ref):
    @pl.when(pl.program_id(1) == 0)
    def _():
        xn_ref[...] = _norm_mod(x_ref[...], g_ref[...], mod_ref[0, 0:1, :], mod_ref[0, 1:2, :]).astype(BF16)

    o_ref[...] = _dot(xn_ref[...], w_ref[...])


def _group_of_tile(i, tm, n_lat, seq, bsz):
    return jnp.where(i * tm < n_lat, (i * tm) // seq, bsz)


def _norm_mod_matmul(x, g, mod, w_bf16, *, n_lat, seq, bsz, tn):
    t, d = x.shape
    n = w_bf16.shape[1]
    tm = 2 * ROW_TILE
    assert t % tm == 0 and n % tn == 0 and n_lat % tm == 0
    return pl.pallas_call(
        _nmm_kernel,
        grid=(t // tm, n // tn),
        in_specs=[pl.BlockSpec((tm, d), lambda i, j: (i, 0)),
                  pl.BlockSpec((1, d), lambda i, j: (0, 0)),
                  pl.BlockSpec((1, 6, d), lambda i, j: (_group_of_tile(i, tm, n_lat, seq, bsz), 0, 0)),
                  pl.BlockSpec((d, tn), lambda i, j: (0, j))],
        out_specs=pl.BlockSpec((tm, tn), lambda i, j: (i, j)),
        out_shape=jax.ShapeDtypeStruct((t, n), F32),
        scratch_shapes=[pltpu.VMEM((tm, d), BF16)],
        compiler_params=_cparams("arbitrary", "arbitrary"),
        name="norm_mod_inproj",
    )(x, g, mod, w_bf16)


def _head_norm(x, g):
    ms = jnp.mean(x * x, axis=-1, keepdims=True)
    return x * lax.rsqrt(ms + EPS) * g


def _prep_rope_kernel(p_ref, cos_ref, sin_ref, gq_ref, gk_ref, qt_ref, k_ref, vt_ref, *, n_q, n_kv, tq, q_scale):
    cos = cos_ref[...]
    sin = sin_ref[...]
    lane = lax.broadcasted_iota(I32, cos.shape, 1)
    first_half = (lane % 64) < 32

    def norm_rope(x, g):
        y = _head_norm(x, g)
        swapped = jnp.where(first_half, pltpu.roll(y, 96, 1), pltpu.roll(y, 32, 1))
        return y * cos + swapped * sin

    group = n_q // n_kv
    for h in range(n_q):
        q = norm_rope(p_ref[:, h * HEAD_DIM:(h + 1) * HEAD_DIM], gq_ref[...]) * q_scale
        kvh, hh = divmod(h, group)
        qt_ref[0, kvh, 0, :, hh * tq:(hh + 1) * tq] = q.T.astype(BF16)
    for h in range(n_kv):
        c0 = (n_q + h) * HEAD_DIM
        k_ref[0, h] = norm_rope(p_ref[:, c0:c0 + HEAD_DIM], gk_ref[...]).astype(BF16)
        c1 = (n_q + n_kv + h) * HEAD_DIM
        vt_ref[0, h, 0:HEAD_DIM, :] = p_ref[:, c1:c1 + HEAD_DIM].T.astype(BF16)
        vt_ref[0, h, HEAD_DIM:VT_ROWS, :] = jnp.ones((VT_ROWS - HEAD_DIM, tq), BF16)


def _pos_maps(tm, n_lat, seq, n_ctx):
    lat_tiles, seq_tiles, ctx_tiles = n_lat // tm, seq // tm, n_ctx // tm

    def batch_of(i):
        return jnp.where(i < lat_tiles, i // seq_tiles, (i - lat_tiles) // ctx_tiles)

    def pos_of(i):
        return jnp.where(i < lat_tiles, ctx_tiles + i % seq_tiles, (i - lat_tiles) % ctx_tiles)

    return batch_of, pos_of


def _prep_rope(p, cos, sin, gq, gk, *, q_scale, n_q, n_kv, n_lat, seq, n_ctx, bsz):
    t = p.shape[0]
    tm = Q_TILE
    pos_len = n_ctx + seq
    group = n_q // n_kv
    width = (n_q + 2 * n_kv) * HEAD_DIM
    batch_of, pos_of = _pos_maps(tm, n_lat, seq, n_ctx)
    kern = functools.partial(_prep_rope_kernel, n_q=n_q, n_kv=n_kv, tq=tm, q_scale=q_scale)
    return pl.pallas_call(
        kern,
        grid=(t // tm,),
        in_specs=[pl.BlockSpec((tm, width), lambda i: (i, 0)),
                  pl.BlockSpec((tm, HEAD_DIM), lambda i: (pos_of(i), 0)),
                  pl.BlockSpec((tm, HEAD_DIM), lambda i: (pos_of(i), 0)),
                  pl.BlockSpec((1, HEAD_DIM), lambda i: (0, 0)),
                  pl.BlockSpec((1, HEAD_DIM), lambda i: (0, 0))],
        out_specs=[pl.BlockSpec((1, n_kv, 1, HEAD_DIM, group * tm), lambda i: (batch_of(i), 0, pos_of(i), 0, 0)),
                   pl.BlockSpec((1, n_kv, tm, HEAD_DIM), lambda i: (batch_of(i), 0, pos_of(i), 0)),
                   pl.BlockSpec((1, n_kv, VT_ROWS, tm), lambda i: (batch_of(i), 0, 0, pos_of(i)))],
        out_shape=[jax.ShapeDtypeStruct((bsz, n_kv, pos_len // tm, HEAD_DIM, group * tm), BF16),
                   jax.ShapeDtypeStruct((bsz, n_kv, pos_len, HEAD_DIM), BF16),
                   jax.ShapeDtypeStruct((bsz, n_kv, VT_ROWS, pos_len), BF16)],
        compiler_params=_cparams("arbitrary"),
        name="prep_rope",
    )(p, cos, sin, gq, gk)


def _rope_tables(seq, n_ctx):
    t = jnp.arange(seq, dtype=I32)
    row = (t // GRID_W).astype(F32)
    col = (t % GRID_W).astype(F32)
    n_freq = HEAD_DIM // 4
    inv = ROPE_THETA ** (-jnp.arange(n_freq, dtype=F32) / n_freq)
    ar, ac = row[:, None] * inv, col[:, None] * inv
    cos = jnp.concatenate([jnp.cos(ar), jnp.cos(ar), jnp.cos(ac), jnp.cos(ac)], axis=1)
    sin = jnp.concatenate([-jnp.sin(ar), jnp.sin(ar), -jnp.sin(ac), jnp.sin(ac)], axis=1)
    cos = jnp.concatenate([jnp.ones((n_ctx, HEAD_DIM), F32), cos], axis=0)
    sin = jnp.concatenate([jnp.zeros((n_ctx, HEAD_DIM), F32), sin], axis=0)
    return cos, sin


def _flash_t_kernel(qt_ref, k_ref, vt_ref, o_ref, m_ref, l_ref, acc_ref, sa_ref, sb_ref, *, tk, n_k, tq, group):
    qt = qt_ref[0, 0, 0]
    m_ref[...] = jnp.full(m_ref.shape, -jnp.inf, F32)
    l_ref[...] = jnp.zeros(l_ref.shape, F32)
    acc_ref[...] = jnp.zeros(acc_ref.shape, F32)

    def scores(j, dst):
        off = pl.multiple_of(jnp.minimum(j, n_k - 1) * tk, tk)
        dst[...] = _dot(k_ref[0, 0, pl.ds(off, tk), :], qt)

    def consume(j, src):
        off = pl.multiple_of(j * tk, tk)
        s = src[...]
        m_old = m_ref[...]
        m_new = jnp.maximum(m_old, jnp.max(s, axis=0, keepdims=True))
        alpha = jnp.exp2(m_old - m_new)
        p = jnp.exp2(s - m_new).astype(BF16)
        r = _dot(vt_ref[0, 0, :, pl.ds(off, tk)], p)
        acc_ref[...] = alpha * acc_ref[...] + r[0:HEAD_DIM]
        l_ref[...] = alpha * l_ref[...] + r[HEAD_DIM:HEAD_DIM + 1]
        m_ref[...] = m_new

    scores(0, sa_ref)

    def body(jj, carry):
        j = 2 * jj
        scores(j + 1, sb_ref)
        consume(j, sa_ref)
        scores(j + 2, sa_ref)
        consume(j + 1, sb_ref)
        return carry

    lax.fori_loop(0, n_k // 2, body, 0)
    if n_k % 2:
        consume(n_k - 1, sa_ref)
    out = acc_ref[...] / l_ref[...]
    for h in range(group):
        o_ref[:, h * HEAD_DIM:(h + 1) * HEAD_DIM] = out[:, h * tq:(h + 1) * tq].T.astype(o_ref.dtype)


def _attention_a(qt, k, vt, *, n_lat, seq, n_ctx, bsz, need_ctx):
    n_kv, group, tq = k.shape[1], qt.shape[-1] // Q_TILE, Q_TILE
    pos_len = n_ctx + seq
    width = n_kv * group * HEAD_DIM
    tk = _largest_tile(pos_len, 768)

    def scratch(tkk):
        n = group * tq
        return [pltpu.VMEM((1, n), F32), pltpu.VMEM((1, n), F32), pltpu.VMEM((HEAD_DIM, n), F32),
                pltpu.VMEM((tkk, n), F32), pltpu.VMEM((tkk, n), F32)]

    y = pl.pallas_call(
        functools.partial(_flash_t_kernel, tk=tk, n_k=pos_len // tk, tq=tq, group=group),
        grid=(bsz, n_kv, seq // tq),
        in_specs=[pl.BlockSpec((1, 1, 1, HEAD_DIM, group * tq), lambda b, h, i: (b, h, n_ctx // tq + i, 0, 0)),
                  pl.BlockSpec((1, 1, pos_len, HEAD_DIM), lambda b, h, i: (b, h, 0, 0)),
                  pl.BlockSpec((1, 1, VT_ROWS, pos_len), lambda b, h, i: (b, h, 0, 0))],
        out_specs=pl.BlockSpec((tq, group * HEAD_DIM), lambda b, h, i: (b * (seq // tq) + i, h)),
        out_shape=jax.ShapeDtypeStruct((n_lat, width), BF16),
        scratch_shapes=scratch(tk),
        compiler_params=_cparams("arbitrary", "arbitrary", "arbitrary"),
        name="attn_a_latent",
    )(qt, k, vt)
    if not need_ctx:
        return y
    tkc = _largest_tile(n_ctx, 768)
    y_ctx = pl.pallas_call(
        functools.partial(_flash_t_kernel, tk=tkc, n_k=n_ctx // tkc, tq=tq, group=group),
        grid=(bsz, n_kv, n_ctx // tq),
        in_specs=[pl.BlockSpec((1, 1, 1, HEAD_DIM, group * tq), lambda b, h, i: (b, h, i, 0, 0)),
                  pl.BlockSpec((1, 1, n_ctx, HEAD_DIM), lambda b, h, i: (b, h, 0, 0)),
                  pl.BlockSpec((1, 1, VT_ROWS, n_ctx), lambda b, h, i: (b, h, 0, 0))],
        out_specs=pl.BlockSpec((tq, group * HEAD_DIM), lambda b, h, i: (b * (n_ctx // tq) + i, h)),
        out_shape=jax.ShapeDtypeStruct((bsz * n_ctx, width), BF16),
        scratch_shapes=scratch(tkc),
        compiler_params=_cparams("arbitrary", "arbitrary", "arbitrary"),
        name="attn_a_context",
    )(qt, k, vt)
    return jnp.concatenate([y, y_ctx], axis=0)


def _segment_edges(i, tm, n_lat, seq, n_ctx):
    lat_tiles, seq_tiles, ctx_tiles = n_lat // tm, seq // tm, n_ctx // tm
    is_lat = i < lat_tiles
    pos = jnp.where(is_lat, i % seq_tiles, (i - lat_tiles) % ctx_tiles)
    last = jnp.where(is_lat, seq_tiles, ctx_tiles) - 1
    return pos == 0, pos == last


def _conv_kernel(cur_ref, prev_ref, next_ref, w_ref, b_ref, o_ref, ext_ref, *, tm, n_lat, seq, n_ctx):
    first, last = _segment_edges(pl.program_id(0), tm, n_lat, seq, n_ctx)
    ext_ref[0:8, :] = jnp.where(first, 0.0, prev_ref[...])
    ext_ref[8:8 + tm, :] = cur_ref[...]
    ext_ref[8 + tm:16 + tm, :] = jnp.where(last, 0.0, next_ref[...])
    acc = jnp.broadcast_to(b_ref[...], (tm, b_ref.shape[1]))
    half = SSM_CONV // 2
    for kk in range(SSM_CONV):
        acc = acc + w_ref[kk:kk + 1, :] * ext_ref[pl.ds(8 - half + kk, tm), :]
    o_ref[...] = _silu(acc)


def _ssm_conv(p, conv_w, conv_b, *, col0, n_lat, seq, n_ctx):
    t = p.shape[0]
    tm, tc = ROW_TILE, 512
    assert col0 % tc == 0 and SSM_CONV_DIM % tc == 0
    cb0 = col0 // tc
    r8 = tm // 8
    kern = functools.partial(_conv_kernel, tm=tm, n_lat=n_lat, seq=seq, n_ctx=n_ctx)
    return pl.pallas_call(
        kern,
        grid=(t // tm, SSM_CONV_DIM // tc),
        in_specs=[pl.BlockSpec((tm, tc), lambda i, j: (i, cb0 + j)),
                  pl.BlockSpec((8, tc), lambda i, j: (jnp.maximum(i * r8 - 1, 0), cb0 + j)),
                  pl.BlockSpec((8, tc), lambda i, j: (jnp.minimum((i + 1) * r8, t // 8 - 1), cb0 + j)),
                  pl.BlockSpec((SSM_CONV, tc), lambda i, j: (0, j)),
                  pl.BlockSpec((1, tc), lambda i, j: (0, j))],
        out_specs=pl.BlockSpec((tm, tc), lambda i, j: (i, j)),
        out_shape=jax.ShapeDtypeStruct((t, SSM_CONV_DIM), F32),
        scratch_shapes=[pltpu.VMEM((tm + 16, tc), F32)],
        compiler_params=_cparams("arbitrary", "arbitrary"),
        name="ssm_conv",
    )(p, p, p, conv_w, conv_b)


def _dt_kernel(p_ref, b_ref, o_ref):
    x = p_ref[...] + b_ref[...]
    sp = jnp.maximum(x, 0.0) + jnp.log1p(jnp.exp(-jnp.abs(x)))
    o_ref[0] = sp
    o_ref[1] = pltpu.roll(sp, HEAD_DIM - SSM_HEADS, 1)


def _ssm_dt(p, bias_row, *, col0):
    t = p.shape[0]
    tm = 2 * ROW_TILE
    assert col0 % 128 == 0
    return pl.pallas_call(
        _dt_kernel,
        grid=(t // tm,),
        in_specs=[pl.BlockSpec((tm, 128), lambda i: (i, col0 // 128)),
                  pl.BlockSpec((1, 128), lambda i: (0, 0))],
        out_specs=pl.BlockSpec((2, tm, 128), lambda i: (0, i, 0)),
        out_shape=jax.ShapeDtypeStruct((2, t, 128), F32),
        compiler_params=_cparams("arbitrary"),
        name="ssm_dt",
    )(p, bias_row)


def _ssd_kernel(x_ref, bc_ref, dt_ref, a_ref, e_ref, y_ref, h_ref):
    d = pl.program_id(1)
    lc = SSM_CHUNK
    gw = SSM_INNER // SSM_GROUPS
    hpg = SSM_HEADS // SSM_GROUPS

    @pl.when(pl.program_id(2) == 0)
    def _():
        h_ref[...] = jnp.zeros(h_ref.shape, F32)

    dt = dt_ref[0]
    w = dt * a_ref[0]
    row = lax.broadcasted_iota(I32, (lc, lc), 0)
    col = lax.broadcasted_iota(I32, (lc, lc), 1)
    fwd = d == 0
    sgn = jnp.where(fwd, 1, -1)
    allowed = (row - col) * sgn >= 0
    allowed_t = (col - row) * sgn >= 0
    cs = _hi_dot(allowed.astype(F32), w)
    cs_t = _hi_dot(w.T, allowed_t.astype(F32))
    dt_t = dt.T
    tot = jnp.where(fwd, cs[lc - 1:lc, :], cs[0:1, :])
    e = e_ref[...]

    def expand(a):
        hi = a.astype(BF16)
        lo = (a - hi.astype(F32)).astype(BF16)
        return _dot(hi, e) + _dot(lo, e)

    x = x_ref[...]
    xw = (x * expand(jnp.exp(tot - cs) * dt)).astype(BF16)
    off_scale = expand(jnp.exp(cs))
    state_decay = expand(jnp.exp(jnp.broadcast_to(tot, (lc, 128))))
    xb = x.astype(BF16)
    for g in range(SSM_GROUPS):
        bm = bc_ref[:, g * SSM_STATE:(g + 1) * SSM_STATE]
        cm = bc_ref[:, (SSM_GROUPS + g) * SSM_STATE:(SSM_GROUPS + g + 1) * SSM_STATE].astype(BF16)
        cb = _dot_nt(cm, bm.astype(BF16))
        h_prev = h_ref[g]
        y_off = _dot(cm, h_prev.astype(BF16))
        s_chunk = _dot(bm.T.astype(BF16), xw[:, g * gw:(g + 1) * gw])
        h_ref[g] = state_decay[:, g * gw:(g + 1) * gw] * h_prev + s_chunk
        pieces = []
        for r in range(hpg):
            hh = g * hpg + r
            seg = cs[:, hh:hh + 1] - cs_t[hh:hh + 1, :]
            dec = jnp.exp(jnp.where(allowed, seg, -jnp.inf))
            mix = (cb * dec * dt_t[hh:hh + 1, :]).astype(BF16)
            pieces.append(_dot(mix, xb[:, hh * SSM_HEAD_DIM:(hh + 1) * SSM_HEAD_DIM]))
        y_ref[0, :, g * gw:(g + 1) * gw] = (jnp.concatenate(pieces, axis=1)
                                           + y_off * off_scale[:, g * gw:(g + 1) * gw])


def _ssd_scan(xbc, dt2, a_rows, expand, *, n_lat, seq, n_ctx, bsz):
    t = xbc.shape[0]
    lc = SSM_CHUNK
    ctx_chunks, lat_chunks = n_ctx // lc, seq // lc
    gw = SSM_INNER // SSM_GROUPS

    def rowblk(b, d, s):
        in_ctx = s < ctx_chunks
        j_ctx = jnp.where(d == 0, s, ctx_chunks - 1 - s)
        sl = s - ctx_chunks
        j_lat = jnp.where(d == 0, sl, lat_chunks - 1 - sl)
        return jnp.where(in_ctx, (n_lat + b * n_ctx) // lc + j_ctx, b * lat_chunks + j_lat)

    return pl.pallas_call(
        _ssd_kernel,
        grid=(bsz, 2, ctx_chunks + lat_chunks),
        in_specs=[pl.BlockSpec((lc, SSM_INNER), lambda b, d, s: (rowblk(b, d, s), 0)),
                  pl.BlockSpec((lc, 2 * SSM_GROUPS * SSM_STATE),
                               lambda b, d, s: (rowblk(b, d, s), SSM_INNER // (2 * SSM_GROUPS * SSM_STATE))),
                  pl.BlockSpec((1, lc, 128), lambda b, d, s: (d, rowblk(b, d, s), 0)),
                  pl.BlockSpec((1, 1, 128), lambda b, d, s: (d, 0, 0)),
                  pl.BlockSpec((128, SSM_INNER), lambda b, d, s: (0, 0))],
        out_specs=pl.BlockSpec((1, lc, SSM_INNER), lambda b, d, s: (d, rowblk(b, d, s), 0)),
        out_shape=jax.ShapeDtypeStruct((2, t, SSM_INNER), F32),
        scratch_shapes=[pltpu.VMEM((SSM_GROUPS, SSM_STATE, gw), F32)],
        compiler_params=_cparams("arbitrary", "arbitrary", "arbitrary"),
        name="ssd_scan",
    )(xbc, xbc, dt2, a_rows, expand)


def _ssd_finish_kernel(yf_ref, yb_ref, xs_ref, z0_ref, z1_ref, dsk_ref, g_ref, o_ref):
    gw = SSM_INNER // SSM_GROUPS
    zs = (z0_ref, z1_ref)
    for g in range(SSM_GROUPS):
        sl = slice(g * gw, (g + 1) * gw)
        y = yf_ref[0, :, sl] + yb_ref[0, :, sl] + dsk_ref[:, sl] * xs_ref[:, sl]
        y = y * _silu(zs[g][...])
        o_ref[:, sl] = _head_norm(y, g_ref[:, sl]).astype(o_ref.dtype)


def _ssd_finish(ydir, xbc, p, dsk_row, g_row, *, z_col0):
    t = xbc.shape[0]
    tm = ROW_TILE
    gw = SSM_INNER // SSM_GROUPS
    assert z_col0 % gw == 0 and SSM_GROUPS == 2
    zb = z_col0 // gw
    return pl.pallas_call(
        _ssd_finish_kernel,
        grid=(t // tm,),
        in_specs=[pl.BlockSpec((1, tm, SSM_INNER), lambda i: (0, i, 0)),
                  pl.BlockSpec((1, tm, SSM_INNER), lambda i: (1, i, 0)),
                  pl.BlockSpec((tm, SSM_INNER), lambda i: (i, 0)),
                  pl.BlockSpec((tm, gw), lambda i: (i, zb)),
                  pl.BlockSpec((tm, gw), lambda i: (i, zb + 1)),
                  pl.BlockSpec((1, SSM_INNER), lambda i: (0, 0)),
                  pl.BlockSpec((1, SSM_INNER), lambda i: (0, 0))],
        out_specs=pl.BlockSpec((tm, SSM_INNER), lambda i: (i, 0)),
        out_shape=jax.ShapeDtypeStruct((t, SSM_INNER), BF16),
        compiler_params=_cparams("arbitrary"),
        name="ssd_finish",
    )(ydir, ydir, xbc, p, p, dsk_row, g_row)


def _window_kernel(qt_ref, k_ref, vt_ref, sink_ref, o_ref, *, tq, group, n_ctx, seq):
    band = tq + 2 * WINDOW
    pos_len = n_ctx + seq
    q0 = pl.program_id(2) * tq
    start = pl.multiple_of(jnp.minimum(n_ctx + q0 - WINDOW, pos_len - band), 128)
    qt = qt_ref[0, 0, 0]
    n = group * tq
    s_cx = _dot(k_ref[0, 0, 0:n_ctx, :], qt)
    s_w = _dot(k_ref[0, 0, pl.ds(start, band), :], qt)
    kpos = start - n_ctx + lax.broadcasted_iota(I32, (band, n), 0)
    qpos = q0 + lax.broadcasted_iota(I32, (band, n), 1) % tq
    valid = (jnp.abs(qpos - kpos) <= WINDOW) & (kpos >= 0)
    s_w = jnp.where(valid, s_w, -jnp.inf)
    sink = sink_ref[0]
    m = jnp.maximum(jnp.maximum(jnp.max(s_cx, axis=0, keepdims=True), jnp.max(s_w, axis=0, keepdims=True)), sink)
    p_cx = jnp.exp(s_cx - m)
    p_w = jnp.exp(s_w - m)
    l = jnp.sum(p_cx, axis=0, keepdims=True) + jnp.sum(p_w, axis=0, keepdims=True) + jnp.exp(sink - m)
    acc = (_dot(vt_ref[0, 0, 0:HEAD_DIM, 0:n_ctx], p_cx.astype(BF16))
           + _dot(vt_ref[0, 0, 0:HEAD_DIM, pl.ds(start, band)], p_w.astype(BF16)))
    out = acc / l
    for h in range(group):
        o_ref[:, h * HEAD_DIM:(h + 1) * HEAD_DIM] = out[:, h * tq:(h + 1) * tq].T.astype(o_ref.dtype)


def _attention_c(qt, k, vt, sink_rows, *, n_lat, seq, n_ctx, bsz):
    n_kv, group, tq = k.shape[1], qt.shape[-1] // Q_TILE, Q_TILE
    pos_len = n_ctx + seq
    assert n_ctx >= WINDOW and n_ctx % 128 == 0 and pos_len >= tq + 2 * WINDOW
    return pl.pallas_call(
        functools.partial(_window_kernel, tq=tq, group=group, n_ctx=n_ctx, seq=seq),
        grid=(bsz, n_kv, seq // tq),
        in_specs=[pl.BlockSpec((1, 1, 1, HEAD_DIM, group * tq), lambda b, h, i: (b, h, n_ctx // tq + i, 0, 0)),
                  pl.BlockSpec((1, 1, pos_len, HEAD_DIM), lambda b, h, i: (b, h, 0, 0)),
                  pl.BlockSpec((1, 1, VT_ROWS, pos_len), lambda b, h, i: (b, h, 0, 0)),
                  pl.BlockSpec((1, 1, group * tq), lambda b, h, i: (h, 0, 0))],
        out_specs=pl.BlockSpec((tq, group * HEAD_DIM), lambda b, h, i: (b * (seq // tq) + i, h)),
        out_shape=jax.ShapeDtypeStruct((n_lat, n_kv * group * HEAD_DIM), BF16),
        compiler_params=_cparams("arbitrary", "arbitrary", "arbitrary"),
        name="attn_c_window",
    )(qt, k, vt, sink_rows)


def _prep_na(p, gq, gk, *, col0, n_lat, seq, n_ctx, bsz):
    t = p.shape[0]
    tm = ROW_TILE
    hw = NA_HEADS * HEAD_DIM
    assert col0 % hw == 0 or (2 * col0) % hw == 0
    pos_len = n_ctx + seq
    batch_of, pos_of = _pos_maps(tm, n_lat, seq, n_ctx)
    lat_tiles, seq_tiles, ctx_tiles = n_lat // tm, seq // tm, n_ctx // tm

    def qpos_of(i):
        return jnp.where(i < lat_tiles, i % seq_tiles, seq_tiles + (i - lat_tiles) % ctx_tiles)

    cw = hw // 2
    cb = col0 // cw
    shp = jax.ShapeDtypeStruct((bsz, NA_HEADS, pos_len, HEAD_DIM), BF16)
    kv_spec = pl.BlockSpec((1, NA_HEADS, tm, HEAD_DIM), lambda i: (batch_of(i), 0, pos_of(i), 0))
    q_spec = pl.BlockSpec((1, NA_HEADS, tm, HEAD_DIM), lambda i: (batch_of(i), 0, qpos_of(i), 0))

    def kern(q0, q1, k0, k1, v0, v1, gq_ref, gk_ref, q_ref, k_ref, v_ref):
        halves = NA_HEADS // 2
        for h in range(NA_HEADS):
            sl = slice((h % halves) * HEAD_DIM, (h % halves + 1) * HEAD_DIM)
            pq, pk, pv = ((q0, k0, v0) if h < halves else (q1, k1, v1))
            q_ref[0, h] = (_head_norm(pq[:, sl], gq_ref[...]) * (HEAD_DIM ** -0.5)).astype(BF16)
            k_ref[0, h] = _head_norm(pk[:, sl], gk_ref[...]).astype(BF16)
            v_ref[0, h] = pv[:, sl].astype(BF16)

    def col_spec(j):
        return pl.BlockSpec((tm, cw), lambda i: (i, cb + j))

    return pl.pallas_call(
        kern,
        grid=(t // tm,),
        in_specs=[col_spec(j) for j in range(6)] + [pl.BlockSpec((1, HEAD_DIM), lambda i: (0, 0)),
                                                    pl.BlockSpec((1, HEAD_DIM), lambda i: (0, 0))],
        out_specs=[q_spec, kv_spec, kv_spec],
        out_shape=[shp, shp, shp],
        compiler_params=_cparams("arbitrary"),
        name="prep_na",
    )(p, p, p, p, p, p, gq, gk)


def _na_kernel(q_ref, k_ref, v_ref, bias_ref, o_ref, *, n_ctx, rows):
    tq = NA_TILE_ROWS * GRID_W
    win = NA_WIN_ROWS * GRID_W
    ti = pl.program_id(2)
    w0 = jnp.clip(ti * NA_TILE_ROWS - NA_ROWS // 2, 0, rows - NA_WIN_ROWS)
    start = pl.multiple_of(n_ctx + w0 * GRID_W, GRID_W)
    q = q_ref[0, 0]
    s_cx = _dot_nt(q, k_ref[0, 0, 0:n_ctx, :])
    s_nb = _dot_nt(q, k_ref[0, 0, pl.ds(start, win), :]) + bias_ref[0, 0]
    m = jnp.maximum(jnp.max(s_cx, axis=1, keepdims=True), jnp.max(s_nb, axis=1, keepdims=True))
    p_cx = jnp.exp(s_cx - m)
    p_nb = jnp.exp(s_nb - m)
    l = jnp.sum(p_cx, axis=1, keepdims=True) + jnp.sum(p_nb, axis=1, keepdims=True)
    acc = _dot(p_cx.astype(BF16), v_ref[0, 0, 0:n_ctx, :]) + _dot(p_nb.astype(BF16), v_ref[0, 0, pl.ds(start, win), :])
    o_ref[...] = (acc / l).astype(o_ref.dtype)


def _na_bias_tables(rpb, rows):
    n_tiles = rows // NA_TILE_ROWS
    tables = []
    for ti in (0, 1, n_tiles - 1):
        w0 = min(max(ti * NA_TILE_ROWS - NA_ROWS // 2, 0), rows - NA_WIN_ROWS)
        qr = ti * NA_TILE_ROWS + jnp.arange(NA_TILE_ROWS)
        r0 = jnp.clip(qr - NA_ROWS // 2, 0, rows - NA_ROWS)
        kr = w0 + jnp.arange(NA_WIN_ROWS)
        row_ok = (kr[None, :] >= r0[:, None]) & (kr[None, :] < r0[:, None] + NA_ROWS)
        dr = jnp.clip(kr[None, :] - qr[:, None] + NA_ROWS - 1, 0, 2 * NA_ROWS - 2)
        qc = jnp.arange(GRID_W)
        c0 = jnp.clip(qc - NA_COLS // 2, 0, GRID_W - NA_COLS)
        kc = jnp.arange(GRID_W)
        col_ok = (kc[None, :] >= c0[:, None]) & (kc[None, :] < c0[:, None] + NA_COLS)
        dc = jnp.clip(kc[None, :] - qc[:, None] + NA_COLS - 1, 0, 2 * NA_COLS - 2)
        sel_r = ((dr[:, :, None] == jnp.arange(2 * NA_ROWS - 1)) & row_ok[:, :, None]).astype(F32)
        sel_c = ((dc[:, :, None] == jnp.arange(2 * NA_COLS - 1)) & col_ok[:, :, None]).astype(F32)
        b = jnp.einsum("rki,hij,cqj->hrckq", sel_r, rpb.astype(F32), sel_c, precision=HI)
        ok = row_ok[:, None, :, None] & col_ok[None, :, None, :]
        b = jnp.where(ok[None], b, -jnp.inf)
        tables.append(b.reshape(rpb.shape[0], NA_TILE_ROWS * GRID_W, NA_WIN_ROWS * GRID_W))
    return jnp.stack(tables, axis=0).astype(F32)


def _attention_d(q, k, v, bias, *, n_lat, seq, n_ctx, bsz):
    rows = seq // GRID_W
    tq = NA_TILE_ROWS * GRID_W
    win = NA_WIN_ROWS * GRID_W
    n_tiles = rows // NA_TILE_ROWS
    pos_len = n_ctx + seq
    assert rows % NA_TILE_ROWS == 0 and rows >= NA_WIN_ROWS and n_tiles >= 3

    def kind(i):
        return jnp.where(i == 0, 0, jnp.where(i == n_tiles - 1, 2, 1))

    return pl.pallas_call(
        functools.partial(_na_kernel, n_ctx=n_ctx, rows=rows),
        grid=(bsz, NA_HEADS, n_tiles),
        in_specs=[pl.BlockSpec((1, 1, tq, HEAD_DIM), lambda b, h, i: (b, h, i, 0)),
                  pl.BlockSpec((1, 1, pos_len, HEAD_DIM), lambda b, h, i: (b, h, 0, 0)),
                  pl.BlockSpec((1, 1, pos_len, HEAD_DIM), lambda b, h, i: (b, h, 0, 0)),
                  pl.BlockSpec((1, 1, tq, win), lambda b, h, i: (kind(i), h, 0, 0))],
        out_specs=pl.BlockSpec((tq, HEAD_DIM), lambda b, h, i: (b * n_tiles + i, h)),
        out_shape=jax.ShapeDtypeStruct((n_lat, NA_HEADS * HEAD_DIM), BF16),
        compiler_params=_cparams("arbitrary", "arbitrary", "arbitrary"),
        name="attn_d_neighbourhood",
    )(q, k, v, bias)


def _pack_bf16_pairs(f):
    half = f.shape[1] // 2
    bits = lax.bitcast_convert_type(f.astype(BF16).astype(F32), U32)
    lo = lax.shift_right_logical(bits[:, :half], jnp.uint32(16))
    hi = bits[:, half:] & jnp.uint32(0xFFFF0000)
    return hi | lo


def _unpack_bf16_pairs(bits):
    lo = lax.bitcast_convert_type(lax.shift_left(bits, jnp.uint32(16)), F32).astype(BF16)
    hi = lax.bitcast_convert_type(bits & jnp.uint32(0xFFFF0000), F32).astype(BF16)
    return lo, hi


def _outproj_kernel(x_ref, y1_ref, y2_ref, w_ref, mod_ref, g_ref, wrh_ref, wrl_ref, xo_ref, fpk_ref, lg_ref):
    half = w_ref.shape[0] // 2
    delta = _dot(y1_ref[...], w_ref[0:half, :]) + _dot(y2_ref[...], w_ref[half:, :])
    xn = x_ref[...] + mod_ref[0, 2:3, :] * delta
    xo_ref[...] = xn
    f = _norm_mod(xn, g_ref[...], mod_ref[0, 3:4, :], mod_ref[0, 4:5, :])
    f_hi = f.astype(BF16)
    f_lo = (f - f_hi.astype(F32)).astype(BF16)
    lg_ref[...] = _dot_nt(wrh_ref[...], f_hi) + (_dot_nt(wrl_ref[...], f_hi) + _dot_nt(wrh_ref[...], f_lo))
    fpk_ref[...] = _pack_bf16_pairs(f)


def _outproj(x, y1, y2, w_bf16, mod, g, wr_t, *, n_rows, n_lat, seq, bsz):
    d = x.shape[1]
    tm = ROW_TILE
    hw = y1.shape[1]
    wr_hi = wr_t.astype(BF16)
    wr_lo = (wr_t - wr_hi.astype(F32)).astype(BF16)
    return pl.pallas_call(
        _outproj_kernel,
        grid=(n_rows // tm,),
        in_specs=[pl.BlockSpec((tm, d), lambda i: (i, 0)),
                  pl.BlockSpec((tm, hw), lambda i: (i, 0)),
                  pl.BlockSpec((tm, hw), lambda i: (i, 0)),
                  pl.BlockSpec((2 * hw, d), lambda i: (0, 0)),
                  pl.BlockSpec((1, 6, d), lambda i: (_group_of_tile(i, tm, n_lat, seq, bsz), 0, 0)),
                  pl.BlockSpec((1, d), lambda i: (0, 0)),
                  pl.BlockSpec((N_EXPERTS, d), lambda i: (0, 0)),
                  pl.BlockSpec((N_EXPERTS, d), lambda i: (0, 0))],
        out_specs=[pl.BlockSpec((tm, d), lambda i: (i, 0)),
                   pl.BlockSpec((tm, d // 2), lambda i: (i, 0)),
                   pl.BlockSpec((N_EXPERTS, tm), lambda i: (0, i))],
        out_shape=[jax.ShapeDtypeStruct((n_rows, d), F32),
                   jax.ShapeDtypeStruct((n_rows, d // 2), U32),
                   jax.ShapeDtypeStruct((N_EXPERTS, n_rows), F32)],
        compiler_params=_cparams("arbitrary"),
        name="outproj_residual_moe_in",
    )(x, y1, y2, w_bf16, mod, g, wr_hi, wr_lo)


def _route_kernel(lg_ref, b_ref, e_ref, r_ref, w_ref, cnt_ref, run_ref):
    tr = lg_ref.shape[1]
    per = N_EXPERTS // N_GROUPS
    neg = -jnp.inf

    @pl.when(pl.program_id(0) == 0)
    def _():
        run_ref[...] = jnp.zeros(run_ref.shape, F32)

    scores = jax.nn.sigmoid(lg_ref[...])
    sel = scores + b_ref[...]
    sel3 = sel.reshape(N_GROUPS, per, tr)
    mem = lax.broadcasted_iota(I32, (N_GROUPS, per, tr), 1)
    m1 = jnp.max(sel3, axis=1, keepdims=True)
    i1 = jnp.min(jnp.where(sel3 == m1, mem, per), axis=1, keepdims=True)
    m2 = jnp.max(jnp.where(mem == i1, neg, sel3), axis=1, keepdims=True)
    gs = (m1 + m2).reshape(N_GROUPS, tr)
    gid = lax.broadcasted_iota(I32, (N_GROUPS, tr), 0)
    keep = jnp.zeros((N_GROUPS, tr), F32)
    for _ in range(TOPK_GROUPS):
        gm = jnp.max(gs, axis=0, keepdims=True)
        gi = jnp.min(jnp.where(gs == gm, gid, N_GROUPS), axis=0, keepdims=True)
        hit = gid == gi
        keep = jnp.where(hit, 1.0, keep)
        gs = jnp.where(hit, neg, gs)
    keep3 = jnp.broadcast_to(keep.reshape(N_GROUPS, 1, tr), (N_GROUPS, per, tr))
    cand = jnp.where(keep3 > 0.5, sel3, neg).reshape(N_EXPERTS, tr)
    eid = lax.broadcasted_iota(I32, (N_EXPERTS, tr), 0)
    idxs, ws = [], []
    sel_f = jnp.zeros((N_EXPERTS, tr), F32)
    for _ in range(TOP_K):
        cm = jnp.max(cand, axis=0, keepdims=True)
        ci = jnp.min(jnp.where(cand == cm, eid, N_EXPERTS), axis=0, keepdims=True)
        hit = eid == ci
        idxs.append(ci)
        ws.append(jnp.sum(jnp.where(hit, scores, 0.0), axis=0, keepdims=True))
        sel_f = jnp.where(hit, 1.0, sel_f)
        cand = jnp.where(hit, neg, cand)
    before = (lax.broadcasted_iota(I32, (tr, tr), 0) < lax.broadcasted_iota(I32, (tr, tr), 1))
    rank = _dot(sel_f.astype(BF16), jnp.where(before, 1.0, 0.0).astype(BF16)) + run_ref[...]
    run_ref[...] = run_ref[...] + jnp.sum(sel_f, axis=1, keepdims=True)
    cnt_ref[...] = run_ref[...]
    wsum = ws[0]
    for kk in range(1, TOP_K):
        wsum = wsum + ws[kk]
    for kk in range(TOP_K):
        e_ref[kk:kk + 1, :] = idxs[kk]
        r_ref[kk:kk + 1, :] = jnp.sum(jnp.where(eid == idxs[kk], rank, 0.0), axis=0, keepdims=True).astype(I32)
        w_ref[kk:kk + 1, :] = ws[kk] / wsum * ROUTE_SCALE


def _route(logits_t, b_col):
    n_e, t = logits_t.shape
    tr = ROW_TILE
    return pl.pallas_call(
        _route_kernel,
        grid=(t // tr,),
        in_specs=[pl.BlockSpec((n_e, tr), lambda i: (0, i)),
                  pl.BlockSpec((n_e, 1), lambda i: (0, 0))],
        out_specs=[pl.BlockSpec((TOP_K, tr), lambda i: (0, i)),
                   pl.BlockSpec((TOP_K, tr), lambda i: (0, i)),
                   pl.BlockSpec((TOP_K, tr), lambda i: (0, i)),
                   pl.BlockSpec((n_e, 1), lambda i: (0, 0))],
        out_shape=[jax.ShapeDtypeStruct((TOP_K, t), I32),
                   jax.ShapeDtypeStruct((TOP_K, t), I32),
                   jax.ShapeDtypeStruct((TOP_K, t), F32),
                   jax.ShapeDtypeStruct((n_e, 1), F32)],
        scratch_shapes=[pltpu.VMEM((n_e, 1), F32)],
        compiler_params=_cparams("arbitrary"),
        name="moe_route",
    )(logits_t, b_col)


def _scatter_kernel(dest_ref, f_ref, xs_ref, sem):
    tm = f_ref.shape[0]

    def copy(t, kk):
        return pltpu.make_async_copy(f_ref.at[pl.ds(t, 1)], xs_ref.at[pl.ds(dest_ref[0, kk, t], 1)], sem)

    def issue(t, c):
        for kk in range(TOP_K):
            copy(t, kk).start()
        return c

    def drain(t, c):
        for kk in range(TOP_K):
            copy(t, kk).wait()
        return c

    lax.fori_loop(0, tm, issue, 0)
    lax.fori_loop(0, tm, drain, 0)


def _scatter_rows(dest3, fpk, n_slots):
    t, hw = fpk.shape
    tm = ROW_TILE
    return pl.pallas_call(
        _scatter_kernel,
        grid=(t // tm,),
        in_specs=[pl.BlockSpec((1, TOP_K, tm), lambda i: (i, 0, 0), memory_space=pltpu.SMEM),
                  pl.BlockSpec((tm, hw), lambda i: (i, 0))],
        out_specs=pl.BlockSpec(memory_space=pl.ANY),
        out_shape=jax.ShapeDtypeStruct((n_slots, hw), U32),
        scratch_shapes=[pltpu.SemaphoreType.DMA(())],
        compiler_params=_cparams("arbitrary"),
        name="moe_scatter_rows",
    )(dest3, fpk)


def _ffn(bits, w1, w3, w2):
    half = bits.shape[1]
    lo, hi = _unpack_bf16_pairs(bits)
    h1 = _dot(lo, w1[0:half, :]) + _dot(hi, w1[half:, :])
    h3 = _dot(lo, w3[0:half, :]) + _dot(hi, w3[half:, :])
    return _dot((_silu(h1) * h3).astype(BF16), w2[...])


def _expert_kernel(be_ref, bv_ref, nu_ref, xs_ref, w1_ref, w3_ref, w2_ref, y_ref, w1b, w3b, w2b):
    i = pl.program_id(0)

    @pl.when(i < nu_ref[0])
    def _():
        e = be_ref[i]
        prev = be_ref[jnp.maximum(i - 1, 0)]

        @pl.when((i == 0) | (e != prev))
        def _():
            w1b[...] = w1_ref[0, 0].astype(BF16)
            w3b[...] = w3_ref[0, 0].astype(BF16)
            w2b[...] = w2_ref[0, 0].astype(BF16)

        rows = lax.broadcasted_iota(I32, xs_ref.shape, 0)
        bits = jnp.where(rows < bv_ref[i], xs_ref[...], jnp.uint32(0))
        y_ref[...] = _ffn(bits, w1b, w3b, w2b)


def _experts(blk_expert, blk_valid, n_used, xs, w1, w3, w2, layer):
    n_slots, hw = xs.shape
    d, de = w1.shape[2], w1.shape[3]
    tm = MOE_BLOCK
    grid_spec = pltpu.PrefetchScalarGridSpec(
        num_scalar_prefetch=3,
        grid=(n_slots // tm,),
        in_specs=[pl.BlockSpec((tm, hw), lambda i, be, bv, nu: (i, 0)),
                  pl.BlockSpec((1, 1, d, de), lambda i, be, bv, nu: (layer, be[i], 0, 0)),
                  pl.BlockSpec((1, 1, d, de), lambda i, be, bv, nu: (layer, be[i], 0, 0)),
                  pl.BlockSpec((1, 1, de, d), lambda i, be, bv, nu: (layer, be[i], 0, 0))],
        out_specs=pl.BlockSpec((tm, d), lambda i, be, bv, nu: (i, 0)),
        scratch_shapes=[pltpu.VMEM((d, de), BF16), pltpu.VMEM((d, de), BF16), pltpu.VMEM((de, d), BF16)],
    )
    return pl.pallas_call(
        _expert_kernel,
        grid_spec=grid_spec,
        out_shape=jax.ShapeDtypeStruct((n_slots, d), F32),
        compiler_params=_cparams("arbitrary"),
        name="moe_experts",
    )(blk_expert, blk_valid, n_used, xs, w1, w3, w2)


def _shared_kernel(f_ref, w1_ref, w3_ref, w2_ref, y_ref):
    y_ref[...] = _ffn(f_ref[...], w1_ref, w3_ref, w2_ref)


def _shared_expert(fpk, w1, w3, w2):
    t, hw = fpk.shape
    d, de = w1.shape
    tm = 2 * ROW_TILE
    return pl.pallas_call(
        _shared_kernel,
        grid=(t // tm,),
        in_specs=[pl.BlockSpec((tm, hw), lambda i: (i, 0)),
                  pl.BlockSpec((d, de), lambda i: (0, 0)),
                  pl.BlockSpec((d, de), lambda i: (0, 0)),
                  pl.BlockSpec((de, d), lambda i: (0, 0))],
        out_specs=pl.BlockSpec((tm, d), lambda i: (i, 0)),
        out_shape=jax.ShapeDtypeStruct((t, d), F32),
        compiler_params=_cparams("arbitrary"),
        name="moe_shared_expert",
    )(fpk, w1, w3, w2)


def _combine_kernel(dest_ref, x_ref, ysh_ref, w_ref, mod_ref, y_ref, o_ref, ybuf, sem):
    tm = x_ref.shape[0]

    def copy(t, kk):
        return pltpu.make_async_copy(y_ref.at[pl.ds(dest_ref[0, kk, t], 1)], ybuf.at[kk, pl.ds(t, 1)], sem)

    def issue(t, c):
        for kk in range(TOP_K):
            copy(t, kk).start()
        return c

    def drain(t, c):
        for kk in range(TOP_K):
            copy(t, kk).wait()
        return c

    lax.fori_loop(0, tm, issue, 0)
    lax.fori_loop(0, tm, drain, 0)
    acc = w_ref[:, 0:1] * ybuf[0]
    for kk in range(1, TOP_K):
        acc = acc + w_ref[:, kk:kk + 1] * ybuf[kk]
    o_ref[...] = x_ref[...] + mod_ref[0, 5:6, :] * (acc + ysh_ref[...])


def _combine(dest3, x, ysh, w_tk, mod, y, *, n_lat, seq, bsz):
    t, d = x.shape
    tm = dest3.shape[2]
    return pl.pallas_call(
        _combine_kernel,
        grid=(t // tm,),
        in_specs=[pl.BlockSpec((1, TOP_K, tm), lambda i: (i, 0, 0), memory_space=pltpu.SMEM),
                  pl.BlockSpec((tm, d), lambda i: (i, 0)),
                  pl.BlockSpec((tm, d), lambda i: (i, 0)),
                  pl.BlockSpec((tm, TOP_K), lambda i: (i, 0)),
                  pl.BlockSpec((1, 6, d), lambda i: (_group_of_tile(i, tm, n_lat, seq, bsz), 0, 0)),
                  pl.BlockSpec(memory_space=pl.ANY)],
        out_specs=pl.BlockSpec((tm, d), lambda i: (i, 0)),
        out_shape=jax.ShapeDtypeStruct((t, d), F32),
        scratch_shapes=[pltpu.VMEM((TOP_K, tm, d), F32), pltpu.SemaphoreType.DMA(())],
        compiler_params=_cparams("arbitrary"),
        name="moe_combine",
    )(dest3, x, ysh, w_tk, mod, y)


def _moe(x_new, fpk, logits_t, mod, b_router, w_e1, w_e3, w_e2, w_s1, w_s3, w_s2, *, layer, n_lat, seq, bsz):
    t = x_new.shape[0]
    top_e, top_r, top_w, counts = _route(logits_t, b_router.reshape(N_EXPERTS, 1))
    counts = counts[:, 0].astype(I32)
    blocks = (counts + MOE_BLOCK - 1) // MOE_BLOCK
    blk_end = jnp.cumsum(blocks)
    blk_start = blk_end - blocks
    n_blocks = -(-(t * TOP_K) // MOE_BLOCK) + N_EXPERTS
    n_slots = n_blocks * MOE_BLOCK
    slot_start = blk_start * MOE_BLOCK
    dest = jnp.sum(jnp.where(top_e[None] == jnp.arange(N_EXPERTS, dtype=I32)[:, None, None],
                             slot_start[:, None, None], 0), axis=0) + top_r
    bi = jnp.arange(n_blocks, dtype=I32)
    blk_expert = jnp.minimum(jnp.sum((bi[:, None] >= blk_end[None, :]).astype(I32), axis=1), N_EXPERTS - 1)
    blk_valid = jnp.clip(counts[blk_expert] - (bi - blk_start[blk_expert]) * MOE_BLOCK, 0, MOE_BLOCK)
    n_used = blk_end[-1:].astype(I32)

    tm_s = ROW_TILE
    dest_s = dest.reshape(TOP_K, t // tm_s, tm_s).transpose(1, 0, 2)
    xs = _scatter_rows(dest_s, fpk, n_slots)
    y = _experts(blk_expert.astype(I32), blk_valid.astype(I32), n_used, xs, w_e1, w_e3, w_e2, layer)
    ysh = _shared_expert(fpk, w_s1.astype(BF16), w_s3.astype(BF16), w_s2.astype(BF16))
    tm_c = ROW_TILE // 2
    dest_c = dest.reshape(TOP_K, t // tm_c, tm_c).transpose(1, 0, 2)
    return _combine(dest_c, x_new, ysh, top_w.T, mod, y, n_lat=n_lat, seq=seq, bsz=bsz)


def _pad_cols(w, n):
    return jnp.pad(w, ((0, 0), (0, n - w.shape[1])))


def kernel(x, c, ctx, c_ctx, w_ada, b_ada, g_mix, g_ffn, w_in_even, gq_a, gk_a, conv_w, conv_b, a_log, dt_bias,
           d_skip, g_ssm, w_in_odd, gq_c, gk_c, sink_c, gq_d, gk_d, rpb_d, w_out, w_router, b_router, w_e1, w_e3,
           w_e2, w_s1, w_s3, w_s2):
    bsz, seq, d = x.shape
    n_ctx = ctx.shape[1]
    depth = w_ada.shape[0]
    n_lat = bsz * seq
    geo = dict(n_lat=n_lat, seq=seq, n_ctx=n_ctx, bsz=bsz)
    xs = jnp.concatenate([x.reshape(n_lat, d), ctx.reshape(bsz * n_ctx, d)], axis=0)
    cvec = jnp.zeros((8, d), F32).at[:bsz].set(c).at[bsz].set(c_ctx)
    cos, sin = _rope_tables(seq, n_ctx)
    expand = jnp.zeros((128, SSM_INNER), BF16).at[:SSM_HEADS].set(
        jnp.repeat(jnp.eye(SSM_HEADS, dtype=BF16), SSM_HEAD_DIM, axis=1))

    for layer in range(depth):
        last = layer == depth - 1
        i = layer // 2
        mod = _adaln(cvec, w_ada, b_ada[layer].reshape(1, -1), layer)[:bsz + 1].reshape(bsz + 1, 6, d)
        g1 = g_mix[layer].reshape(1, d)
        if layer % 2 == 0:
            n_in = w_in_even.shape[2]
            n_pad = -(-n_in // 384) * 384
            p = _norm_mod_matmul(xs, g1, mod, _pad_cols(w_in_even[i], n_pad).astype(BF16),
                                 n_lat=n_lat, seq=seq, bsz=bsz, tn=n_pad // 3)
            qt, k, vt = _prep_rope(p, cos, sin, gq_a[i].reshape(1, -1), gk_a[i].reshape(1, -1),
                                   q_scale=HEAD_DIM ** -0.5 * LOG2E, n_q=A_HEADS, n_kv=A_KV_HEADS, **geo)
            y1 = _attention_a(qt, k, vt, need_ctx=not last, **geo)
            xbc = _ssm_conv(p, conv_w[i], conv_b[i].reshape(1, -1), col0=A_IN + SSM_INNER,
                            n_lat=n_lat, seq=seq, n_ctx=n_ctx)
            dt_row = jnp.zeros((1, 128), F32).at[0, :2 * SSM_HEADS].set(dt_bias[i].reshape(-1))
            dt2 = _ssm_dt(p, dt_row, col0=A_IN + SSM_INNER + SSM_CONV_DIM)
            a_rows = jnp.zeros((2, 1, 128), F32).at[:, 0, :SSM_HEADS].set(-jnp.exp(a_log[i]))
            ydir = _ssd_scan(xbc, dt2, a_rows, expand, **geo)
            y2 = _ssd_finish(ydir, xbc, p, jnp.repeat(d_skip[i], SSM_HEAD_DIM).reshape(1, -1),
                             g_ssm[i].reshape(1, -1), z_col0=A_IN)
        else:
            p = _norm_mod_matmul(xs, g1, mod, w_in_odd[i].astype(BF16), n_lat=n_lat, seq=seq, bsz=bsz,
                                 tn=w_in_odd.shape[2] // 3)
            qt, k, vt = _prep_rope(p, cos, sin, gq_c[i].reshape(1, -1), gk_c[i].reshape(1, -1),
                                   q_scale=HEAD_DIM ** -0.5, n_q=A_HEADS, n_kv=A_KV_HEADS, **geo)
            group = A_HEADS // A_KV_HEADS
            sink_rows = jnp.repeat(sink_c[i].reshape(A_KV_HEADS, group), Q_TILE, axis=1).reshape(
                A_KV_HEADS, 1, group * Q_TILE)
            y1 = _attention_c(qt, k, vt, sink_rows, **geo)
            qd, kd, vd = _prep_na(p, gq_d[i].reshape(1, -1), gk_d[i].reshape(1, -1), col0=A_IN, **geo)
            y2 = _attention_d(qd, kd, vd, _na_bias_tables(rpb_d[i], seq // GRID_W), **geo)
        n_rows = n_lat if last else xs.shape[0]
        x_new, fpk, logits_t = _outproj(xs, y1, y2, w_out[layer].astype(BF16), mod, g_ffn[layer].reshape(1, d),
                                        w_router[layer].T, n_rows=n_rows, n_lat=n_lat, seq=seq, bsz=bsz)
        xs = _moe(x_new, fpk, logits_t, mod, b_router[layer], w_e1, w_e3, w_e2,
                  w_s1[layer], w_s3[layer], w_s2[layer], layer=layer, n_lat=n_lat, seq=seq, bsz=bsz)
    return xs[:n_lat].reshape(bsz, seq, d)
```

```python
import functools
import math

import jax
import jax.numpy as jnp
from jax import lax
from jax.experimental import pallas as pl
from jax.experimental.pallas import tpu as pltpu

F32 = jnp.float32
BF16 = jnp.bfloat16
I32 = jnp.int32
U32 = jnp.uint32

EPS = 1e-6
HEAD_DIM = 128
GRID_W = 64
ROPE_THETA = 10000.0
WINDOW = 128
NA_ROWS = 8
NA_COLS = 16
NA_TILE_ROWS = 8
NA_WIN_ROWS = 16
A_HEADS = 8
A_KV_HEADS = 2
SSM_HEADS = 16
SSM_HEAD_DIM = 64
SSM_INNER = SSM_HEADS * SSM_HEAD_DIM
SSM_GROUPS = 2
SSM_STATE = 128
SSM_CONV = 5
SSM_CONV_DIM = SSM_INNER + 2 * SSM_GROUPS * SSM_STATE
SSM_CHUNK = 128
A_IN = (A_HEADS + 2 * A_KV_HEADS) * HEAD_DIM
NA_HEADS = 8
N_EXPERTS = 64
TOP_K = 8
N_GROUPS = 8
TOPK_GROUPS = 4
ROUTE_SCALE = 2.5
MOE_BLOCK = 256

VMEM_LIMIT_BYTES = 56 * 1024 * 1024
ROW_TILE = 256
Q_TILE = 256
VT_ROWS = HEAD_DIM + 16
LOG2E = 1.4426950408889634
HI = lax.Precision.HIGHEST


def _cparams(*sem):
    return pltpu.CompilerParams(dimension_semantics=sem, vmem_limit_bytes=VMEM_LIMIT_BYTES)


def _silu(x):
    return x * jax.nn.sigmoid(x)


def _hi_dot(a, b):
    return jnp.dot(a, b, precision=HI, preferred_element_type=F32)


def _dot(a, b):
    return jnp.dot(a, b, preferred_element_type=F32)


def _dot_nt(a, b):
    return lax.dot_general(a, b, (((1,), (1,)), ((), ())), preferred_element_type=F32)


def _largest_tile(n, cap, mult=128):
    best = None
    for t in range(mult, cap + 1, mult):
        if n % t == 0:
            best = t
    assert best is not None, (n, cap)
    return best


def _adaln_kernel(c_ref, w_ref, b_ref, o_ref):
    o_ref[...] = _hi_dot(_silu(c_ref[...]), w_ref[0]) + b_ref[...]


def _adaln(cvec, w, b, layer):
    _, d, n = w.shape
    tn = 1024
    return pl.pallas_call(
        _adaln_kernel,
        grid=(n // tn,),
        in_specs=[pl.BlockSpec((8, d), lambda j: (0, 0)),
                  pl.BlockSpec((1, d, tn), lambda j: (layer, 0, j)),
                  pl.BlockSpec((1, tn), lambda j: (0, j))],
        out_specs=pl.BlockSpec((8, tn), lambda j: (0, j)),
        out_shape=jax.ShapeDtypeStruct((8, n), F32),
        compiler_params=_cparams("arbitrary"),
        name="adaln",
    )(cvec, w, b)


def _norm_mod(x, g, shift, scale):
    ms = jnp.mean(x * x, axis=-1, keepdims=True)
    return (x * lax.rsqrt(ms + EPS) * g) * (1.0 + scale) + shift


def _nmm_kernel(x_ref, g_ref, mod_ref, w_ref, o_ref, xn_ref):
    @pl.when(pl.program_id(1) == 0)
    def _():
        xn_ref[...] = _norm_mod(x_ref[...], g_ref[...], mod_ref[0, 0:1, :], mod_ref[0, 1:2, :]).astype(BF16)

    o_ref[...] = _dot(xn_ref[...], w_ref[...])


def _group_of_tile(i, tm, n_lat, seq, bsz):
    return jnp.where(i * tm < n_lat, (i * tm) // seq, bsz)


def _norm_mod_matmul(x, g, mod, w_bf16, *, n_lat, seq, bsz, tn):
    t, d = x.shape
    n = w_bf16.shape[1]
    tm = 2 * ROW_TILE
    assert t % tm == 0 and n % tn == 0 and n_lat % tm == 0
    return pl.pallas_call(
        _nmm_kernel,
        grid=(t // tm, n // tn),
        in_specs=[pl.BlockSpec((tm, d), lambda i, j: (i, 0)),
                  pl.BlockSpec((1, d), lambda i, j: (0, 0)),
                  pl.BlockSpec((1, 6, d), lambda i, j: (_group_of_tile(i, tm, n_lat, seq, bsz), 0, 0)),
                  pl.BlockSpec((d, tn), lambda i, j: (0, j))],
        out_specs=pl.BlockSpec((tm, tn), lambda i, j: (i, j)),
        out_shape=jax.ShapeDtypeStruct((t, n), F32),
        scratch_shapes=[pltpu.VMEM((tm, d), BF16)],
        compiler_params=_cparams("arbitrary", "arbitrary"),
        name="norm_mod_inproj",
    )(x, g, mod, w_bf16)


def _head_norm(x, g):
    ms = jnp.mean(x * x, axis=-1, keepdims=True)
    return x * lax.rsqrt(ms + EPS) * g


def _prep_rope_kernel(p_ref, cos_ref, sin_ref, gq_ref, gk_ref, qt_ref, k_ref, vt_ref, *, n_q, n_kv, tq, q_scale):
    cos = cos_ref[...]
    sin = sin_ref[...]
    lane = lax.broadcasted_iota(I32, cos.shape, 1)
    first_half = (lane % 64) < 32

    def norm_rope(x, g):
        y = _head_norm(x, g)
        swapped = jnp.where(first_half, pltpu.roll(y, 96, 1), pltpu.roll(y, 32, 1))
        return y * cos + swapped * sin

    group = n_q // n_kv
    for h in range(n_q):
        q = norm_rope(p_ref[:, h * HEAD_DIM:(h + 1) * HEAD_DIM], gq_ref[...]) * q_scale
        kvh, hh = divmod(h, group)
        qt_ref[0, kvh, 0, :, hh * tq:(hh + 1) * tq] = q.T.astype(BF16)
    for h in range(n_kv):
        c0 = (n_q + h) * HEAD_DIM
        k_ref[0, h] = norm_rope(p_ref[:, c0:c0 + HEAD_DIM], gk_ref[...]).astype(BF16)
        c1 = (n_q + n_kv + h) * HEAD_DIM
        vt_ref[0, h, 0:HEAD_DIM, :] = p_ref[:, c1:c1 + HEAD_DIM].T.astype(BF16)
        vt_ref[0, h, HEAD_DIM:VT_ROWS, :] = jnp.ones((VT_ROWS - HEAD_DIM, tq), BF16)


def _pos_maps(tm, n_lat, seq, n_ctx):
    lat_tiles, seq_tiles, ctx_tiles = n_lat // tm, seq // tm, n_ctx // tm

    def batch_of(i):
        return jnp.where(i < lat_tiles, i // seq_tiles, (i - lat_tiles) // ctx_tiles)

    def pos_of(i):
        return jnp.where(i < lat_tiles, ctx_tiles + i % seq_tiles, (i - lat_tiles) % ctx_tiles)

    return batch_of, pos_of


def _prep_rope(p, cos, sin, gq, gk, *, q_scale, n_q, n_kv, n_lat, seq, n_ctx, bsz):
    t = p.shape[0]
    tm = Q_TILE
    pos_len = n_ctx + seq
    group = n_q // n_kv
    width = (n_q + 2 * n_kv) * HEAD_DIM
    batch_of, pos_of = _pos_maps(tm, n_lat, seq, n_ctx)
    kern = functools.partial(_prep_rope_kernel, n_q=n_q, n_kv=n_kv, tq=tm, q_scale=q_scale)
    return pl.pallas_call(
        kern,
        grid=(t // tm,),
        in_specs=[pl.BlockSpec((tm, width), lambda i: (i, 0)),
                  pl.BlockSpec((tm, HEAD_DIM), lambda i: (pos_of(i), 0)),
                  pl.BlockSpec((tm, HEAD_DIM), lambda i: (pos_of(i), 0)),
                  pl.BlockSpec((1, HEAD_DIM), lambda i: (0, 0)),
                  pl.BlockSpec((1, HEAD_DIM), lambda i: (0, 0))],
        out_specs=[pl.BlockSpec((1, n_kv, 1, HEAD_DIM, group * tm), lambda i: (batch_of(i), 0, pos_of(i), 0, 0)),
                   pl.BlockSpec((1, n_kv, tm, HEAD_DIM), lambda i: (batch_of(i), 0, pos_of(i), 0)),
                   pl.BlockSpec((1, n_kv, VT_ROWS, tm), lambda i: (batch_of(i), 0, 0, pos_of(i)))],
        out_shape=[jax.ShapeDtypeStruct((bsz, n_kv, pos_len // tm, HEAD_DIM, group * tm), BF16),
                   jax.ShapeDtypeStruct((bsz, n_kv, pos_len, HEAD_DIM), BF16),
                   jax.ShapeDtypeStruct((bsz, n_kv, VT_ROWS, pos_len), BF16)],
        compiler_params=_cparams("arbitrary"),
        name="prep_rope",
    )(p, cos, sin, gq, gk)


def _rope_tables(seq, n_ctx):
    t = jnp.arange(seq, dtype=I32)
    row = (t // GRID_W).astype(F32)
    col = (t % GRID_W).astype(F32)
    n_freq = HEAD_DIM // 4
    inv = ROPE_THETA ** (-jnp.arange(n_freq, dtype=F32) / n_freq)
    ar, ac = row[:, None] * inv, col[:, None] * inv
    cos = jnp.concatenate([jnp.cos(ar), jnp.cos(ar), jnp.cos(ac), jnp.cos(ac)], axis=1)
    sin = jnp.concatenate([-jnp.sin(ar), jnp.sin(ar), -jnp.sin(ac), jnp.sin(ac)], axis=1)
    cos = jnp.concatenate([jnp.ones((n_ctx, HEAD_DIM), F32), cos], axis=0)
    sin = jnp.concatenate([jnp.zeros((n_ctx, HEAD_DIM), F32), sin], axis=0)
    return cos, sin


def _flash_t_kernel(qt_ref, k_ref, vt_ref, o_ref, m_ref, l_ref, acc_ref, sa_ref, sb_ref, *, tk, n_k, tq, group):
    qt = qt_ref[0, 0, 0]
    m_ref[...] = jnp.full(m_ref.shape, -jnp.inf, F32)
    l_ref[...] = jnp.zeros(l_ref.shape, F32)
    acc_ref[...] = jnp.zeros(acc_ref.shape, F32)

    def scores(j, dst):
        off = pl.multiple_of(jnp.minimum(j, n_k - 1) * tk, tk)
        dst[...] = _dot(k_ref[0, 0, pl.ds(off, tk), :], qt)

    def consume(j, src):
        off = pl.multiple_of(j * tk, tk)
        s = src[...]
        m_old = m_ref[...]
        m_new = jnp.maximum(m_old, jnp.max(s, axis=0, keepdims=True))
        alpha = jnp.exp2(m_old - m_new)
        p = jnp.exp2(s - m_new).astype(BF16)
        r = _dot(vt_ref[0, 0, :, pl.ds(off, tk)], p)
        acc_ref[...] = alpha * acc_ref[...] + r[0:HEAD_DIM]
        l_ref[...] = alpha * l_ref[...] + r[HEAD_DIM:HEAD_DIM + 1]
        m_ref[...] = m_new

    scores(0, sa_ref)

    def body(jj, carry):
        j = 2 * jj
        scores(j + 1, sb_ref)
        consume(j, sa_ref)
        scores(j + 2, sa_ref)
        consume(j + 1, sb_ref)
        return carry

    lax.fori_loop(0, n_k // 2, body, 0)
    if n_k % 2:
        consume(n_k - 1, sa_ref)
    out = acc_ref[...] / l_ref[...]
    for h in range(group):
        o_ref[:, h * HEAD_DIM:(h + 1) * HEAD_DIM] = out[:, h * tq:(h + 1) * tq].T.astype(o_ref.dtype)


def _attention_a(qt, k, vt, *, n_lat, seq, n_ctx, bsz, need_ctx):
    n_kv, group, tq = k.shape[1], qt.shape[-1] // Q_TILE, Q_TILE
    pos_len = n_ctx + seq
    width = n_kv * group * HEAD_DIM
    tk = _largest_tile(pos_len, 1280)

    def scratch(tkk):
        n = group * tq
        return [pltpu.VMEM((1, n), F32), pltpu.VMEM((1, n), F32), pltpu.VMEM((HEAD_DIM, n), F32),
                pltpu.VMEM((tkk, n), F32), pltpu.VMEM((tkk, n), F32)]

    y = pl.pallas_call(
        functools.partial(_flash_t_kernel, tk=tk, n_k=pos_len // tk, tq=tq, group=group),
        grid=(bsz, n_kv, seq // tq),
        in_specs=[pl.BlockSpec((1, 1, 1, HEAD_DIM, group * tq), lambda b, h, i: (b, h, n_ctx // tq + i, 0, 0)),
                  pl.BlockSpec((1, 1, pos_len, HEAD_DIM), lambda b, h, i: (b, h, 0, 0)),
                  pl.BlockSpec((1, 1, VT_ROWS, pos_len), lambda b, h, i: (b, h, 0, 0))],
        out_specs=pl.BlockSpec((tq, group * HEAD_DIM), lambda b, h, i: (b * (seq // tq) + i, h)),
        out_shape=jax.ShapeDtypeStruct((n_lat, width), BF16),
        scratch_shapes=scratch(tk),
        compiler_params=_cparams("arbitrary", "arbitrary", "arbitrary"),
        name="attn_a_latent",
    )(qt, k, vt)
    if not need_ctx:
        return y
    tkc = _largest_tile(n_ctx, 768)
    y_ctx = pl.pallas_call(
        functools.partial(_flash_t_kernel, tk=tkc, n_k=n_ctx // tkc, tq=tq, group=group),
        grid=(bsz, n_kv, n_ctx // tq),
        in_specs=[pl.BlockSpec((1, 1, 1, HEAD_DIM, group * tq), lambda b, h, i: (b, h, i, 0, 0)),
                  pl.BlockSpec((1, 1, n_ctx, HEAD_DIM), lambda b, h, i: (b, h, 0, 0)),
                  pl.BlockSpec((1, 1, VT_ROWS, n_ctx), lambda b, h, i: (b, h, 0, 0))],
        out_specs=pl.BlockSpec((tq, group * HEAD_DIM), lambda b, h, i: (b * (n_ctx // tq) + i, h)),
        out_shape=jax.ShapeDtypeStruct((bsz * n_ctx, width), BF16),
        scratch_shapes=scratch(tkc),
        compiler_params=_cparams("arbitrary", "arbitrary", "arbitrary"),
        name="attn_a_context",
    )(qt, k, vt)
    return jnp.concatenate([y, y_ctx], axis=0)


def _segment_edges(i, tm, n_lat, seq, n_ctx):
    lat_tiles, seq_tiles, ctx_tiles = n_lat // tm, seq // tm, n_ctx // tm
    is_lat = i < lat_tiles
    pos = jnp.where(is_lat, i % seq_tiles, (i - lat_tiles) % ctx_tiles)
    last = jnp.where(is_lat, seq_tiles, ctx_tiles) - 1
    return pos == 0, pos == last


def _conv_kernel(cur_ref, prev_ref, next_ref, w_ref, b_ref, o_ref, ext_ref, *, tm, n_lat, seq, n_ctx):
    first, last = _segment_edges(pl.program_id(0), tm, n_lat, seq, n_ctx)
    ext_ref[0:8, :] = jnp.where(first, 0.0, prev_ref[...])
    ext_ref[8:8 + tm, :] = cur_ref[...]
    ext_ref[8 + tm:16 + tm, :] = jnp.where(last, 0.0, next_ref[...])
    acc = jnp.broadcast_to(b_ref[...], (tm, b_ref.shape[1]))
    half = SSM_CONV // 2
    for kk in range(SSM_CONV):
        acc = acc + w_ref[kk:kk + 1, :] * ext_ref[pl.ds(8 - half + kk, tm), :]
    o_ref[...] = _silu(acc)


def _ssm_conv(p, conv_w, conv_b, *, col0, n_lat, seq, n_ctx):
    t = p.shape[0]
    tm, tc = ROW_TILE, 512
    assert col0 % tc == 0 and SSM_CONV_DIM % tc == 0
    cb0 = col0 // tc
    r8 = tm // 8
    kern = functools.partial(_conv_kernel, tm=tm, n_lat=n_lat, seq=seq, n_ctx=n_ctx)
    return pl.pallas_call(
        kern,
        grid=(t // tm, SSM_CONV_DIM // tc),
        in_specs=[pl.BlockSpec((tm, tc), lambda i, j: (i, cb0 + j)),
                  pl.BlockSpec((8, tc), lambda i, j: (jnp.maximum(i * r8 - 1, 0), cb0 + j)),
                  pl.BlockSpec((8, tc), lambda i, j: (jnp.minimum((i + 1) * r8, t // 8 - 1), cb0 + j)),
                  pl.BlockSpec((SSM_CONV, tc), lambda i, j: (0, j)),
                  pl.BlockSpec((1, tc), lambda i, j: (0, j))],
        out_specs=pl.BlockSpec((tm, tc), lambda i, j: (i, j)),
        out_shape=jax.ShapeDtypeStruct((t, SSM_CONV_DIM), F32),
        scratch_shapes=[pltpu.VMEM((tm + 16, tc), F32)],
        compiler_params=_cparams("arbitrary", "arbitrary"),
        name="ssm_conv",
    )(p, p, p, conv_w, conv_b)


def _dt_kernel(p_ref, b_ref, o_ref):
    x = p_ref[...] + b_ref[...]
    sp = jnp.maximum(x, 0.0) + jnp.log1p(jnp.exp(-jnp.abs(x)))
    o_ref[0] = sp
    o_ref[1] = pltpu.roll(sp, HEAD_DIM - SSM_HEADS, 1)


def _ssm_dt(p, bias_row, *, col0):
    t = p.shape[0]
    tm = 2 * ROW_TILE
    assert col0 % 128 == 0
    return pl.pallas_call(
        _dt_kernel,
        grid=(t // tm,),
        in_specs=[pl.BlockSpec((tm, 128), lambda i: (i, col0 // 128)),
                  pl.BlockSpec((1, 128), lambda i: (0, 0))],
        out_specs=pl.BlockSpec((2, tm, 128), lambda i: (0, i, 0)),
        out_shape=jax.ShapeDtypeStruct((2, t, 128), F32),
        compiler_params=_cparams("arbitrary"),
        name="ssm_dt",
    )(p, bias_row)


def _ssd_kernel(x_ref, bc_ref, dt_ref, a_ref, e_ref, y_ref, h_ref):
    d = pl.program_id(1)
    lc = SSM_CHUNK
    gw = SSM_INNER // SSM_GROUPS
    hpg = SSM_HEADS // SSM_GROUPS

    @pl.when(pl.program_id(2) == 0)
    def _():
        h_ref[...] = jnp.zeros(h_ref.shape, F32)

    dt = dt_ref[0]
    w = dt * a_ref[0]
    row = lax.broadcasted_iota(I32, (lc, lc), 0)
    col = lax.broadcasted_iota(I32, (lc, lc), 1)
    fwd = d == 0
    sgn = jnp.where(fwd, 1, -1)
    allowed = (row - col) * sgn >= 0
    allowed_t = (col - row) * sgn >= 0
    cs = _hi_dot(allowed.astype(F32), w)
    cs_t = _hi_dot(w.T, allowed_t.astype(F32))
    dt_t = dt.T
    tot = jnp.where(fwd, cs[lc - 1:lc, :], cs[0:1, :])
    e = e_ref[...]

    def expand(a):
        hi = a.astype(BF16)
        lo = (a - hi.astype(F32)).astype(BF16)
        return _dot(hi, e) + _dot(lo, e)

    x = x_ref[...]
    xw = (x * expand(jnp.exp(tot - cs) * dt)).astype(BF16)
    off_scale = expand(jnp.exp(cs))
    state_decay = expand(jnp.exp(jnp.broadcast_to(tot, (lc, 128))))
    xb = x.astype(BF16)
    for g in range(SSM_GROUPS):
        bm = bc_ref[:, g * SSM_STATE:(g + 1) * SSM_STATE]
        cm = bc_ref[:, (SSM_GROUPS + g) * SSM_STATE:(SSM_GROUPS + g + 1) * SSM_STATE].astype(BF16)
        cb = _dot_nt(cm, bm.astype(BF16))
        h_prev = h_ref[g]
        y_off = _dot(cm, h_prev.astype(BF16))
        s_chunk = _dot(bm.T.astype(BF16), xw[:, g * gw:(g + 1) * gw])
        h_ref[g] = state_decay[:, g * gw:(g + 1) * gw] * h_prev + s_chunk
        pieces = []
        for r in range(hpg):
            hh = g * hpg + r
            seg = cs[:, hh:hh + 1] - cs_t[hh:hh + 1, :]
            dec = jnp.exp(jnp.where(allowed, seg, -jnp.inf))
            mix = (cb * dec * dt_t[hh:hh + 1, :]).astype(BF16)
            pieces.append(_dot(mix, xb[:, hh * SSM_HEAD_DIM:(hh + 1) * SSM_HEAD_DIM]))
        y_ref[0, :, g * gw:(g + 1) * gw] = (jnp.concatenate(pieces, axis=1)
                                           + y_off * off_scale[:, g * gw:(g + 1) * gw])


def _ssd_scan(xbc, dt2, a_rows, expand, *, n_lat, seq, n_ctx, bsz):
    t = xbc.shape[0]
    lc = SSM_CHUNK
    ctx_chunks, lat_chunks = n_ctx // lc, seq // lc
    gw = SSM_INNER // SSM_GROUPS

    def rowblk(b, d, s):
        in_ctx = s < ctx_chunks
        j_ctx = jnp.where(d == 0, s, ctx_chunks - 1 - s)
        sl = s - ctx_chunks
        j_lat = jnp.where(d == 0, sl, lat_chunks - 1 - sl)
        return jnp.where(in_ctx, (n_lat + b * n_ctx) // lc + j_ctx, b * lat_chunks + j_lat)

    return pl.pallas_call(
        _ssd_kernel,
        grid=(bsz, 2, ctx_chunks + lat_chunks),
        in_specs=[pl.BlockSpec((lc, SSM_INNER), lambda b, d, s: (rowblk(b, d, s), 0)),
                  pl.BlockSpec((lc, 2 * SSM_GROUPS * SSM_STATE),
                               lambda b, d, s: (rowblk(b, d, s), SSM_INNER // (2 * SSM_GROUPS * SSM_STATE))),
                  pl.BlockSpec((1, lc, 128), lambda b, d, s: (d, rowblk(b, d, s), 0)),
                  pl.BlockSpec((1, 1, 128), lambda b, d, s: (d, 0, 0)),
                  pl.BlockSpec((128, SSM_INNER), lambda b, d, s: (0, 0))],
        out_specs=pl.BlockSpec((1, lc, SSM_INNER), lambda b, d, s: (d, rowblk(b, d, s), 0)),
        out_shape=jax.ShapeDtypeStruct((2, t, SSM_INNER), F32),
        scratch_shapes=[pltpu.VMEM((SSM_GROUPS, SSM_STATE, gw), F32)],
        compiler_params=_cparams("arbitrary", "arbitrary", "arbitrary"),
        name="ssd_scan",
    )(xbc, xbc, dt2, a_rows, expand)


def _ssd_finish_kernel(yf_ref, yb_ref, xs_ref, z0_ref, z1_ref, dsk_ref, g_ref, o_ref):
    gw = SSM_INNER // SSM_GROUPS
    zs = (z0_ref, z1_ref)
    for g in range(SSM_GROUPS):
        sl = slice(g * gw, (g + 1) * gw)
        y = yf_ref[0, :, sl] + yb_ref[0, :, sl] + dsk_ref[:, sl] * xs_ref[:, sl]
        y = y * _silu(zs[g][...])
        o_ref[:, sl] = _head_norm(y, g_ref[:, sl]).astype(o_ref.dtype)


def _ssd_finish(ydir, xbc, p, dsk_row, g_row, *, z_col0):
    t = xbc.shape[0]
    tm = ROW_TILE
    gw = SSM_INNER // SSM_GROUPS
    assert z_col0 % gw == 0 and SSM_GROUPS == 2
    zb = z_col0 // gw
    return pl.pallas_call(
        _ssd_finish_kernel,
        grid=(t // tm,),
        in_specs=[pl.BlockSpec((1, tm, SSM_INNER), lambda i: (0, i, 0)),
                  pl.BlockSpec((1, tm, SSM_INNER), lambda i: (1, i, 0)),
                  pl.BlockSpec((tm, SSM_INNER), lambda i: (i, 0)),
                  pl.BlockSpec((tm, gw), lambda i: (i, zb)),
                  pl.BlockSpec((tm, gw), lambda i: (i, zb + 1)),
                  pl.BlockSpec((1, SSM_INNER), lambda i: (0, 0)),
                  pl.BlockSpec((1, SSM_INNER), lambda i: (0, 0))],
        out_specs=pl.BlockSpec((tm, SSM_INNER), lambda i: (i, 0)),
        out_shape=jax.ShapeDtypeStruct((t, SSM_INNER), BF16),
        compiler_params=_cparams("arbitrary"),
        name="ssd_finish",
    )(ydir, ydir, xbc, p, p, dsk_row, g_row)


def _window_kernel(qt_ref, k_ref, vt_ref, sink_ref, o_ref, *, tq, group, n_ctx, seq):
    band = tq + 2 * WINDOW
    pos_len = n_ctx + seq
    q0 = pl.program_id(2) * tq
    start = pl.multiple_of(jnp.minimum(n_ctx + q0 - WINDOW, pos_len - band), 128)
    qt = qt_ref[0, 0, 0]
    n = group * tq
    s_cx = _dot(k_ref[0, 0, 0:n_ctx, :], qt)
    s_w = _dot(k_ref[0, 0, pl.ds(start, band), :], qt)
    kpos = start - n_ctx + lax.broadcasted_iota(I32, (band, n), 0)
    qpos = q0 + lax.broadcasted_iota(I32, (band, n), 1) % tq
    valid = (jnp.abs(qpos - kpos) <= WINDOW) & (kpos >= 0)
    s_w = jnp.where(valid, s_w, -jnp.inf)
    sink = sink_ref[0]
    m = jnp.maximum(jnp.maximum(jnp.max(s_cx, axis=0, keepdims=True), jnp.max(s_w, axis=0, keepdims=True)), sink)
    p_cx = jnp.exp(s_cx - m)
    p_w = jnp.exp(s_w - m)
    l = jnp.sum(p_cx, axis=0, keepdims=True) + jnp.sum(p_w, axis=0, keepdims=True) + jnp.exp(sink - m)
    acc = (_dot(vt_ref[0, 0, 0:HEAD_DIM, 0:n_ctx], p_cx.astype(BF16))
           + _dot(vt_ref[0, 0, 0:HEAD_DIM, pl.ds(start, band)], p_w.astype(BF16)))
    out = acc / l
    for h in range(group):
        o_ref[:, h * HEAD_DIM:(h + 1) * HEAD_DIM] = out[:, h * tq:(h + 1) * tq].T.astype(o_ref.dtype)


def _attention_c(qt, k, vt, sink_rows, *, n_lat, seq, n_ctx, bsz):
    n_kv, group, tq = k.shape[1], qt.shape[-1] // Q_TILE, Q_TILE
    pos_len = n_ctx + seq
    assert n_ctx >= WINDOW and n_ctx % 128 == 0 and pos_len >= tq + 2 * WINDOW
    return pl.pallas_call(
        functools.partial(_window_kernel, tq=tq, group=group, n_ctx=n_ctx, seq=seq),
        grid=(bsz, n_kv, seq // tq),
        in_specs=[pl.BlockSpec((1, 1, 1, HEAD_DIM, group * tq), lambda b, h, i: (b, h, n_ctx // tq + i, 0, 0)),
                  pl.BlockSpec((1, 1, pos_len, HEAD_DIM), lambda b, h, i: (b, h, 0, 0)),
                  pl.BlockSpec((1, 1, VT_ROWS, pos_len), lambda b, h, i: (b, h, 0, 0)),
                  pl.BlockSpec((1, 1, group * tq), lambda b, h, i: (h, 0, 0))],
        out_specs=pl.BlockSpec((tq, group * HEAD_DIM), lambda b, h, i: (b * (seq // tq) + i, h)),
        out_shape=jax.ShapeDtypeStruct((n_lat, n_kv * group * HEAD_DIM), BF16),
        compiler_params=_cparams("arbitrary", "arbitrary", "arbitrary"),
        name="attn_c_window",
    )(qt, k, vt, sink_rows)


def _prep_na(p, gq, gk, *, col0, n_lat, seq, n_ctx, bsz):
    t = p.shape[0]
    tm = ROW_TILE
    hw = NA_HEADS * HEAD_DIM
    assert col0 % hw == 0 or (2 * col0) % hw == 0
    pos_len = n_ctx + seq
    batch_of, pos_of = _pos_maps(tm, n_lat, seq, n_ctx)
    lat_tiles, seq_tiles, ctx_tiles = n_lat // tm, seq // tm, n_ctx // tm

    def qpos_of(i):
        return jnp.where(i < lat_tiles, i % seq_tiles, seq_tiles + (i - lat_tiles) % ctx_tiles)

    cw = hw // 2
    cb = col0 // cw
    shp = jax.ShapeDtypeStruct((bsz, NA_HEADS, pos_len, HEAD_DIM), BF16)
    kv_spec = pl.BlockSpec((1, NA_HEADS, tm, HEAD_DIM), lambda i: (batch_of(i), 0, pos_of(i), 0))
    q_spec = pl.BlockSpec((1, NA_HEADS, tm, HEAD_DIM), lambda i: (batch_of(i), 0, qpos_of(i), 0))

    def kern(q0, q1, k0, k1, v0, v1, gq_ref, gk_ref, q_ref, k_ref, v_ref):
        halves = NA_HEADS // 2
        for h in range(NA_HEADS):
            sl = slice((h % halves) * HEAD_DIM, (h % halves + 1) * HEAD_DIM)
            pq, pk, pv = ((q0, k0, v0) if h < halves else (q1, k1, v1))
            q_ref[0, h] = (_head_norm(pq[:, sl], gq_ref[...]) * (HEAD_DIM ** -0.5)).astype(BF16)
            k_ref[0, h] = _head_norm(pk[:, sl], gk_ref[...]).astype(BF16)
            v_ref[0, h] = pv[:, sl].astype(BF16)

    def col_spec(j):
        return pl.BlockSpec((tm, cw), lambda i: (i, cb + j))

    return pl.pallas_call(
        kern,
        grid=(t // tm,),
        in_specs=[col_spec(j) for j in range(6)] + [pl.BlockSpec((1, HEAD_DIM), lambda i: (0, 0)),
                                                    pl.BlockSpec((1, HEAD_DIM), lambda i: (0, 0))],
        out_specs=[q_spec, kv_spec, kv_spec],
        out_shape=[shp, shp, shp],
        compiler_params=_cparams("arbitrary"),
        name="prep_na",
    )(p, p, p, p, p, p, gq, gk)


def _na_kernel(q_ref, k_ref, v_ref, bias_ref, o_ref, *, n_ctx, rows):
    tq = NA_TILE_ROWS * GRID_W
    win = NA_WIN_ROWS * GRID_W
    ti = pl.program_id(2)
    w0 = jnp.clip(ti * NA_TILE_ROWS - NA_ROWS // 2, 0, rows - NA_WIN_ROWS)
    start = pl.multiple_of(n_ctx + w0 * GRID_W, GRID_W)
    q = q_ref[0, 0]
    s_cx = _dot_nt(q, k_ref[0, 0, 0:n_ctx, :])
    s_nb = _dot_nt(q, k_ref[0, 0, pl.ds(start, win), :]) + bias_ref[0, 0]
    m = jnp.maximum(jnp.max(s_cx, axis=1, keepdims=True), jnp.max(s_nb, axis=1, keepdims=True))
    p_cx = jnp.exp(s_cx - m)
    p_nb = jnp.exp(s_nb - m)
    l = jnp.sum(p_cx, axis=1, keepdims=True) + jnp.sum(p_nb, axis=1, keepdims=True)
    acc = _dot(p_cx.astype(BF16), v_ref[0, 0, 0:n_ctx, :]) + _dot(p_nb.astype(BF16), v_ref[0, 0, pl.ds(start, win), :])
    o_ref[...] = (acc / l).astype(o_ref.dtype)


def _na_bias_tables(rpb, rows):
    n_tiles = rows // NA_TILE_ROWS
    tables = []
    for ti in (0, 1, n_tiles - 1):
        w0 = min(max(ti * NA_TILE_ROWS - NA_ROWS // 2, 0), rows - NA_WIN_ROWS)
        qr = ti * NA_TILE_ROWS + jnp.arange(NA_TILE_ROWS)
        r0 = jnp.clip(qr - NA_ROWS // 2, 0, rows - NA_ROWS)
        kr = w0 + jnp.arange(NA_WIN_ROWS)
        row_ok = (kr[None, :] >= r0[:, None]) & (kr[None, :] < r0[:, None] + NA_ROWS)
        dr = jnp.clip(kr[None, :] - qr[:, None] + NA_ROWS - 1, 0, 2 * NA_ROWS - 2)
        qc = jnp.arange(GRID_W)
        c0 = jnp.clip(qc - NA_COLS // 2, 0, GRID_W - NA_COLS)
        kc = jnp.arange(GRID_W)
        col_ok = (kc[None, :] >= c0[:, None]) & (kc[None, :] < c0[:, None] + NA_COLS)
        dc = jnp.clip(kc[None, :] - qc[:, None] + NA_COLS - 1, 0, 2 * NA_COLS - 2)
        sel_r = ((dr[:, :, None] == jnp.arange(2 * NA_ROWS - 1)) & row_ok[:, :, None]).astype(F32)
        sel_c = ((dc[:, :, None] == jnp.arange(2 * NA_COLS - 1)) & col_ok[:, :, None]).astype(F32)
        b = jnp.einsum("rki,hij,cqj->hrckq", sel_r, rpb.astype(F32), sel_c, precision=HI)
        ok = row_ok[:, None, :, None] & col_ok[None, :, None, :]
        b = jnp.where(ok[None], b, -jnp.inf)
        tables.append(b.reshape(rpb.shape[0], NA_TILE_ROWS * GRID_W, NA_WIN_ROWS * GRID_W))
    return jnp.stack(tables, axis=0).astype(F32)


def _attention_d(q, k, v, bias, *, n_lat, seq, n_ctx, bsz):
    rows = seq // GRID_W
    tq = NA_TILE_ROWS * GRID_W
    win = NA_WIN_ROWS * GRID_W
    n_tiles = rows // NA_TILE_ROWS
    pos_len = n_ctx + seq
    assert rows % NA_TILE_ROWS == 0 and rows >= NA_WIN_ROWS and n_tiles >= 3

    def kind(i):
        return jnp.where(i == 0, 0, jnp.where(i == n_tiles - 1, 2, 1))

    return pl.pallas_call(
        functools.partial(_na_kernel, n_ctx=n_ctx, rows=rows),
        grid=(bsz, NA_HEADS, n_tiles),
        in_specs=[pl.BlockSpec((1, 1, tq, HEAD_DIM), lambda b, h, i: (b, h, i, 0)),
                  pl.BlockSpec((1, 1, pos_len, HEAD_DIM), lambda b, h, i: (b, h, 0, 0)),
                  pl.BlockSpec((1, 1, pos_len, HEAD_DIM), lambda b, h, i: (b, h, 0, 0)),
                  pl.BlockSpec((1, 1, tq, win), lambda b, h, i: (kind(i), h, 0, 0))],
        out_specs=pl.BlockSpec((tq, HEAD_DIM), lambda b, h, i: (b * n_tiles + i, h)),
        out_shape=jax.ShapeDtypeStruct((n_lat, NA_HEADS * HEAD_DIM), BF16),
        compiler_params=_cparams("arbitrary", "arbitrary", "arbitrary"),
        name="attn_d_neighbourhood",
    )(q, k, v, bias)


def _pack_bf16_pairs(f):
    half = f.shape[1] // 2
    bits = lax.bitcast_convert_type(f.astype(BF16).astype(F32), U32)
    lo = lax.shift_right_logical(bits[:, :half], jnp.uint32(16))
    hi = bits[:, half:] & jnp.uint32(0xFFFF0000)
    return hi | lo


def _unpack_bf16_pairs(bits):
    lo = lax.bitcast_convert_type(lax.shift_left(bits, jnp.uint32(16)), F32).astype(BF16)
    hi = lax.bitcast_convert_type(bits & jnp.uint32(0xFFFF0000), F32).astype(BF16)
    return lo, hi


def _store_row_tiles(ref, val):
    rows, width = val.shape
    per = width // 128
    for s in range(per):
        ref[pl.ds(s, rows, stride=per), :] = val[:, s * 128:(s + 1) * 128]


def _load_row_tiles(ref, rows, per):
    return jnp.concatenate([ref[pl.ds(s, rows, stride=per), :] for s in range(per)], axis=1)


def _outproj_kernel(x_ref, y1_ref, y2_ref, w_ref, mod_ref, g_ref, wrh_ref, wrl_ref, xo_ref, fpk_ref, lg_ref):
    half = w_ref.shape[0] // 2
    delta = _dot(y1_ref[...], w_ref[0:half, :]) + _dot(y2_ref[...], w_ref[half:, :])
    xn = x_ref[...] + mod_ref[0, 2:3, :] * delta
    xo_ref[...] = xn
    f = _norm_mod(xn, g_ref[...], mod_ref[0, 3:4, :], mod_ref[0, 4:5, :])
    f_hi = f.astype(BF16)
    f_lo = (f - f_hi.astype(F32)).astype(BF16)
    lg_ref[...] = _dot_nt(wrh_ref[...], f_hi) + (_dot_nt(wrl_ref[...], f_hi) + _dot_nt(wrh_ref[...], f_lo))
    _store_row_tiles(fpk_ref, _pack_bf16_pairs(f))


def _outproj(x, y1, y2, w_bf16, mod, g, wr_t, *, n_rows, n_lat, seq, bsz):
    d = x.shape[1]
    tm = ROW_TILE
    hw = y1.shape[1]
    wr_hi = wr_t.astype(BF16)
    wr_lo = (wr_t - wr_hi.astype(F32)).astype(BF16)
    return pl.pallas_call(
        _outproj_kernel,
        grid=(n_rows // tm,),
        in_specs=[pl.BlockSpec((tm, d), lambda i: (i, 0)),
                  pl.BlockSpec((tm, hw), lambda i: (i, 0)),
                  pl.BlockSpec((tm, hw), lambda i: (i, 0)),
                  pl.BlockSpec((2 * hw, d), lambda i: (0, 0)),
                  pl.BlockSpec((1, 6, d), lambda i: (_group_of_tile(i, tm, n_lat, seq, bsz), 0, 0)),
                  pl.BlockSpec((1, d), lambda i: (0, 0)),
                  pl.BlockSpec((N_EXPERTS, d), lambda i: (0, 0)),
                  pl.BlockSpec((N_EXPERTS, d), lambda i: (0, 0))],
        out_specs=[pl.BlockSpec((tm, d), lambda i: (i, 0)),
                   pl.BlockSpec((tm * (d // 256), 128), lambda i: (i, 0)),
                   pl.BlockSpec((N_EXPERTS, tm), lambda i: (0, i))],
        out_shape=[jax.ShapeDtypeStruct((n_rows, d), F32),
                   jax.ShapeDtypeStruct((n_rows * (d // 256), 128), U32),
                   jax.ShapeDtypeStruct((N_EXPERTS, n_rows), F32)],
        compiler_params=_cparams("arbitrary"),
        name="outproj_residual_moe_in",
    )(x, y1, y2, w_bf16, mod, g, wr_hi, wr_lo)


def _route_kernel(lg_ref, b_ref, e_ref, r_ref, w_ref, cnt_ref, run_ref):
    tr = lg_ref.shape[1]
    per = N_EXPERTS // N_GROUPS
    neg = -jnp.inf

    @pl.when(pl.program_id(0) == 0)
    def _():
        run_ref[...] = jnp.zeros(run_ref.shape, F32)

    scores = jax.nn.sigmoid(lg_ref[...])
    sel = scores + b_ref[...]
    sel3 = sel.reshape(N_GROUPS, per, tr)
    mem = lax.broadcasted_iota(I32, (N_GROUPS, per, tr), 1)
    m1 = jnp.max(sel3, axis=1, keepdims=True)
    i1 = jnp.min(jnp.where(sel3 == m1, mem, per), axis=1, keepdims=True)
    m2 = jnp.max(jnp.where(mem == i1, neg, sel3), axis=1, keepdims=True)
    gs = (m1 + m2).reshape(N_GROUPS, tr)
    gid = lax.broadcasted_iota(I32, (N_GROUPS, tr), 0)
    keep = jnp.zeros((N_GROUPS, tr), F32)
    for _ in range(TOPK_GROUPS):
        gm = jnp.max(gs, axis=0, keepdims=True)
        gi = jnp.min(jnp.where(gs == gm, gid, N_GROUPS), axis=0, keepdims=True)
        hit = gid == gi
        keep = jnp.where(hit, 1.0, keep)
        gs = jnp.where(hit, neg, gs)
    keep3 = jnp.broadcast_to(keep.reshape(N_GROUPS, 1, tr), (N_GROUPS, per, tr))
    cand = jnp.where(keep3 > 0.5, sel3, neg).reshape(N_EXPERTS, tr)
    eid = lax.broadcasted_iota(I32, (N_EXPERTS, tr), 0)
    idxs, ws = [], []
    sel_f = jnp.zeros((N_EXPERTS, tr), F32)
    for _ in range(TOP_K):
        cm = jnp.max(cand, axis=0, keepdims=True)
        ci = jnp.min(jnp.where(cand == cm, eid, N_EXPERTS), axis=0, keepdims=True)
        hit = eid == ci
        idxs.append(ci)
        ws.append(jnp.sum(jnp.where(hit, scores, 0.0), axis=0, keepdims=True))
        sel_f = jnp.where(hit, 1.0, sel_f)
        cand = jnp.where(hit, neg, cand)
    before = (lax.broadcasted_iota(I32, (tr, tr), 0) < lax.broadcasted_iota(I32, (tr, tr), 1))
    rank = _dot(sel_f.astype(BF16), jnp.where(before, 1.0, 0.0).astype(BF16)) + run_ref[...]
    run_ref[...] = run_ref[...] + jnp.sum(sel_f, axis=1, keepdims=True)
    cnt_ref[...] = run_ref[...]
    wsum = ws[0]
    for kk in range(1, TOP_K):
        wsum = wsum + ws[kk]
    for kk in range(TOP_K):
        e_ref[kk:kk + 1, :] = idxs[kk]
        r_ref[kk:kk + 1, :] = jnp.sum(jnp.where(eid == idxs[kk], rank, 0.0), axis=0, keepdims=True).astype(I32)
        w_ref[kk:kk + 1, :] = ws[kk] / wsum * ROUTE_SCALE


def _route(logits_t, b_col):
    n_e, t = logits_t.shape
    tr = ROW_TILE
    return pl.pallas_call(
        _route_kernel,
        grid=(t // tr,),
        in_specs=[pl.BlockSpec((n_e, tr), lambda i: (0, i)),
                  pl.BlockSpec((n_e, 1), lambda i: (0, 0))],
        out_specs=[pl.BlockSpec((TOP_K, tr), lambda i: (0, i)),
                   pl.BlockSpec((TOP_K, tr), lambda i: (0, i)),
                   pl.BlockSpec((TOP_K, tr), lambda i: (0, i)),
                   pl.BlockSpec((n_e, 1), lambda i: (0, 0))],
        out_shape=[jax.ShapeDtypeStruct((TOP_K, t), I32),
                   jax.ShapeDtypeStruct((TOP_K, t), I32),
                   jax.ShapeDtypeStruct((TOP_K, t), F32),
                   jax.ShapeDtypeStruct((n_e, 1), F32)],
        scratch_shapes=[pltpu.VMEM((n_e, 1), F32)],
        compiler_params=_cparams("arbitrary"),
        name="moe_route",
    )(logits_t, b_col)


PK_TILES = 8
Y_TILES = 16


def _scatter_kernel(dest_ref, f_ref, xs_ref, sem, *, tm):
    def copy(t, kk):
        src = pl.multiple_of(t * PK_TILES, PK_TILES)
        dst = pl.multiple_of(dest_ref[0, 0, t * TOP_K + kk], PK_TILES)
        return pltpu.make_async_copy(f_ref.at[pl.ds(src, PK_TILES)], xs_ref.at[pl.ds(dst, PK_TILES)], sem)

    def issue(t, c):
        for kk in range(TOP_K):
            copy(t, kk).start()
        return c

    def drain(t, c):
        for kk in range(TOP_K):
            copy(t, kk).wait()
        return c

    lax.fori_loop(0, tm, issue, 0)
    lax.fori_loop(0, tm, drain, 0)


def _scatter_rows(dest3, fpk, n_slots):
    tm = ROW_TILE
    t = fpk.shape[0] // PK_TILES
    return pl.pallas_call(
        functools.partial(_scatter_kernel, tm=tm),
        grid=(t // tm,),
        in_specs=[pl.BlockSpec((1, 1, TOP_K * tm), lambda i: (i, 0, 0), memory_space=pltpu.SMEM),
                  pl.BlockSpec((tm * PK_TILES, 128), lambda i: (i, 0))],
        out_specs=pl.BlockSpec(memory_space=pl.ANY),
        out_shape=jax.ShapeDtypeStruct((n_slots * PK_TILES, 128), U32),
        scratch_shapes=[pltpu.SemaphoreType.DMA(())],
        compiler_params=_cparams("arbitrary"),
        name="moe_scatter_rows",
    )(dest3, fpk)


def _ffn(bits, w1, w3, w2):
    half = bits.shape[1]
    lo, hi = _unpack_bf16_pairs(bits)
    h1 = _dot(lo, w1[0:half, :]) + _dot(hi, w1[half:, :])
    h3 = _dot(lo, w3[0:half, :]) + _dot(hi, w3[half:, :])
    return _dot((_silu(h1) * h3).astype(BF16), w2[...])


def _expert_kernel(be_ref, bv_ref, nu_ref, xs_ref, w1_ref, w3_ref, w2_ref, y_ref, w1b, w3b, w2b):
    i = pl.program_id(0)

    @pl.when(i < nu_ref[0])
    def _():
        e = be_ref[i]
        prev = be_ref[jnp.maximum(i - 1, 0)]

        @pl.when((i == 0) | (e != prev))
        def _():
            w1b[...] = w1_ref[0, 0].astype(BF16)
            w3b[...] = w3_ref[0, 0].astype(BF16)
            w2b[...] = w2_ref[0, 0].astype(BF16)

        tm = MOE_BLOCK
        bits = _load_row_tiles(xs_ref, tm, PK_TILES)
        rows = lax.broadcasted_iota(I32, bits.shape, 0)
        bits = jnp.where(rows < bv_ref[i], bits, jnp.uint32(0))
        _store_row_tiles(y_ref, _ffn(bits, w1b, w3b, w2b))


def _experts(blk_expert, blk_valid, n_used, xs, w1, w3, w2, layer):
    n_slots = xs.shape[0] // PK_TILES
    d, de = w1.shape[2], w1.shape[3]
    assert d == Y_TILES * 128 == 2 * PK_TILES * 128
    tm = MOE_BLOCK
    grid_spec = pltpu.PrefetchScalarGridSpec(
        num_scalar_prefetch=3,
        grid=(n_slots // tm,),
        in_specs=[pl.BlockSpec((tm * PK_TILES, 128), lambda i, be, bv, nu: (i, 0)),
                  pl.BlockSpec((1, 1, d, de), lambda i, be, bv, nu: (layer, be[i], 0, 0)),
                  pl.BlockSpec((1, 1, d, de), lambda i, be, bv, nu: (layer, be[i], 0, 0)),
                  pl.BlockSpec((1, 1, de, d), lambda i, be, bv, nu: (layer, be[i], 0, 0))],
        out_specs=pl.BlockSpec((tm * Y_TILES, 128), lambda i, be, bv, nu: (i, 0)),
        scratch_shapes=[pltpu.VMEM((d, de), BF16), pltpu.VMEM((d, de), BF16), pltpu.VMEM((de, d), BF16)],
    )
    return pl.pallas_call(
        _expert_kernel,
        grid_spec=grid_spec,
        out_shape=jax.ShapeDtypeStruct((n_slots * Y_TILES, 128), F32),
        compiler_params=_cparams("arbitrary"),
        name="moe_experts",
    )(blk_expert, blk_valid, n_used, xs, w1, w3, w2)


def _shared_kernel(f_ref, w1_ref, w3_ref, w2_ref, y_ref):
    y_ref[...] = _ffn(_load_row_tiles(f_ref, y_ref.shape[0], PK_TILES), w1_ref, w3_ref, w2_ref)


def _shared_expert(fpk, w1, w3, w2):
    t = fpk.shape[0] // PK_TILES
    d, de = w1.shape
    tm = 2 * ROW_TILE
    return pl.pallas_call(
        _shared_kernel,
        grid=(t // tm,),
        in_specs=[pl.BlockSpec((tm * PK_TILES, 128), lambda i: (i, 0)),
                  pl.BlockSpec((d, de), lambda i: (0, 0)),
                  pl.BlockSpec((d, de), lambda i: (0, 0)),
                  pl.BlockSpec((de, d), lambda i: (0, 0))],
        out_specs=pl.BlockSpec((tm, d), lambda i: (i, 0)),
        out_shape=jax.ShapeDtypeStruct((t, d), F32),
        compiler_params=_cparams("arbitrary"),
        name="moe_shared_expert",
    )(fpk, w1, w3, w2)


def _combine_kernel(dest_ref, w_ref, x_ref, ysh_ref, mod_ref, y_ref, o_ref, ybuf, acc_ref, sem):
    tm = x_ref.shape[0]

    def copy(t, kk):
        src = pl.multiple_of(dest_ref[0, 0, t * TOP_K + kk], Y_TILES)
        dst = pl.multiple_of(t * Y_TILES, Y_TILES)
        return pltpu.make_async_copy(y_ref.at[pl.ds(src, Y_TILES)], ybuf.at[kk, pl.ds(dst, Y_TILES)], sem)

    def issue(t, c):
        for kk in range(TOP_K):
            copy(t, kk).start()
        return c

    def drain(t, c):
        for kk in range(TOP_K):
            copy(t, kk).wait()
        return c

    lax.fori_loop(0, tm, issue, 0)
    lax.fori_loop(0, tm, drain, 0)

    def token(t, c):
        r = pl.multiple_of(t * Y_TILES, Y_TILES)
        a = w_ref[0, 0, t * TOP_K] * ybuf[0, pl.ds(r, Y_TILES), :]
        for kk in range(1, TOP_K):
            a = a + w_ref[0, 0, t * TOP_K + kk] * ybuf[kk, pl.ds(r, Y_TILES), :]
        acc_ref[pl.ds(r, Y_TILES), :] = a
        return c

    lax.fori_loop(0, tm, token, 0, unroll=4)
    for c in range(Y_TILES):
        sl = slice(c * 128, (c + 1) * 128)
        routed = acc_ref[pl.ds(c, tm, stride=Y_TILES), :]
        o_ref[:, sl] = x_ref[:, sl] + mod_ref[0, 5:6, sl] * (routed + ysh_ref[:, sl])


def _combine(dest3, x, ysh, w3, mod, y, *, tm, n_lat, seq, bsz):
    t, d = x.shape
    return pl.pallas_call(
        _combine_kernel,
        grid=(t // tm,),
        in_specs=[pl.BlockSpec((1, 1, TOP_K * tm), lambda i: (i, 0, 0), memory_space=pltpu.SMEM),
                  pl.BlockSpec((1, 1, TOP_K * tm), lambda i: (i, 0, 0), memory_space=pltpu.SMEM),
                  pl.BlockSpec((tm, d), lambda i: (i, 0)),
                  pl.BlockSpec((tm, d), lambda i: (i, 0)),
                  pl.BlockSpec((1, 6, d), lambda i: (_group_of_tile(i, tm, n_lat, seq, bsz), 0, 0)),
                  pl.BlockSpec(memory_space=pl.ANY)],
        out_specs=pl.BlockSpec((tm, d), lambda i: (i, 0)),
        out_shape=jax.ShapeDtypeStruct((t, d), F32),
        scratch_shapes=[pltpu.VMEM((TOP_K, tm * Y_TILES, 128), F32), pltpu.VMEM((tm * Y_TILES, 128), F32),
                        pltpu.SemaphoreType.DMA(())],
        compiler_params=_cparams("arbitrary"),
        name="moe_combine",
    )(dest3, w3, x, ysh, mod, y)


def _moe(x_new, fpk, logits_t, mod, b_router, w_e1, w_e3, w_e2, w_s1, w_s3, w_s2, *, layer, n_lat, seq, bsz):
    t = x_new.shape[0]
    top_e, top_r, top_w, counts = _route(logits_t, b_router.reshape(N_EXPERTS, 1))
    counts = counts[:, 0].astype(I32)
    blocks = (counts + MOE_BLOCK - 1) // MOE_BLOCK
    blk_end = jnp.cumsum(blocks)
    blk_start = blk_end - blocks
    n_blocks = -(-(t * TOP_K) // MOE_BLOCK) + N_EXPERTS
    n_slots = n_blocks * MOE_BLOCK
    slot_start = blk_start * MOE_BLOCK
    dest = jnp.sum(jnp.where(top_e[None] == jnp.arange(N_EXPERTS, dtype=I32)[:, None, None],
                             slot_start[:, None, None], 0), axis=0) + top_r
    bi = jnp.arange(n_blocks, dtype=I32)
    blk_expert = jnp.minimum(jnp.sum((bi[:, None] >= blk_end[None, :]).astype(I32), axis=1), N_EXPERTS - 1)
    blk_valid = jnp.clip(counts[blk_expert] - (bi - blk_start[blk_expert]) * MOE_BLOCK, 0, MOE_BLOCK)
    n_used = blk_end[-1:].astype(I32)

    dest_tk = dest.T
    tm_s = ROW_TILE
    dest_s = (dest_tk * PK_TILES).reshape(t // tm_s, 1, tm_s * TOP_K)
    xs = _scatter_rows(dest_s, fpk, n_slots)
    y = _experts(blk_expert.astype(I32), blk_valid.astype(I32), n_used, xs, w_e1, w_e3, w_e2, layer)
    ysh = _shared_expert(fpk, w_s1.astype(BF16), w_s3.astype(BF16), w_s2.astype(BF16))
    tm_c = ROW_TILE // 2
    dest_c = (dest_tk * Y_TILES).reshape(t // tm_c, 1, tm_c * TOP_K)
    w_c = top_w.T.reshape(t // tm_c, 1, tm_c * TOP_K)
    return _combine(dest_c, x_new, ysh, w_c, mod, y, tm=tm_c, n_lat=n_lat, seq=seq, bsz=bsz)


def _pad_cols(w, n):
    return jnp.pad(w, ((0, 0), (0, n - w.shape[1])))


def kernel(x, c, ctx, c_ctx, w_ada, b_ada, g_mix, g_ffn, w_in_even, gq_a, gk_a, conv_w, conv_b, a_log, dt_bias,
           d_skip, g_ssm, w_in_odd, gq_c, gk_c, sink_c, gq_d, gk_d, rpb_d, w_out, w_router, b_router, w_e1, w_e3,
           w_e2, w_s1, w_s3, w_s2):
    bsz, seq, d = x.shape
    n_ctx = ctx.shape[1]
    depth = w_ada.shape[0]
    n_lat = bsz * seq
    geo = dict(n_lat=n_lat, seq=seq, n_ctx=n_ctx, bsz=bsz)
    xs = jnp.concatenate([x.reshape(n_lat, d), ctx.reshape(bsz * n_ctx, d)], axis=0)
    cvec = jnp.zeros((8, d), F32).at[:bsz].set(c).at[bsz].set(c_ctx)
    cos, sin = _rope_tables(seq, n_ctx)
    expand = jnp.zeros((128, SSM_INNER), BF16).at[:SSM_HEADS].set(
        jnp.repeat(jnp.eye(SSM_HEADS, dtype=BF16), SSM_HEAD_DIM, axis=1))

    for layer in range(depth):
        last = layer == depth - 1
        i = layer // 2
        mod = _adaln(cvec, w_ada, b_ada[layer].reshape(1, -1), layer)[:bsz + 1].reshape(bsz + 1, 6, d)
        g1 = g_mix[layer].reshape(1, d)
        if layer % 2 == 0:
            n_in = w_in_even.shape[2]
            n_pad = -(-n_in // 384) * 384
            p = _norm_mod_matmul(xs, g1, mod, _pad_cols(w_in_even[i], n_pad).astype(BF16),
                                 n_lat=n_lat, seq=seq, bsz=bsz, tn=n_pad // 3)
            qt, k, vt = _prep_rope(p, cos, sin, gq_a[i].reshape(1, -1), gk_a[i].reshape(1, -1),
                                   q_scale=HEAD_DIM ** -0.5 * LOG2E, n_q=A_HEADS, n_kv=A_KV_HEADS, **geo)
            y1 = _attention_a(qt, k, vt, need_ctx=not last, **geo)
            xbc = _ssm_conv(p, conv_w[i], conv_b[i].reshape(1, -1), col0=A_IN + SSM_INNER,
                            n_lat=n_lat, seq=seq, n_ctx=n_ctx)
            dt_row = jnp.zeros((1, 128), F32).at[0, :2 * SSM_HEADS].set(dt_bias[i].reshape(-1))
            dt2 = _ssm_dt(p, dt_row, col0=A_IN + SSM_INNER + SSM_CONV_DIM)
            a_rows = jnp.zeros((2, 1, 128), F32).at[:, 0, :SSM_HEADS].set(-jnp.exp(a_log[i]))
            ydir = _ssd_scan(xbc, dt2, a_rows, expand, **geo)
            y2 = _ssd_finish(ydir, xbc, p, jnp.repeat(d_skip[i], SSM_HEAD_DIM).reshape(1, -1),
                             g_ssm[i].reshape(1, -1), z_col0=A_IN)
        else:
            p = _norm_mod_matmul(xs, g1, mod, w_in_odd[i].astype(BF16), n_lat=n_lat, seq=seq, bsz=bsz,
                                 tn=w_in_odd.shape[2] // 3)
            qt, k, vt = _prep_rope(p, cos, sin, gq_c[i].reshape(1, -1), gk_c[i].reshape(1, -1),
                                   q_scale=HEAD_DIM ** -0.5, n_q=A_HEADS, n_kv=A_KV_HEADS, **geo)
            group = A_HEADS // A_KV_HEADS
            sink_rows = jnp.repeat(sink_c[i].reshape(A_KV_HEADS, group), Q_TILE, axis=1).reshape(
                A_KV_HEADS, 1, group * Q_TILE)
            y1 = _attention_c(qt, k, vt, sink_rows, **geo)
            qd, kd, vd = _prep_na(p, gq_d[i].reshape(1, -1), gk_d[i].reshape(1, -1), col0=A_IN, **geo)
            y2 = _attention_d(qd, kd, vd, _na_bias_tables(rpb_d[i], seq // GRID_W), **geo)
        n_rows = n_lat if last else xs.shape[0]
        x_new, fpk, logits_t = _outproj(xs, y1, y2, w_out[layer].astype(BF16), mod, g_ffn[layer].reshape(1, d),
                                        w_router[layer].T, n_rows=n_rows, n_lat=n_lat, seq=seq, bsz=bsz)
        xs = _moe(x_new, fpk, logits_t, mod, b_router[layer], w_e1, w_e3, w_e2,
                  w_s1[layer], w_s3[layer], w_s2[layer], layer=layer, n_lat=n_lat, seq=seq, bsz=bsz)
    return xs[:n_lat].reshape(bsz, seq, d)
```

```python
import functools
import math

import jax
import jax.numpy as jnp
from jax import lax
from jax.experimental import pallas as pl
from jax.experimental.pallas import tpu as pltpu

F32 = jnp.float32
BF16 = jnp.bfloat16
I32 = jnp.int32
U32 = jnp.uint32

EPS = 1e-6
HEAD_DIM = 128
GRID_W = 64
ROPE_THETA = 10000.0
WINDOW = 128
NA_ROWS = 8
NA_COLS = 16
NA_TILE_ROWS = 8
NA_WIN_ROWS = 16
A_HEADS = 8
A_KV_HEADS = 2
SSM_HEADS = 16
SSM_HEAD_DIM = 64
SSM_INNER = SSM_HEADS * SSM_HEAD_DIM
SSM_GROUPS = 2
SSM_STATE = 128
SSM_CONV = 5
SSM_CONV_DIM = SSM_INNER + 2 * SSM_GROUPS * SSM_STATE
SSM_CHUNK = 128
A_IN = (A_HEADS + 2 * A_KV_HEADS) * HEAD_DIM
NA_HEADS = 8
N_EXPERTS = 64
TOP_K = 8
N_GROUPS = 8
TOPK_GROUPS = 4
ROUTE_SCALE = 2.5
MOE_BLOCK = 256

VMEM_LIMIT_BYTES = 56 * 1024 * 1024
ROW_TILE = 256
Q_TILE = 256
VT_ROWS = HEAD_DIM + 16
LOG2E = 1.4426950408889634
HI = lax.Precision.HIGHEST


def _cparams(*sem):
    return pltpu.CompilerParams(dimension_semantics=sem, vmem_limit_bytes=VMEM_LIMIT_BYTES)


def _silu(x):
    return x * jax.nn.sigmoid(x)


def _hi_dot(a, b):
    return jnp.dot(a, b, precision=HI, preferred_element_type=F32)


def _dot(a, b):
    return jnp.dot(a, b, preferred_element_type=F32)


def _dot_nt(a, b):
    return lax.dot_general(a, b, (((1,), (1,)), ((), ())), preferred_element_type=F32)


def _largest_tile(n, cap, mult=128):
    best = None
    for t in range(mult, cap + 1, mult):
        if n % t == 0:
            best = t
    assert best is not None, (n, cap)
    return best


def _adaln_kernel(c_ref, w_ref, b_ref, o_ref):
    o_ref[...] = _hi_dot(_silu(c_ref[...]), w_ref[0]) + b_ref[...]


def _adaln(cvec, w, b, layer):
    _, d, n = w.shape
    tn = 1024
    return pl.pallas_call(
        _adaln_kernel,
        grid=(n // tn,),
        in_specs=[pl.BlockSpec((8, d), lambda j: (0, 0)),
                  pl.BlockSpec((1, d, tn), lambda j: (layer, 0, j)),
                  pl.BlockSpec((1, tn), lambda j: (0, j))],
        out_specs=pl.BlockSpec((8, tn), lambda j: (0, j)),
        out_shape=jax.ShapeDtypeStruct((8, n), F32),
        compiler_params=_cparams("arbitrary"),
        name="adaln",
    )(cvec, w, b)


def _norm_mod(x, g, shift, scale):
    ms = jnp.mean(x * x, axis=-1, keepdims=True)
    return (x * lax.rsqrt(ms + EPS) * g) * (1.0 + scale) + shift


def _nmm_kernel(x_ref, g_ref, mod_ref, w_ref, o_ref, xn_ref):
    @pl.when(pl.program_id(1) == 0)
    def _():
        xn_ref[...] = _norm_mod(x_ref[...], g_ref[...], mod_ref[0, 0:1, :], mod_ref[0, 1:2, :]).astype(BF16)

    o_ref[...] = _dot(xn_ref[...], w_ref[...])


def _group_of_tile(i, tm, n_lat, seq, bsz):
    return jnp.where(i * tm < n_lat, (i * tm) // seq, bsz)


def _norm_mod_matmul(x, g, mod, w_bf16, *, n_lat, seq, bsz, tn):
    t, d = x.shape
    n = w_bf16.shape[1]
    tm = 2 * ROW_TILE
    assert t % tm == 0 and n % tn == 0 and n_lat % tm == 0
    return pl.pallas_call(
        _nmm_kernel,
        grid=(t // tm, n // tn),
        in_specs=[pl.BlockSpec((tm, d), lambda i, j: (i, 0)),
                  pl.BlockSpec((1, d), lambda i, j: (0, 0)),
                  pl.BlockSpec((1, 6, d), lambda i, j: (_group_of_tile(i, tm, n_lat, seq, bsz), 0, 0)),
                  pl.BlockSpec((d, tn), lambda i, j: (0, j))],
        out_specs=pl.BlockSpec((tm, tn), lambda i, j: (i, j)),
        out_shape=jax.ShapeDtypeStruct((t, n), F32),
        scratch_shapes=[pltpu.VMEM((tm, d), BF16)],
        compiler_params=_cparams("arbitrary", "arbitrary"),
        name="norm_mod_inproj",
    )(x, g, mod, w_bf16)


def _head_norm(x, g):
    ms = jnp.mean(x * x, axis=-1, keepdims=True)
    return x * lax.rsqrt(ms + EPS) * g


def _prep_rope_kernel(p_ref, cos_ref, sin_ref, gq_ref, gk_ref, qt_ref, k_ref, vt_ref, *, n_q, n_kv, tq, q_scale):
    cos = cos_ref[...]
    sin = sin_ref[...]
    lane = lax.broadcasted_iota(I32, cos.shape, 1)
    first_half = (lane % 64) < 32

    def norm_rope(x, g):
        y = _head_norm(x, g)
        swapped = jnp.where(first_half, pltpu.roll(y, 96, 1), pltpu.roll(y, 32, 1))
        return y * cos + swapped * sin

    group = n_q // n_kv
    for h in range(n_q):
        q = norm_rope(p_ref[:, h * HEAD_DIM:(h + 1) * HEAD_DIM], gq_ref[...]) * q_scale
        kvh, hh = divmod(h, group)
        qt_ref[0, kvh, 0, :, hh * tq:(hh + 1) * tq] = q.T.astype(BF16)
    for h in range(n_kv):
        c0 = (n_q + h) * HEAD_DIM
        k_ref[0, h] = norm_rope(p_ref[:, c0:c0 + HEAD_DIM], gk_ref[...]).astype(BF16)
        c1 = (n_q + n_kv + h) * HEAD_DIM
        vt_ref[0, h, 0:HEAD_DIM, :] = p_ref[:, c1:c1 + HEAD_DIM].T.astype(BF16)
        vt_ref[0, h, HEAD_DIM:VT_ROWS, :] = jnp.ones((VT_ROWS - HEAD_DIM, tq), BF16)


def _pos_maps(tm, n_lat, seq, n_ctx):
    lat_tiles, seq_tiles, ctx_tiles = n_lat // tm, seq // tm, n_ctx // tm

    def batch_of(i):
        return jnp.where(i < lat_tiles, i // seq_tiles, (i - lat_tiles) // ctx_tiles)

    def pos_of(i):
        return jnp.where(i < lat_tiles, ctx_tiles + i % seq_tiles, (i - lat_tiles) % ctx_tiles)

    return batch_of, pos_of


def _prep_rope(p, cos, sin, gq, gk, *, q_scale, n_q, n_kv, n_lat, seq, n_ctx, bsz):
    t = p.shape[0]
    tm = Q_TILE
    pos_len = n_ctx + seq
    group = n_q // n_kv
    width = (n_q + 2 * n_kv) * HEAD_DIM
    batch_of, pos_of = _pos_maps(tm, n_lat, seq, n_ctx)
    kern = functools.partial(_prep_rope_kernel, n_q=n_q, n_kv=n_kv, tq=tm, q_scale=q_scale)
    return pl.pallas_call(
        kern,
        grid=(t // tm,),
        in_specs=[pl.BlockSpec((tm, width), lambda i: (i, 0)),
                  pl.BlockSpec((tm, HEAD_DIM), lambda i: (pos_of(i), 0)),
                  pl.BlockSpec((tm, HEAD_DIM), lambda i: (pos_of(i), 0)),
                  pl.BlockSpec((1, HEAD_DIM), lambda i: (0, 0)),
                  pl.BlockSpec((1, HEAD_DIM), lambda i: (0, 0))],
        out_specs=[pl.BlockSpec((1, n_kv, 1, HEAD_DIM, group * tm), lambda i: (batch_of(i), 0, pos_of(i), 0, 0)),
                   pl.BlockSpec((1, n_kv, tm, HEAD_DIM), lambda i: (batch_of(i), 0, pos_of(i), 0)),
                   pl.BlockSpec((1, n_kv, VT_ROWS, tm), lambda i: (batch_of(i), 0, 0, pos_of(i)))],
        out_shape=[jax.ShapeDtypeStruct((bsz, n_kv, pos_len // tm, HEAD_DIM, group * tm), BF16),
                   jax.ShapeDtypeStruct((bsz, n_kv, pos_len, HEAD_DIM), BF16),
                   jax.ShapeDtypeStruct((bsz, n_kv, VT_ROWS, pos_len), BF16)],
        compiler_params=_cparams("arbitrary"),
        name="prep_rope",
    )(p, cos, sin, gq, gk)


def _rope_tables(seq, n_ctx):
    t = jnp.arange(seq, dtype=I32)
    row = (t // GRID_W).astype(F32)
    col = (t % GRID_W).astype(F32)
    n_freq = HEAD_DIM // 4
    inv = ROPE_THETA ** (-jnp.arange(n_freq, dtype=F32) / n_freq)
    ar, ac = row[:, None] * inv, col[:, None] * inv
    cos = jnp.concatenate([jnp.cos(ar), jnp.cos(ar), jnp.cos(ac), jnp.cos(ac)], axis=1)
    sin = jnp.concatenate([-jnp.sin(ar), jnp.sin(ar), -jnp.sin(ac), jnp.sin(ac)], axis=1)
    cos = jnp.concatenate([jnp.ones((n_ctx, HEAD_DIM), F32), cos], axis=0)
    sin = jnp.concatenate([jnp.zeros((n_ctx, HEAD_DIM), F32), sin], axis=0)
    return cos, sin


def _flash_t_kernel(qt_ref, k_ref, vt_ref, o_ref, m_ref, l_ref, acc_ref, sa_ref, sb_ref, *, tk, n_k, tq, group):
    qt = qt_ref[0, 0, 0]
    m_ref[...] = jnp.full(m_ref.shape, -jnp.inf, F32)
    l_ref[...] = jnp.zeros(l_ref.shape, F32)
    acc_ref[...] = jnp.zeros(acc_ref.shape, F32)

    def scores(j, dst):
        off = pl.multiple_of(jnp.minimum(j, n_k - 1) * tk, tk)
        dst[...] = _dot(k_ref[0, 0, pl.ds(off, tk), :], qt)

    def consume(j, src):
        off = pl.multiple_of(j * tk, tk)
        s = src[...]
        m_old = m_ref[...]
        m_new = jnp.maximum(m_old, jnp.max(s, axis=0, keepdims=True))
        alpha = jnp.exp2(m_old - m_new)
        p = jnp.exp2(s - m_new).astype(BF16)
        r = _dot(vt_ref[0, 0, :, pl.ds(off, tk)], p)
        acc_ref[...] = alpha * acc_ref[...] + r[0:HEAD_DIM]
        l_ref[...] = alpha * l_ref[...] + r[HEAD_DIM:HEAD_DIM + 1]
        m_ref[...] = m_new

    scores(0, sa_ref)

    def body(jj, carry):
        j = 2 * jj
        scores(j + 1, sb_ref)
        consume(j, sa_ref)
        scores(j + 2, sa_ref)
        consume(j + 1, sb_ref)
        return carry

    lax.fori_loop(0, n_k // 2, body, 0)
    if n_k % 2:
        consume(n_k - 1, sa_ref)
    out = acc_ref[...] / l_ref[...]
    for h in range(group):
        o_ref[:, h * HEAD_DIM:(h + 1) * HEAD_DIM] = out[:, h * tq:(h + 1) * tq].T.astype(o_ref.dtype)


def _attention_a(qt, k, vt, *, n_lat, seq, n_ctx, bsz, need_ctx):
    n_kv, group, tq = k.shape[1], qt.shape[-1] // Q_TILE, Q_TILE
    pos_len = n_ctx + seq
    width = n_kv * group * HEAD_DIM
    tk = _largest_tile(pos_len, 1280)

    def scratch(tkk):
        n = group * tq
        return [pltpu.VMEM((1, n), F32), pltpu.VMEM((1, n), F32), pltpu.VMEM((HEAD_DIM, n), F32),
                pltpu.VMEM((tkk, n), F32), pltpu.VMEM((tkk, n), F32)]

    y = pl.pallas_call(
        functools.partial(_flash_t_kernel, tk=tk, n_k=pos_len // tk, tq=tq, group=group),
        grid=(bsz, n_kv, seq // tq),
        in_specs=[pl.BlockSpec((1, 1, 1, HEAD_DIM, group * tq), lambda b, h, i: (b, h, n_ctx // tq + i, 0, 0)),
                  pl.BlockSpec((1, 1, pos_len, HEAD_DIM), lambda b, h, i: (b, h, 0, 0)),
                  pl.BlockSpec((1, 1, VT_ROWS, pos_len), lambda b, h, i: (b, h, 0, 0))],
        out_specs=pl.BlockSpec((tq, group * HEAD_DIM), lambda b, h, i: (b * (seq // tq) + i, h)),
        out_shape=jax.ShapeDtypeStruct((n_lat, width), BF16),
        scratch_shapes=scratch(tk),
        compiler_params=_cparams("arbitrary", "arbitrary", "arbitrary"),
        name="attn_a_latent",
    )(qt, k, vt)
    if not need_ctx:
        return y
    tkc = _largest_tile(n_ctx, 768)
    y_ctx = pl.pallas_call(
        functools.partial(_flash_t_kernel, tk=tkc, n_k=n_ctx // tkc, tq=tq, group=group),
        grid=(bsz, n_kv, n_ctx // tq),
        in_specs=[pl.BlockSpec((1, 1, 1, HEAD_DIM, group * tq), lambda b, h, i: (b, h, i, 0, 0)),
                  pl.BlockSpec((1, 1, n_ctx, HEAD_DIM), lambda b, h, i: (b, h, 0, 0)),
                  pl.BlockSpec((1, 1, VT_ROWS, n_ctx), lambda b, h, i: (b, h, 0, 0))],
        out_specs=pl.BlockSpec((tq, group * HEAD_DIM), lambda b, h, i: (b * (n_ctx // tq) + i, h)),
        out_shape=jax.ShapeDtypeStruct((bsz * n_ctx, width), BF16),
        scratch_shapes=scratch(tkc),
        compiler_params=_cparams("arbitrary", "arbitrary", "arbitrary"),
        name="attn_a_context",
    )(qt, k, vt)
    return jnp.concatenate([y, y_ctx], axis=0)


def _segment_edges(i, tm, n_lat, seq, n_ctx):
    lat_tiles, seq_tiles, ctx_tiles = n_lat // tm, seq // tm, n_ctx // tm
    is_lat = i < lat_tiles
    pos = jnp.where(is_lat, i % seq_tiles, (i - lat_tiles) % ctx_tiles)
    last = jnp.where(is_lat, seq_tiles, ctx_tiles) - 1
    return pos == 0, pos == last


def _conv_kernel(cur_ref, prev_ref, next_ref, w_ref, b_ref, o_ref, ext_ref, *, tm, n_lat, seq, n_ctx):
    first, last = _segment_edges(pl.program_id(0), tm, n_lat, seq, n_ctx)
    ext_ref[0:8, :] = jnp.where(first, 0.0, prev_ref[...])
    ext_ref[8:8 + tm, :] = cur_ref[...]
    ext_ref[8 + tm:16 + tm, :] = jnp.where(last, 0.0, next_ref[...])
    acc = jnp.broadcast_to(b_ref[...], (tm, b_ref.shape[1]))
    half = SSM_CONV // 2
    for kk in range(SSM_CONV):
        acc = acc + w_ref[kk:kk + 1, :] * ext_ref[pl.ds(8 - half + kk, tm), :]
    o_ref[...] = _silu(acc)


def _ssm_conv(p, conv_w, conv_b, *, col0, n_lat, seq, n_ctx):
    t = p.shape[0]
    tm, tc = ROW_TILE, 512
    assert col0 % tc == 0 and SSM_CONV_DIM % tc == 0
    cb0 = col0 // tc
    r8 = tm // 8
    kern = functools.partial(_conv_kernel, tm=tm, n_lat=n_lat, seq=seq, n_ctx=n_ctx)
    return pl.pallas_call(
        kern,
        grid=(t // tm, SSM_CONV_DIM // tc),
        in_specs=[pl.BlockSpec((tm, tc), lambda i, j: (i, cb0 + j)),
                  pl.BlockSpec((8, tc), lambda i, j: (jnp.maximum(i * r8 - 1, 0), cb0 + j)),
                  pl.BlockSpec((8, tc), lambda i, j: (jnp.minimum((i + 1) * r8, t // 8 - 1), cb0 + j)),
                  pl.BlockSpec((SSM_CONV, tc), lambda i, j: (0, j)),
                  pl.BlockSpec((1, tc), lambda i, j: (0, j))],
        out_specs=pl.BlockSpec((tm, tc), lambda i, j: (i, j)),
        out_shape=jax.ShapeDtypeStruct((t, SSM_CONV_DIM), F32),
        scratch_shapes=[pltpu.VMEM((tm + 16, tc), F32)],
        compiler_params=_cparams("arbitrary", "arbitrary"),
        name="ssm_conv",
    )(p, p, p, conv_w, conv_b)


def _dt_kernel(p_ref, b_ref, o_ref):
    x = p_ref[...] + b_ref[...]
    sp = jnp.maximum(x, 0.0) + jnp.log1p(jnp.exp(-jnp.abs(x)))
    o_ref[0] = sp
    o_ref[1] = pltpu.roll(sp, HEAD_DIM - SSM_HEADS, 1)


def _ssm_dt(p, bias_row, *, col0):
    t = p.shape[0]
    tm = 2 * ROW_TILE
    assert col0 % 128 == 0
    return pl.pallas_call(
        _dt_kernel,
        grid=(t // tm,),
        in_specs=[pl.BlockSpec((tm, 128), lambda i: (i, col0 // 128)),
                  pl.BlockSpec((1, 128), lambda i: (0, 0))],
        out_specs=pl.BlockSpec((2, tm, 128), lambda i: (0, i, 0)),
        out_shape=jax.ShapeDtypeStruct((2, t, 128), F32),
        compiler_params=_cparams("arbitrary"),
        name="ssm_dt",
    )(p, bias_row)


def _ssd_kernel(x_ref, bc_ref, dt_ref, a_ref, e_ref, y_ref, h_ref):
    d = pl.program_id(1)
    lc = SSM_CHUNK
    gw = SSM_INNER // SSM_GROUPS
    hpg = SSM_HEADS // SSM_GROUPS

    @pl.when(pl.program_id(2) == 0)
    def _():
        h_ref[...] = jnp.zeros(h_ref.shape, F32)

    dt = dt_ref[0]
    w = dt * a_ref[0]
    row = lax.broadcasted_iota(I32, (lc, lc), 0)
    col = lax.broadcasted_iota(I32, (lc, lc), 1)
    fwd = d == 0
    sgn = jnp.where(fwd, 1, -1)
    allowed = (row - col) * sgn >= 0
    allowed_t = (col - row) * sgn >= 0
    cs = _hi_dot(allowed.astype(F32), w)
    cs_t = _hi_dot(w.T, allowed_t.astype(F32))
    dt_t = dt.T
    tot = jnp.where(fwd, cs[lc - 1:lc, :], cs[0:1, :])
    e = e_ref[...]

    def expand(a):
        hi = a.astype(BF16)
        lo = (a - hi.astype(F32)).astype(BF16)
        return _dot(hi, e) + _dot(lo, e)

    x = x_ref[...]
    xw = (x * expand(jnp.exp(tot - cs) * dt)).astype(BF16)
    off_scale = expand(jnp.exp(cs))
    state_decay = expand(jnp.exp(jnp.broadcast_to(tot, (lc, 128))))
    xb = x.astype(BF16)
    for g in range(SSM_GROUPS):
        bm = bc_ref[:, g * SSM_STATE:(g + 1) * SSM_STATE]
        cm = bc_ref[:, (SSM_GROUPS + g) * SSM_STATE:(SSM_GROUPS + g + 1) * SSM_STATE].astype(BF16)
        cb = _dot_nt(cm, bm.astype(BF16))
        h_prev = h_ref[g]
        y_off = _dot(cm, h_prev.astype(BF16))
        s_chunk = _dot(bm.T.astype(BF16), xw[:, g * gw:(g + 1) * gw])
        h_ref[g] = state_decay[:, g * gw:(g + 1) * gw] * h_prev + s_chunk
        pieces = []
        for r in range(hpg):
            hh = g * hpg + r
            seg = cs[:, hh:hh + 1] - cs_t[hh:hh + 1, :]
            dec = jnp.exp(jnp.where(allowed, seg, -jnp.inf))
            mix = (cb * dec * dt_t[hh:hh + 1, :]).astype(BF16)
            pieces.append(_dot(mix, xb[:, hh * SSM_HEAD_DIM:(hh + 1) * SSM_HEAD_DIM]))
        y_ref[0, :, g * gw:(g + 1) * gw] = (jnp.concatenate(pieces, axis=1)
                                           + y_off * off_scale[:, g * gw:(g + 1) * gw])


def _ssd_scan(xbc, dt2, a_rows, expand, *, n_lat, seq, n_ctx, bsz):
    t = xbc.shape[0]
    lc = SSM_CHUNK
    ctx_chunks, lat_chunks = n_ctx // lc, seq // lc
    gw = SSM_INNER // SSM_GROUPS

    def rowblk(b, d, s):
        in_ctx = s < ctx_chunks
        j_ctx = jnp.where(d == 0, s, ctx_chunks - 1 - s)
        sl = s - ctx_chunks
        j_lat = jnp.where(d == 0, sl, lat_chunks - 1 - sl)
        return jnp.where(in_ctx, (n_lat + b * n_ctx) // lc + j_ctx, b * lat_chunks + j_lat)

    return pl.pallas_call(
        _ssd_kernel,
        grid=(bsz, 2, ctx_chunks + lat_chunks),
        in_specs=[pl.BlockSpec((lc, SSM_INNER), lambda b, d, s: (rowblk(b, d, s), 0)),
                  pl.BlockSpec((lc, 2 * SSM_GROUPS * SSM_STATE),
                               lambda b, d, s: (rowblk(b, d, s), SSM_INNER // (2 * SSM_GROUPS * SSM_STATE))),
                  pl.BlockSpec((1, lc, 128), lambda b, d, s: (d, rowblk(b, d, s), 0)),
                  pl.BlockSpec((1, 1, 128), lambda b, d, s: (d, 0, 0)),
                  pl.BlockSpec((128, SSM_INNER), lambda b, d, s: (0, 0))],
        out_specs=pl.BlockSpec((1, lc, SSM_INNER), lambda b, d, s: (d, rowblk(b, d, s), 0)),
        out_shape=jax.ShapeDtypeStruct((2, t, SSM_INNER), F32),
        scratch_shapes=[pltpu.VMEM((SSM_GROUPS, SSM_STATE, gw), F32)],
        compiler_params=_cparams("arbitrary", "arbitrary", "arbitrary"),
        name="ssd_scan",
    )(xbc, xbc, dt2, a_rows, expand)


def _ssd_finish_kernel(yf_ref, yb_ref, xs_ref, z0_ref, z1_ref, dsk_ref, g_ref, o_ref):
    gw = SSM_INNER // SSM_GROUPS
    zs = (z0_ref, z1_ref)
    for g in range(SSM_GROUPS):
        sl = slice(g * gw, (g + 1) * gw)
        y = yf_ref[0, :, sl] + yb_ref[0, :, sl] + dsk_ref[:, sl] * xs_ref[:, sl]
        y = y * _silu(zs[g][...])
        o_ref[:, sl] = _head_norm(y, g_ref[:, sl]).astype(o_ref.dtype)


def _ssd_finish(ydir, xbc, p, dsk_row, g_row, *, z_col0):
    t = xbc.shape[0]
    tm = ROW_TILE
    gw = SSM_INNER // SSM_GROUPS
    assert z_col0 % gw == 0 and SSM_GROUPS == 2
    zb = z_col0 // gw
    return pl.pallas_call(
        _ssd_finish_kernel,
        grid=(t // tm,),
        in_specs=[pl.BlockSpec((1, tm, SSM_INNER), lambda i: (0, i, 0)),
                  pl.BlockSpec((1, tm, SSM_INNER), lambda i: (1, i, 0)),
                  pl.BlockSpec((tm, SSM_INNER), lambda i: (i, 0)),
                  pl.BlockSpec((tm, gw), lambda i: (i, zb)),
                  pl.BlockSpec((tm, gw), lambda i: (i, zb + 1)),
                  pl.BlockSpec((1, SSM_INNER), lambda i: (0, 0)),
                  pl.BlockSpec((1, SSM_INNER), lambda i: (0, 0))],
        out_specs=pl.BlockSpec((tm, SSM_INNER), lambda i: (i, 0)),
        out_shape=jax.ShapeDtypeStruct((t, SSM_INNER), BF16),
        compiler_params=_cparams("arbitrary"),
        name="ssd_finish",
    )(ydir, ydir, xbc, p, p, dsk_row, g_row)


def _window_kernel(qt_ref, k_ref, vt_ref, sink_ref, o_ref, *, tq, group, n_ctx, seq):
    band = tq + 2 * WINDOW
    pos_len = n_ctx + seq
    q0 = pl.program_id(2) * tq
    start = pl.multiple_of(jnp.minimum(n_ctx + q0 - WINDOW, pos_len - band), 128)
    qt = qt_ref[0, 0, 0]
    n = group * tq
    s_cx = _dot(k_ref[0, 0, 0:n_ctx, :], qt)
    s_w = _dot(k_ref[0, 0, pl.ds(start, band), :], qt)
    kpos = start - n_ctx + lax.broadcasted_iota(I32, (band, n), 0)
    qpos = q0 + lax.broadcasted_iota(I32, (band, n), 1) % tq
    valid = (jnp.abs(qpos - kpos) <= WINDOW) & (kpos >= 0)
    s_w = jnp.where(valid, s_w, -jnp.inf)
    sink = sink_ref[0]
    m = jnp.maximum(jnp.maximum(jnp.max(s_cx, axis=0, keepdims=True), jnp.max(s_w, axis=0, keepdims=True)), sink)
    p_cx = jnp.exp(s_cx - m)
    p_w = jnp.exp(s_w - m)
    l = jnp.sum(p_cx, axis=0, keepdims=True) + jnp.sum(p_w, axis=0, keepdims=True) + jnp.exp(sink - m)
    acc = (_dot(vt_ref[0, 0, 0:HEAD_DIM, 0:n_ctx], p_cx.astype(BF16))
           + _dot(vt_ref[0, 0, 0:HEAD_DIM, pl.ds(start, band)], p_w.astype(BF16)))
    out = acc / l
    for h in range(group):
        o_ref[:, h * HEAD_DIM:(h + 1) * HEAD_DIM] = out[:, h * tq:(h + 1) * tq].T.astype(o_ref.dtype)


def _attention_c(qt, k, vt, sink_rows, *, n_lat, seq, n_ctx, bsz):
    n_kv, group, tq = k.shape[1], qt.shape[-1] // Q_TILE, Q_TILE
    pos_len = n_ctx + seq
    assert n_ctx >= WINDOW and n_ctx % 128 == 0 and pos_len >= tq + 2 * WINDOW
    return pl.pallas_call(
        functools.partial(_window_kernel, tq=tq, group=group, n_ctx=n_ctx, seq=seq),
        grid=(bsz, n_kv, seq // tq),
        in_specs=[pl.BlockSpec((1, 1, 1, HEAD_DIM, group * tq), lambda b, h, i: (b, h, n_ctx // tq + i, 0, 0)),
                  pl.BlockSpec((1, 1, pos_len, HEAD_DIM), lambda b, h, i: (b, h, 0, 0)),
                  pl.BlockSpec((1, 1, VT_ROWS, pos_len), lambda b, h, i: (b, h, 0, 0)),
                  pl.BlockSpec((1, 1, group * tq), lambda b, h, i: (h, 0, 0))],
        out_specs=pl.BlockSpec((tq, group * HEAD_DIM), lambda b, h, i: (b * (seq // tq) + i, h)),
        out_shape=jax.ShapeDtypeStruct((n_lat, n_kv * group * HEAD_DIM), BF16),
        compiler_params=_cparams("arbitrary", "arbitrary", "arbitrary"),
        name="attn_c_window",
    )(qt, k, vt, sink_rows)


def _prep_na(p, gq, gk, *, col0, n_lat, seq, n_ctx, bsz):
    t = p.shape[0]
    tm = ROW_TILE
    hw = NA_HEADS * HEAD_DIM
    assert col0 % hw == 0 or (2 * col0) % hw == 0
    pos_len = n_ctx + seq
    batch_of, pos_of = _pos_maps(tm, n_lat, seq, n_ctx)
    lat_tiles, seq_tiles, ctx_tiles = n_lat // tm, seq // tm, n_ctx // tm

    def qpos_of(i):
        return jnp.where(i < lat_tiles, i % seq_tiles, seq_tiles + (i - lat_tiles) % ctx_tiles)

    cw = hw // 2
    cb = col0 // cw
    shp = jax.ShapeDtypeStruct((bsz, NA_HEADS, pos_len, HEAD_DIM), BF16)
    kv_spec = pl.BlockSpec((1, NA_HEADS, tm, HEAD_DIM), lambda i: (batch_of(i), 0, pos_of(i), 0))
    q_spec = pl.BlockSpec((1, NA_HEADS, tm, HEAD_DIM), lambda i: (batch_of(i), 0, qpos_of(i), 0))

    def kern(q0, q1, k0, k1, v0, v1, gq_ref, gk_ref, q_ref, k_ref, v_ref):
        halves = NA_HEADS // 2
        for h in range(NA_HEADS):
            sl = slice((h % halves) * HEAD_DIM, (h % halves + 1) * HEAD_DIM)
            pq, pk, pv = ((q0, k0, v0) if h < halves else (q1, k1, v1))
            q_ref[0, h] = (_head_norm(pq[:, sl], gq_ref[...]) * (HEAD_DIM ** -0.5)).astype(BF16)
            k_ref[0, h] = _head_norm(pk[:, sl], gk_ref[...]).astype(BF16)
            v_ref[0, h] = pv[:, sl].astype(BF16)

    def col_spec(j):
        return pl.BlockSpec((tm, cw), lambda i: (i, cb + j))

    return pl.pallas_call(
        kern,
        grid=(t // tm,),
        in_specs=[col_spec(j) for j in range(6)] + [pl.BlockSpec((1, HEAD_DIM), lambda i: (0, 0)),
                                                    pl.BlockSpec((1, HEAD_DIM), lambda i: (0, 0))],
        out_specs=[q_spec, kv_spec, kv_spec],
        out_shape=[shp, shp, shp],
        compiler_params=_cparams("arbitrary"),
        name="prep_na",
    )(p, p, p, p, p, p, gq, gk)


def _na_kernel(q_ref, k_ref, v_ref, bias_ref, o_ref, *, n_ctx, rows):
    tq = NA_TILE_ROWS * GRID_W
    win = NA_WIN_ROWS * GRID_W
    ti = pl.program_id(2)
    w0 = jnp.clip(ti * NA_TILE_ROWS - NA_ROWS // 2, 0, rows - NA_WIN_ROWS)
    start = pl.multiple_of(n_ctx + w0 * GRID_W, GRID_W)
    q = q_ref[0, 0]
    s_cx = _dot_nt(q, k_ref[0, 0, 0:n_ctx, :])
    s_nb = _dot_nt(q, k_ref[0, 0, pl.ds(start, win), :]) + bias_ref[0, 0]
    m = jnp.maximum(jnp.max(s_cx, axis=1, keepdims=True), jnp.max(s_nb, axis=1, keepdims=True))
    p_cx = jnp.exp(s_cx - m)
    p_nb = jnp.exp(s_nb - m)
    l = jnp.sum(p_cx, axis=1, keepdims=True) + jnp.sum(p_nb, axis=1, keepdims=True)
    acc = _dot(p_cx.astype(BF16), v_ref[0, 0, 0:n_ctx, :]) + _dot(p_nb.astype(BF16), v_ref[0, 0, pl.ds(start, win), :])
    o_ref[...] = (acc / l).astype(o_ref.dtype)


def _na_bias_tables(rpb, rows):
    n_tiles = rows // NA_TILE_ROWS
    tables = []
    for ti in (0, 1, n_tiles - 1):
        w0 = min(max(ti * NA_TILE_ROWS - NA_ROWS // 2, 0), rows - NA_WIN_ROWS)
        qr = ti * NA_TILE_ROWS + jnp.arange(NA_TILE_ROWS)
        r0 = jnp.clip(qr - NA_ROWS // 2, 0, rows - NA_ROWS)
        kr = w0 + jnp.arange(NA_WIN_ROWS)
        row_ok = (kr[None, :] >= r0[:, None]) & (kr[None, :] < r0[:, None] + NA_ROWS)
        dr = jnp.clip(kr[None, :] - qr[:, None] + NA_ROWS - 1, 0, 2 * NA_ROWS - 2)
        qc = jnp.arange(GRID_W)
        c0 = jnp.clip(qc - NA_COLS // 2, 0, GRID_W - NA_COLS)
        kc = jnp.arange(GRID_W)
        col_ok = (kc[None, :] >= c0[:, None]) & (kc[None, :] < c0[:, None] + NA_COLS)
        dc = jnp.clip(kc[None, :] - qc[:, None] + NA_COLS - 1, 0, 2 * NA_COLS - 2)
        sel_r = ((dr[:, :, None] == jnp.arange(2 * NA_ROWS - 1)) & row_ok[:, :, None]).astype(F32)
        sel_c = ((dc[:, :, None] == jnp.arange(2 * NA_COLS - 1)) & col_ok[:, :, None]).astype(F32)
        b = jnp.einsum("rki,hij,cqj->hrckq", sel_r, rpb.astype(F32), sel_c, precision=HI)
        ok = row_ok[:, None, :, None] & col_ok[None, :, None, :]
        b = jnp.where(ok[None], b, -jnp.inf)
        tables.append(b.reshape(rpb.shape[0], NA_TILE_ROWS * GRID_W, NA_WIN_ROWS * GRID_W))
    return jnp.stack(tables, axis=0).astype(F32)


def _attention_d(q, k, v, bias, *, n_lat, seq, n_ctx, bsz):
    rows = seq // GRID_W
    tq = NA_TILE_ROWS * GRID_W
    win = NA_WIN_ROWS * GRID_W
    n_tiles = rows // NA_TILE_ROWS
    pos_len = n_ctx + seq
    assert rows % NA_TILE_ROWS == 0 and rows >= NA_WIN_ROWS and n_tiles >= 3

    def kind(i):
        return jnp.where(i == 0, 0, jnp.where(i == n_tiles - 1, 2, 1))

    return pl.pallas_call(
        functools.partial(_na_kernel, n_ctx=n_ctx, rows=rows),
        grid=(bsz, NA_HEADS, n_tiles),
        in_specs=[pl.BlockSpec((1, 1, tq, HEAD_DIM), lambda b, h, i: (b, h, i, 0)),
                  pl.BlockSpec((1, 1, pos_len, HEAD_DIM), lambda b, h, i: (b, h, 0, 0)),
                  pl.BlockSpec((1, 1, pos_len, HEAD_DIM), lambda b, h, i: (b, h, 0, 0)),
                  pl.BlockSpec((1, 1, tq, win), lambda b, h, i: (kind(i), h, 0, 0))],
        out_specs=pl.BlockSpec((tq, HEAD_DIM), lambda b, h, i: (b * n_tiles + i, h)),
        out_shape=jax.ShapeDtypeStruct((n_lat, NA_HEADS * HEAD_DIM), BF16),
        compiler_params=_cparams("arbitrary", "arbitrary", "arbitrary"),
        name="attn_d_neighbourhood",
    )(q, k, v, bias)


def _pack_bf16_pairs(f):
    half = f.shape[1] // 2
    bits = lax.bitcast_convert_type(f.astype(BF16).astype(F32), U32)
    lo = lax.shift_right_logical(bits[:, :half], jnp.uint32(16))
    hi = bits[:, half:] & jnp.uint32(0xFFFF0000)
    return hi | lo


def _unpack_bf16_pairs(bits):
    lo = lax.bitcast_convert_type(lax.shift_left(bits, jnp.uint32(16)), F32).astype(BF16)
    hi = lax.bitcast_convert_type(bits & jnp.uint32(0xFFFF0000), F32).astype(BF16)
    return lo, hi


def _store_row_tiles(ref, val):
    rows, width = val.shape
    per = width // 128
    for s in range(per):
        ref[pl.ds(s, rows, stride=per), :] = val[:, s * 128:(s + 1) * 128]


def _load_row_tiles(ref, rows, per):
    return jnp.concatenate([ref[pl.ds(s, rows, stride=per), :] for s in range(per)], axis=1)


def _outproj_kernel(x_ref, y1_ref, y2_ref, w_ref, mod_ref, g_ref, wrh_ref, wrl_ref, xo_ref, fpk_ref, lg_ref):
    half = w_ref.shape[0] // 2
    delta = _dot(y1_ref[...], w_ref[0:half, :]) + _dot(y2_ref[...], w_ref[half:, :])
    xn = x_ref[...] + mod_ref[0, 2:3, :] * delta
    xo_ref[...] = xn
    f = _norm_mod(xn, g_ref[...], mod_ref[0, 3:4, :], mod_ref[0, 4:5, :])
    f_hi = f.astype(BF16)
    f_lo = (f - f_hi.astype(F32)).astype(BF16)
    lg_ref[...] = _dot_nt(wrh_ref[...], f_hi) + (_dot_nt(wrl_ref[...], f_hi) + _dot_nt(wrh_ref[...], f_lo))
    _store_row_tiles(fpk_ref, _pack_bf16_pairs(f))


def _outproj(x, y1, y2, w_bf16, mod, g, wr_t, *, n_rows, n_lat, seq, bsz):
    d = x.shape[1]
    tm = ROW_TILE
    hw = y1.shape[1]
    wr_hi = wr_t.astype(BF16)
    wr_lo = (wr_t - wr_hi.astype(F32)).astype(BF16)
    return pl.pallas_call(
        _outproj_kernel,
        grid=(n_rows // tm,),
        in_specs=[pl.BlockSpec((tm, d), lambda i: (i, 0)),
                  pl.BlockSpec((tm, hw), lambda i: (i, 0)),
                  pl.BlockSpec((tm, hw), lambda i: (i, 0)),
                  pl.BlockSpec((2 * hw, d), lambda i: (0, 0)),
                  pl.BlockSpec((1, 6, d), lambda i: (_group_of_tile(i, tm, n_lat, seq, bsz), 0, 0)),
                  pl.BlockSpec((1, d), lambda i: (0, 0)),
                  pl.BlockSpec((N_EXPERTS, d), lambda i: (0, 0)),
                  pl.BlockSpec((N_EXPERTS, d), lambda i: (0, 0))],
        out_specs=[pl.BlockSpec((tm, d), lambda i: (i, 0)),
                   pl.BlockSpec((tm * (d // 256), 128), lambda i: (i, 0)),
                   pl.BlockSpec((N_EXPERTS, tm), lambda i: (0, i))],
        out_shape=[jax.ShapeDtypeStruct((n_rows, d), F32),
                   jax.ShapeDtypeStruct((n_rows * (d // 256), 128), U32),
                   jax.ShapeDtypeStruct((N_EXPERTS, n_rows), F32)],
        compiler_params=_cparams("arbitrary"),
        name="outproj_residual_moe_in",
    )(x, y1, y2, w_bf16, mod, g, wr_hi, wr_lo)


def _route_kernel(lg_ref, b_ref, e_ref, r_ref, w_ref, cnt_ref, run_ref):
    tr = lg_ref.shape[1]
    per = N_EXPERTS // N_GROUPS
    neg = -jnp.inf

    @pl.when(pl.program_id(0) == 0)
    def _():
        run_ref[...] = jnp.zeros(run_ref.shape, F32)

    scores = jax.nn.sigmoid(lg_ref[...])
    sel = scores + b_ref[...]
    sel3 = sel.reshape(N_GROUPS, per, tr)
    mem = lax.broadcasted_iota(I32, (N_GROUPS, per, tr), 1)
    m1 = jnp.max(sel3, axis=1, keepdims=True)
    i1 = jnp.min(jnp.where(sel3 == m1, mem, per), axis=1, keepdims=True)
    m2 = jnp.max(jnp.where(mem == i1, neg, sel3), axis=1, keepdims=True)
    gs = (m1 + m2).reshape(N_GROUPS, tr)
    gid = lax.broadcasted_iota(I32, (N_GROUPS, tr), 0)
    keep = jnp.zeros((N_GROUPS, tr), F32)
    for _ in range(TOPK_GROUPS):
        gm = jnp.max(gs, axis=0, keepdims=True)
        gi = jnp.min(jnp.where(gs == gm, gid, N_GROUPS), axis=0, keepdims=True)
        hit = gid == gi
        keep = jnp.where(hit, 1.0, keep)
        gs = jnp.where(hit, neg, gs)
    keep3 = jnp.broadcast_to(keep.reshape(N_GROUPS, 1, tr), (N_GROUPS, per, tr))
    cand = jnp.where(keep3 > 0.5, sel3, neg).reshape(N_EXPERTS, tr)
    eid = lax.broadcasted_iota(I32, (N_EXPERTS, tr), 0)
    idxs, ws = [], []
    sel_f = jnp.zeros((N_EXPERTS, tr), F32)
    for _ in range(TOP_K):
        cm = jnp.max(cand, axis=0, keepdims=True)
        ci = jnp.min(jnp.where(cand == cm, eid, N_EXPERTS), axis=0, keepdims=True)
        hit = eid == ci
        idxs.append(ci)
        ws.append(jnp.sum(jnp.where(hit, scores, 0.0), axis=0, keepdims=True))
        sel_f = jnp.where(hit, 1.0, sel_f)
        cand = jnp.where(hit, neg, cand)
    before = (lax.broadcasted_iota(I32, (tr, tr), 0) < lax.broadcasted_iota(I32, (tr, tr), 1))
    rank = _dot(sel_f.astype(BF16), jnp.where(before, 1.0, 0.0).astype(BF16)) + run_ref[...]
    run_ref[...] = run_ref[...] + jnp.sum(sel_f, axis=1, keepdims=True)
    cnt_ref[...] = run_ref[...]
    wsum = ws[0]
    for kk in range(1, TOP_K):
        wsum = wsum + ws[kk]
    for kk in range(TOP_K):
        e_ref[kk:kk + 1, :] = idxs[kk]
        r_ref[kk:kk + 1, :] = jnp.sum(jnp.where(eid == idxs[kk], rank, 0.0), axis=0, keepdims=True).astype(I32)
        w_ref[kk:kk + 1, :] = ws[kk] / wsum * ROUTE_SCALE


def _route(logits_t, b_col):
    n_e, t = logits_t.shape
    tr = ROW_TILE
    return pl.pallas_call(
        _route_kernel,
        grid=(t // tr,),
        in_specs=[pl.BlockSpec((n_e, tr), lambda i: (0, i)),
                  pl.BlockSpec((n_e, 1), lambda i: (0, 0))],
        out_specs=[pl.BlockSpec((TOP_K, tr), lambda i: (0, i)),
                   pl.BlockSpec((TOP_K, tr), lambda i: (0, i)),
                   pl.BlockSpec((TOP_K, tr), lambda i: (0, i)),
                   pl.BlockSpec((n_e, 1), lambda i: (0, 0))],
        out_shape=[jax.ShapeDtypeStruct((TOP_K, t), I32),
                   jax.ShapeDtypeStruct((TOP_K, t), I32),
                   jax.ShapeDtypeStruct((TOP_K, t), F32),
                   jax.ShapeDtypeStruct((n_e, 1), F32)],
        scratch_shapes=[pltpu.VMEM((n_e, 1), F32)],
        compiler_params=_cparams("arbitrary"),
        name="moe_route",
    )(logits_t, b_col)


PK_TILES = 8
Y_TILES = 16


def _scatter_kernel(dest_ref, f_ref, xs_ref, sem, *, tm):
    def copy(t, kk):
        src = pl.multiple_of(t * PK_TILES, PK_TILES)
        dst = pl.multiple_of(dest_ref[0, 0, t * TOP_K + kk], PK_TILES)
        return pltpu.make_async_copy(f_ref.at[pl.ds(src, PK_TILES)], xs_ref.at[pl.ds(dst, PK_TILES)], sem)

    def issue(t, c):
        for kk in range(TOP_K):
            copy(t, kk).start()
        return c

    def drain(t, c):
        for kk in range(TOP_K):
            copy(t, kk).wait()
        return c

    lax.fori_loop(0, tm, issue, 0)
    lax.fori_loop(0, tm, drain, 0)


def _scatter_rows(dest3, fpk, n_slots):
    tm = ROW_TILE
    t = fpk.shape[0] // PK_TILES
    return pl.pallas_call(
        functools.partial(_scatter_kernel, tm=tm),
        grid=(t // tm,),
        in_specs=[pl.BlockSpec((1, 1, TOP_K * tm), lambda i: (i, 0, 0), memory_space=pltpu.SMEM),
                  pl.BlockSpec((tm * PK_TILES, 128), lambda i: (i, 0))],
        out_specs=pl.BlockSpec(memory_space=pl.ANY),
        out_shape=jax.ShapeDtypeStruct((n_slots * PK_TILES, 128), U32),
        scratch_shapes=[pltpu.SemaphoreType.DMA(())],
        compiler_params=_cparams("arbitrary"),
        name="moe_scatter_rows",
    )(dest3, fpk)


def _ffn(bits, w1, w3, w2):
    half = bits.shape[1]
    lo, hi = _unpack_bf16_pairs(bits)
    h1 = _dot(lo, w1[0:half, :]) + _dot(hi, w1[half:, :])
    h3 = _dot(lo, w3[0:half, :]) + _dot(hi, w3[half:, :])
    return _dot((_silu(h1) * h3).astype(BF16), w2[...])


def _expert_kernel(be_ref, bv_ref, nu_ref, xs_ref, w1_ref, w3_ref, w2_ref, y_ref, w1b, w3b, w2b):
    i = pl.program_id(0)

    @pl.when(i < nu_ref[0])
    def _():
        e = be_ref[i]
        prev = be_ref[jnp.maximum(i - 1, 0)]

        @pl.when((i == 0) | (e != prev))
        def _():
            w1b[...] = w1_ref[0, 0].astype(BF16)
            w3b[...] = w3_ref[0, 0].astype(BF16)
            w2b[...] = w2_ref[0, 0].astype(BF16)

        tm = MOE_BLOCK
        bits = _load_row_tiles(xs_ref, tm, PK_TILES)
        rows = lax.broadcasted_iota(I32, bits.shape, 0)
        bits = jnp.where(rows < bv_ref[i], bits, jnp.uint32(0))
        _store_row_tiles(y_ref, _pack_bf16_pairs(_ffn(bits, w1b, w3b, w2b)))


def _experts(blk_expert, blk_valid, n_used, xs, w1, w3, w2, layer):
    n_slots = xs.shape[0] // PK_TILES
    d, de = w1.shape[2], w1.shape[3]
    assert d == 2 * PK_TILES * 128
    tm = MOE_BLOCK
    grid_spec = pltpu.PrefetchScalarGridSpec(
        num_scalar_prefetch=3,
        grid=(n_slots // tm,),
        in_specs=[pl.BlockSpec((tm * PK_TILES, 128), lambda i, be, bv, nu: (i, 0)),
                  pl.BlockSpec((1, 1, d, de), lambda i, be, bv, nu: (layer, be[i], 0, 0)),
                  pl.BlockSpec((1, 1, d, de), lambda i, be, bv, nu: (layer, be[i], 0, 0)),
                  pl.BlockSpec((1, 1, de, d), lambda i, be, bv, nu: (layer, be[i], 0, 0))],
        out_specs=pl.BlockSpec((tm * PK_TILES, 128), lambda i, be, bv, nu: (i, 0)),
        scratch_shapes=[pltpu.VMEM((d, de), BF16), pltpu.VMEM((d, de), BF16), pltpu.VMEM((de, d), BF16)],
    )
    return pl.pallas_call(
        _expert_kernel,
        grid_spec=grid_spec,
        out_shape=jax.ShapeDtypeStruct((n_slots * PK_TILES, 128), U32),
        compiler_params=_cparams("arbitrary"),
        name="moe_experts",
    )(blk_expert, blk_valid, n_used, xs, w1, w3, w2)


def _shared_kernel(f_ref, w1_ref, w3_ref, w2_ref, y_ref):
    y_ref[...] = _ffn(_load_row_tiles(f_ref, y_ref.shape[0], PK_TILES), w1_ref, w3_ref, w2_ref)


def _shared_expert(fpk, w1, w3, w2):
    t = fpk.shape[0] // PK_TILES
    d, de = w1.shape
    tm = 2 * ROW_TILE
    return pl.pallas_call(
        _shared_kernel,
        grid=(t // tm,),
        in_specs=[pl.BlockSpec((tm * PK_TILES, 128), lambda i: (i, 0)),
                  pl.BlockSpec((d, de), lambda i: (0, 0)),
                  pl.BlockSpec((d, de), lambda i: (0, 0)),
                  pl.BlockSpec((de, d), lambda i: (0, 0))],
        out_specs=pl.BlockSpec((tm, d), lambda i: (i, 0)),
        out_shape=jax.ShapeDtypeStruct((t, d), F32),
        compiler_params=_cparams("arbitrary"),
        name="moe_shared_expert",
    )(fpk, w1, w3, w2)


def _combine_kernel(dest_ref, dest_next_ref, w_ref, x_ref, ysh_ref, mod_ref, y_ref, o_ref, ybuf, lo_ref, hi_ref, sems):
    tm = x_ref.shape[0]
    i = pl.program_id(0)
    n = pl.num_programs(0)
    slot = i % 2

    def copy(idx_ref, s, t, kk):
        src = pl.multiple_of(idx_ref[0, 0, t * TOP_K + kk], PK_TILES)
        dst = pl.multiple_of(t * PK_TILES, PK_TILES)
        return pltpu.make_async_copy(y_ref.at[pl.ds(src, PK_TILES)], ybuf.at[s, kk, pl.ds(dst, PK_TILES)],
                                     sems.at[s])

    def issue_tile(idx_ref, s):
        def issue(t, c):
            for kk in range(TOP_K):
                copy(idx_ref, s, t, kk).start()
            return c

        lax.fori_loop(0, tm, issue, 0)

    @pl.when(i == 0)
    def _():
        issue_tile(dest_ref, 0)

    @pl.when(i + 1 < n)
    def _():
        issue_tile(dest_next_ref, 1 - slot)

    def drain(t, c):
        for kk in range(TOP_K):
            copy(dest_ref, slot, t, kk).wait()
        return c

    lax.fori_loop(0, tm, drain, 0)

    def token(t, c):
        r = pl.multiple_of(t * PK_TILES, PK_TILES)
        a_lo = a_hi = None
        for kk in range(TOP_K):
            w = w_ref[0, 0, t * TOP_K + kk]
            bits = ybuf[slot, kk, pl.ds(r, PK_TILES), :]
            lo = lax.bitcast_convert_type(lax.shift_left(bits, jnp.uint32(16)), F32)
            hi = lax.bitcast_convert_type(bits & jnp.uint32(0xFFFF0000), F32)
            a_lo = w * lo if a_lo is None else a_lo + w * lo
            a_hi = w * hi if a_hi is None else a_hi + w * hi
        lo_ref[pl.ds(r, PK_TILES), :] = a_lo
        hi_ref[pl.ds(r, PK_TILES), :] = a_hi
        return c

    lax.fori_loop(0, tm, token, 0, unroll=4)
    half = PK_TILES * 128
    for c in range(PK_TILES):
        for base, acc_ref in ((0, lo_ref), (half, hi_ref)):
            sl = slice(base + c * 128, base + (c + 1) * 128)
            routed = acc_ref[pl.ds(c, tm, stride=PK_TILES), :]
            o_ref[:, sl] = x_ref[:, sl] + mod_ref[0, 5:6, sl] * (routed + ysh_ref[:, sl])


def _combine(dest3, x, ysh, w3, mod, y, *, tm, n_lat, seq, bsz):
    t, d = x.shape
    n = t // tm
    return pl.pallas_call(
        _combine_kernel,
        grid=(n,),
        in_specs=[pl.BlockSpec((1, 1, TOP_K * tm), lambda i: (i, 0, 0), memory_space=pltpu.SMEM),
                  pl.BlockSpec((1, 1, TOP_K * tm), lambda i: (jnp.minimum(i + 1, n - 1), 0, 0),
                               memory_space=pltpu.SMEM),
                  pl.BlockSpec((1, 1, TOP_K * tm), lambda i: (i, 0, 0), memory_space=pltpu.SMEM),
                  pl.BlockSpec((tm, d), lambda i: (i, 0)),
                  pl.BlockSpec((tm, d), lambda i: (i, 0)),
                  pl.BlockSpec((1, 6, d), lambda i: (_group_of_tile(i, tm, n_lat, seq, bsz), 0, 0)),
                  pl.BlockSpec(memory_space=pl.ANY)],
        out_specs=pl.BlockSpec((tm, d), lambda i: (i, 0)),
        out_shape=jax.ShapeDtypeStruct((t, d), F32),
        scratch_shapes=[pltpu.VMEM((2, TOP_K, tm * PK_TILES, 128), U32),
                        pltpu.VMEM((tm * PK_TILES, 128), F32), pltpu.VMEM((tm * PK_TILES, 128), F32),
                        pltpu.SemaphoreType.DMA((2,))],
        compiler_params=_cparams("arbitrary"),
        name="moe_combine",
    )(dest3, dest3, w3, x, ysh, mod, y)


def _moe(x_new, fpk, logits_t, mod, b_router, w_e1, w_e3, w_e2, w_s1, w_s3, w_s2, *, layer, n_lat, seq, bsz):
    t = x_new.shape[0]
    top_e, top_r, top_w, counts = _route(logits_t, b_router.reshape(N_EXPERTS, 1))
    counts = counts[:, 0].astype(I32)
    blocks = (counts + MOE_BLOCK - 1) // MOE_BLOCK
    blk_end = jnp.cumsum(blocks)
    blk_start = blk_end - blocks
    n_blocks = -(-(t * TOP_K) // MOE_BLOCK) + N_EXPERTS
    n_slots = n_blocks * MOE_BLOCK
    slot_start = blk_start * MOE_BLOCK
    dest = jnp.sum(jnp.where(top_e[None] == jnp.arange(N_EXPERTS, dtype=I32)[:, None, None],
                             slot_start[:, None, None], 0), axis=0) + top_r
    bi = jnp.arange(n_blocks, dtype=I32)
    blk_expert = jnp.minimum(jnp.sum((bi[:, None] >= blk_end[None, :]).astype(I32), axis=1), N_EXPERTS - 1)
    blk_valid = jnp.clip(counts[blk_expert] - (bi - blk_start[blk_expert]) * MOE_BLOCK, 0, MOE_BLOCK)
    n_used = blk_end[-1:].astype(I32)

    dest_tk = dest.T
    tm_s = ROW_TILE
    dest_s = (dest_tk * PK_TILES).reshape(t // tm_s, 1, tm_s * TOP_K)
    xs = _scatter_rows(dest_s, fpk, n_slots)
    y = _experts(blk_expert.astype(I32), blk_valid.astype(I32), n_used, xs, w_e1, w_e3, w_e2, layer)
    ysh = _shared_expert(fpk, w_s1.astype(BF16), w_s3.astype(BF16), w_s2.astype(BF16))
    tm_c = ROW_TILE
    dest_c = (dest_tk * PK_TILES).reshape(t // tm_c, 1, tm_c * TOP_K)
    w_c = top_w.T.reshape(t // tm_c, 1, tm_c * TOP_K)
    return _combine(dest_c, x_new, ysh, w_c, mod, y, tm=tm_c, n_lat=n_lat, seq=seq, bsz=bsz)


def _pad_cols(w, n):
    return jnp.pad(w, ((0, 0), (0, n - w.shape[1])))


def kernel(x, c, ctx, c_ctx, w_ada, b_ada, g_mix, g_ffn, w_in_even, gq_a, gk_a, conv_w, conv_b, a_log, dt_bias,
           d_skip, g_ssm, w_in_odd, gq_c, gk_c, sink_c, gq_d, gk_d, rpb_d, w_out, w_router, b_router, w_e1, w_e3,
           w_e2, w_s1, w_s3, w_s2):
    bsz, seq, d = x.shape
    n_ctx = ctx.shape[1]
    depth = w_ada.shape[0]
    n_lat = bsz * seq
    geo = dict(n_lat=n_lat, seq=seq, n_ctx=n_ctx, bsz=bsz)
    xs = jnp.concatenate([x.reshape(n_lat, d), ctx.reshape(bsz * n_ctx, d)], axis=0)
    cvec = jnp.zeros((8, d), F32).at[:bsz].set(c).at[bsz].set(c_ctx)
    cos, sin = _rope_tables(seq, n_ctx)
    expand = jnp.zeros((128, SSM_INNER), BF16).at[:SSM_HEADS].set(
        jnp.repeat(jnp.eye(SSM_HEADS, dtype=BF16), SSM_HEAD_DIM, axis=1))

    for layer in range(depth):
        last = layer == depth - 1
        i = layer // 2
        mod = _adaln(cvec, w_ada, b_ada[layer].reshape(1, -1), layer)[:bsz + 1].reshape(bsz + 1, 6, d)
        g1 = g_mix[layer].reshape(1, d)
        if layer % 2 == 0:
            n_in = w_in_even.shape[2]
            n_pad = -(-n_in // 384) * 384
            p = _norm_mod_matmul(xs, g1, mod, _pad_cols(w_in_even[i], n_pad).astype(BF16),
                                 n_lat=n_lat, seq=seq, bsz=bsz, tn=n_pad // 3)
            qt, k, vt = _prep_rope(p, cos, sin, gq_a[i].reshape(1, -1), gk_a[i].reshape(1, -1),
                                   q_scale=HEAD_DIM ** -0.5 * LOG2E, n_q=A_HEADS, n_kv=A_KV_HEADS, **geo)
            y1 = _attention_a(qt, k, vt, need_ctx=not last, **geo)
            xbc = _ssm_conv(p, conv_w[i], conv_b[i].reshape(1, -1), col0=A_IN + SSM_INNER,
                            n_lat=n_lat, seq=seq, n_ctx=n_ctx)
            dt_row = jnp.zeros((1, 128), F32).at[0, :2 * SSM_HEADS].set(dt_bias[i].reshape(-1))
            dt2 = _ssm_dt(p, dt_row, col0=A_IN + SSM_INNER + SSM_CONV_DIM)
            a_rows = jnp.zeros((2, 1, 128), F32).at[:, 0, :SSM_HEADS].set(-jnp.exp(a_log[i]))
            ydir = _ssd_scan(xbc, dt2, a_rows, expand, **geo)
            y2 = _ssd_finish(ydir, xbc, p, jnp.repeat(d_skip[i], SSM_HEAD_DIM).reshape(1, -1),
                             g_ssm[i].reshape(1, -1), z_col0=A_IN)
        else:
            p = _norm_mod_matmul(xs, g1, mod, w_in_odd[i].astype(BF16), n_lat=n_lat, seq=seq, bsz=bsz,
                                 tn=w_in_odd.shape[2] // 3)
            qt, k, vt = _prep_rope(p, cos, sin, gq_c[i].reshape(1, -1), gk_c[i].reshape(1, -1),
                                   q_scale=HEAD_DIM ** -0.5, n_q=A_HEADS, n_kv=A_KV_HEADS, **geo)
            group = A_HEADS // A_KV_HEADS
            sink_rows = jnp.repeat(sink_c[i].reshape(A_KV_HEADS, group), Q_TILE, axis=1).reshape(
                A_KV_HEADS, 1, group * Q_TILE)
            y1 = _attention_c(qt, k, vt, sink_rows, **geo)
            qd, kd, vd = _prep_na(p, gq_d[i].reshape(1, -1), gk_d[i].reshape(1, -1), col0=A_IN, **geo)
            y2 = _attention_d(qd, kd, vd, _na_bias_tables(rpb_d[i], seq // GRID_W), **geo)
        n_rows = n_lat if last else xs.shape[0]
        x_new, fpk, logits_t = _outproj(xs, y1, y2, w_out[layer].astype(BF16), mod, g_ffn[layer].reshape(1, d),
                                        w_router[layer].T, n_rows=n_rows, n_lat=n_lat, seq=seq, bsz=bsz)
        xs = _moe(x_new, fpk, logits_t, mod, b_router[layer], w_e1, w_e3, w_e2,
                  w_s1[layer], w_s3[layer], w_s2[layer], layer=layer, n_lat=n_lat, seq=seq, bsz=bsz)
    return xs[:n_lat].reshape(bsz, seq, d)
```

```python
import functools
import math

import jax
import jax.numpy as jnp
from jax import lax
from jax.experimental import pallas as pl
from jax.experimental.pallas import tpu as pltpu

F32 = jnp.float32
BF16 = jnp.bfloat16
I32 = jnp.int32
U32 = jnp.uint32

EPS = 1e-6
HEAD_DIM = 128
GRID_W = 64
ROPE_THETA = 10000.0
WINDOW = 128
NA_ROWS = 8
NA_COLS = 16
NA_TILE_ROWS = 8
NA_WIN_ROWS = 16
A_HEADS = 8
A_KV_HEADS = 2
SSM_HEADS = 16
SSM_HEAD_DIM = 64
SSM_INNER = SSM_HEADS * SSM_HEAD_DIM
SSM_GROUPS = 2
SSM_STATE = 128
SSM_CONV = 5
SSM_CONV_DIM = SSM_INNER + 2 * SSM_GROUPS * SSM_STATE
SSM_CHUNK = 128
A_IN = (A_HEADS + 2 * A_KV_HEADS) * HEAD_DIM
NA_HEADS = 8
N_EXPERTS = 64
TOP_K = 8
N_GROUPS = 8
TOPK_GROUPS = 4
ROUTE_SCALE = 2.5
MOE_BLOCK = 512

VMEM_LIMIT_BYTES = 56 * 1024 * 1024
ROW_TILE = 256
Q_TILE = 256
VT_ROWS = HEAD_DIM + 16
LOG2E = 1.4426950408889634
HI = lax.Precision.HIGHEST


def _cparams(*sem):
    return pltpu.CompilerParams(dimension_semantics=sem, vmem_limit_bytes=VMEM_LIMIT_BYTES)


def _silu(x):
    return x * jax.nn.sigmoid(x)


def _hi_dot(a, b):
    return jnp.dot(a, b, precision=HI, preferred_element_type=F32)


def _dot(a, b):
    return jnp.dot(a, b, preferred_element_type=F32)


def _dot_nt(a, b):
    return lax.dot_general(a, b, (((1,), (1,)), ((), ())), preferred_element_type=F32)


def _largest_tile(n, cap, mult=128):
    best = None
    for t in range(mult, cap + 1, mult):
        if n % t == 0:
            best = t
    assert best is not None, (n, cap)
    return best


def _adaln_kernel(c_ref, w_ref, b_ref, o_ref):
    o_ref[...] = _hi_dot(_silu(c_ref[...]), w_ref[0]) + b_ref[...]


def _adaln(cvec, w, b, layer):
    _, d, n = w.shape
    tn = 1024
    return pl.pallas_call(
        _adaln_kernel,
        grid=(n // tn,),
        in_specs=[pl.BlockSpec((8, d), lambda j: (0, 0)),
                  pl.BlockSpec((1, d, tn), lambda j: (layer, 0, j)),
                  pl.BlockSpec((1, tn), lambda j: (0, j))],
        out_specs=pl.BlockSpec((8, tn), lambda j: (0, j)),
        out_shape=jax.ShapeDtypeStruct((8, n), F32),
        compiler_params=_cparams("arbitrary"),
        name="adaln",
    )(cvec, w, b)


def _norm_mod(x, g, shift, scale):
    ms = jnp.mean(x * x, axis=-1, keepdims=True)
    return (x * lax.rsqrt(ms + EPS) * g) * (1.0 + scale) + shift


def _nmm_kernel(x_ref, g_ref, mod_ref, w_ref, o_ref, xn_ref):
    @pl.when(pl.program_id(1) == 0)
    def _():
        xn_ref[...] = _norm_mod(x_ref[...], g_ref[...], mod_ref[0, 0:1, :], mod_ref[0, 1:2, :]).astype(BF16)

    o_ref[...] = _dot(xn_ref[...], w_ref[...])


def _group_of_tile(i, tm, n_lat, seq, bsz):
    return jnp.where(i * tm < n_lat, (i * tm) // seq, bsz)


def _norm_mod_matmul(x, g, mod, w_bf16, *, n_lat, seq, bsz, tn):
    t, d = x.shape
    n = w_bf16.shape[1]
    tm = 2 * ROW_TILE
    assert t % tm == 0 and n % tn == 0 and n_lat % tm == 0
    return pl.pallas_call(
        _nmm_kernel,
        grid=(t // tm, n // tn),
        in_specs=[pl.BlockSpec((tm, d), lambda i, j: (i, 0)),
                  pl.BlockSpec((1, d), lambda i, j: (0, 0)),
                  pl.BlockSpec((1, 6, d), lambda i, j: (_group_of_tile(i, tm, n_lat, seq, bsz), 0, 0)),
                  pl.BlockSpec((d, tn), lambda i, j: (0, j))],
        out_specs=pl.BlockSpec((tm, tn), lambda i, j: (i, j)),
        out_shape=jax.ShapeDtypeStruct((t, n), F32),
        scratch_shapes=[pltpu.VMEM((tm, d), BF16)],
        compiler_params=_cparams("arbitrary", "arbitrary"),
        name="norm_mod_inproj",
    )(x, g, mod, w_bf16)


def _head_norm(x, g):
    ms = jnp.mean(x * x, axis=-1, keepdims=True)
    return x * lax.rsqrt(ms + EPS) * g


def _prep_rope_kernel(p_ref, cos_ref, sin_ref, gq_ref, gk_ref, qt_ref, k_ref, vt_ref, *, n_q, n_kv, tq, q_scale):
    cos = cos_ref[...]
    sin = sin_ref[...]
    lane = lax.broadcasted_iota(I32, cos.shape, 1)
    first_half = (lane % 64) < 32

    def norm_rope(x, g):
        y = _head_norm(x, g)
        swapped = jnp.where(first_half, pltpu.roll(y, 96, 1), pltpu.roll(y, 32, 1))
        return y * cos + swapped * sin

    group = n_q // n_kv
    for h in range(n_q):
        q = norm_rope(p_ref[:, h * HEAD_DIM:(h + 1) * HEAD_DIM], gq_ref[...]) * q_scale
        kvh, hh = divmod(h, group)
        qt_ref[0, kvh, 0, :, hh * tq:(hh + 1) * tq] = q.T.astype(BF16)
    for h in range(n_kv):
        c0 = (n_q + h) * HEAD_DIM
        k_ref[0, h] = norm_rope(p_ref[:, c0:c0 + HEAD_DIM], gk_ref[...]).astype(BF16)
        c1 = (n_q + n_kv + h) * HEAD_DIM
        vt_ref[0, h, 0:HEAD_DIM, :] = p_ref[:, c1:c1 + HEAD_DIM].T.astype(BF16)
        vt_ref[0, h, HEAD_DIM:VT_ROWS, :] = jnp.ones((VT_ROWS - HEAD_DIM, tq), BF16)


def _pos_maps(tm, n_lat, seq, n_ctx):
    lat_tiles, seq_tiles, ctx_tiles = n_lat // tm, seq // tm, n_ctx // tm

    def batch_of(i):
        return jnp.where(i < lat_tiles, i // seq_tiles, (i - lat_tiles) // ctx_tiles)

    def pos_of(i):
        return jnp.where(i < lat_tiles, ctx_tiles + i % seq_tiles, (i - lat_tiles) % ctx_tiles)

    return batch_of, pos_of


def _prep_rope(p, cos, sin, gq, gk, *, q_scale, n_q, n_kv, n_lat, seq, n_ctx, bsz):
    t = p.shape[0]
    tm = Q_TILE
    pos_len = n_ctx + seq
    group = n_q // n_kv
    width = (n_q + 2 * n_kv) * HEAD_DIM
    batch_of, pos_of = _pos_maps(tm, n_lat, seq, n_ctx)
    kern = functools.partial(_prep_rope_kernel, n_q=n_q, n_kv=n_kv, tq=tm, q_scale=q_scale)
    return pl.pallas_call(
        kern,
        grid=(t // tm,),
        in_specs=[pl.BlockSpec((tm, width), lambda i: (i, 0)),
                  pl.BlockSpec((tm, HEAD_DIM), lambda i: (pos_of(i), 0)),
                  pl.BlockSpec((tm, HEAD_DIM), lambda i: (pos_of(i), 0)),
                  pl.BlockSpec((1, HEAD_DIM), lambda i: (0, 0)),
                  pl.BlockSpec((1, HEAD_DIM), lambda i: (0, 0))],
        out_specs=[pl.BlockSpec((1, n_kv, 1, HEAD_DIM, group * tm), lambda i: (batch_of(i), 0, pos_of(i), 0, 0)),
                   pl.BlockSpec((1, n_kv, tm, HEAD_DIM), lambda i: (batch_of(i), 0, pos_of(i), 0)),
                   pl.BlockSpec((1, n_kv, VT_ROWS, tm), lambda i: (batch_of(i), 0, 0, pos_of(i)))],
        out_shape=[jax.ShapeDtypeStruct((bsz, n_kv, pos_len // tm, HEAD_DIM, group * tm), BF16),
                   jax.ShapeDtypeStruct((bsz, n_kv, pos_len, HEAD_DIM), BF16),
                   jax.ShapeDtypeStruct((bsz, n_kv, VT_ROWS, pos_len), BF16)],
        compiler_params=_cparams("arbitrary"),
        name="prep_rope",
    )(p, cos, sin, gq, gk)


def _rope_tables(seq, n_ctx):
    t = jnp.arange(seq, dtype=I32)
    row = (t // GRID_W).astype(F32)
    col = (t % GRID_W).astype(F32)
    n_freq = HEAD_DIM // 4
    inv = ROPE_THETA ** (-jnp.arange(n_freq, dtype=F32) / n_freq)
    ar, ac = row[:, None] * inv, col[:, None] * inv
    cos = jnp.concatenate([jnp.cos(ar), jnp.cos(ar), jnp.cos(ac), jnp.cos(ac)], axis=1)
    sin = jnp.concatenate([-jnp.sin(ar), jnp.sin(ar), -jnp.sin(ac), jnp.sin(ac)], axis=1)
    cos = jnp.concatenate([jnp.ones((n_ctx, HEAD_DIM), F32), cos], axis=0)
    sin = jnp.concatenate([jnp.zeros((n_ctx, HEAD_DIM), F32), sin], axis=0)
    return cos, sin


def _flash_t_kernel(qt_ref, k_ref, vt_ref, o_ref, m_ref, l_ref, acc_ref, sa_ref, sb_ref, *, tk, n_k, tq, group):
    qt = qt_ref[0, 0, 0]
    m_ref[...] = jnp.full(m_ref.shape, -jnp.inf, F32)
    l_ref[...] = jnp.zeros(l_ref.shape, F32)
    acc_ref[...] = jnp.zeros(acc_ref.shape, F32)

    def scores(j, dst):
        off = pl.multiple_of(jnp.minimum(j, n_k - 1) * tk, tk)
        dst[...] = _dot(k_ref[0, 0, pl.ds(off, tk), :], qt)

    def consume(j, src):
        off = pl.multiple_of(j * tk, tk)
        s = src[...]
        m_old = m_ref[...]
        m_new = jnp.maximum(m_old, jnp.max(s, axis=0, keepdims=True))
        alpha = jnp.exp2(m_old - m_new)
        p = jnp.exp2(s - m_new).astype(BF16)
        r = _dot(vt_ref[0, 0, :, pl.ds(off, tk)], p)
        acc_ref[...] = alpha * acc_ref[...] + r[0:HEAD_DIM]
        l_ref[...] = alpha * l_ref[...] + r[HEAD_DIM:HEAD_DIM + 1]
        m_ref[...] = m_new

    scores(0, sa_ref)

    def body(jj, carry):
        j = 2 * jj
        scores(j + 1, sb_ref)
        consume(j, sa_ref)
        scores(j + 2, sa_ref)
        consume(j + 1, sb_ref)
        return carry

    lax.fori_loop(0, n_k // 2, body, 0)
    if n_k % 2:
        consume(n_k - 1, sa_ref)
    out = acc_ref[...] / l_ref[...]
    for h in range(group):
        o_ref[:, h * HEAD_DIM:(h + 1) * HEAD_DIM] = out[:, h * tq:(h + 1) * tq].T.astype(o_ref.dtype)


def _attention_a(qt, k, vt, *, n_lat, seq, n_ctx, bsz, need_ctx):
    n_kv, group, tq = k.shape[1], qt.shape[-1] // Q_TILE, Q_TILE
    pos_len = n_ctx + seq
    width = n_kv * group * HEAD_DIM
    tk = _largest_tile(pos_len, 1280)

    def scratch(tkk):
        n = group * tq
        return [pltpu.VMEM((1, n), F32), pltpu.VMEM((1, n), F32), pltpu.VMEM((HEAD_DIM, n), F32),
                pltpu.VMEM((tkk, n), F32), pltpu.VMEM((tkk, n), F32)]

    y = pl.pallas_call(
        functools.partial(_flash_t_kernel, tk=tk, n_k=pos_len // tk, tq=tq, group=group),
        grid=(bsz, n_kv, seq // tq),
        in_specs=[pl.BlockSpec((1, 1, 1, HEAD_DIM, group * tq), lambda b, h, i: (b, h, n_ctx // tq + i, 0, 0)),
                  pl.BlockSpec((1, 1, pos_len, HEAD_DIM), lambda b, h, i: (b, h, 0, 0)),
                  pl.BlockSpec((1, 1, VT_ROWS, pos_len), lambda b, h, i: (b, h, 0, 0))],
        out_specs=pl.BlockSpec((tq, group * HEAD_DIM), lambda b, h, i: (b * (seq // tq) + i, h)),
        out_shape=jax.ShapeDtypeStruct((n_lat, width), BF16),
        scratch_shapes=scratch(tk),
        compiler_params=_cparams("arbitrary", "arbitrary", "arbitrary"),
        name="attn_a_latent",
    )(qt, k, vt)
    if not need_ctx:
        return y
    tkc = _largest_tile(n_ctx, 768)
    y_ctx = pl.pallas_call(
        functools.partial(_flash_t_kernel, tk=tkc, n_k=n_ctx // tkc, tq=tq, group=group),
        grid=(bsz, n_kv, n_ctx // tq),
        in_specs=[pl.BlockSpec((1, 1, 1, HEAD_DIM, group * tq), lambda b, h, i: (b, h, i, 0, 0)),
                  pl.BlockSpec((1, 1, n_ctx, HEAD_DIM), lambda b, h, i: (b, h, 0, 0)),
                  pl.BlockSpec((1, 1, VT_ROWS, n_ctx), lambda b, h, i: (b, h, 0, 0))],
        out_specs=pl.BlockSpec((tq, group * HEAD_DIM), lambda b, h, i: (b * (n_ctx // tq) + i, h)),
        out_shape=jax.ShapeDtypeStruct((bsz * n_ctx, width), BF16),
        scratch_shapes=scratch(tkc),
        compiler_params=_cparams("arbitrary", "arbitrary", "arbitrary"),
        name="attn_a_context",
    )(qt, k, vt)
    return jnp.concatenate([y, y_ctx], axis=0)


def _segment_edges(i, tm, n_lat, seq, n_ctx):
    lat_tiles, seq_tiles, ctx_tiles = n_lat // tm, seq // tm, n_ctx // tm
    is_lat = i < lat_tiles
    pos = jnp.where(is_lat, i % seq_tiles, (i - lat_tiles) % ctx_tiles)
    last = jnp.where(is_lat, seq_tiles, ctx_tiles) - 1
    return pos == 0, pos == last


def _conv_kernel(cur_ref, prev_ref, next_ref, w_ref, b_ref, o_ref, ext_ref, *, tm, n_lat, seq, n_ctx):
    first, last = _segment_edges(pl.program_id(0), tm, n_lat, seq, n_ctx)
    ext_ref[0:8, :] = jnp.where(first, 0.0, prev_ref[...])
    ext_ref[8:8 + tm, :] = cur_ref[...]
    ext_ref[8 + tm:16 + tm, :] = jnp.where(last, 0.0, next_ref[...])
    acc = jnp.broadcast_to(b_ref[...], (tm, b_ref.shape[1]))
    half = SSM_CONV // 2
    for kk in range(SSM_CONV):
        acc = acc + w_ref[kk:kk + 1, :] * ext_ref[pl.ds(8 - half + kk, tm), :]
    o_ref[...] = _silu(acc)


def _ssm_conv(p, conv_w, conv_b, *, col0, n_lat, seq, n_ctx):
    t = p.shape[0]
    tm, tc = ROW_TILE, 512
    assert col0 % tc == 0 and SSM_CONV_DIM % tc == 0
    cb0 = col0 // tc
    r8 = tm // 8
    kern = functools.partial(_conv_kernel, tm=tm, n_lat=n_lat, seq=seq, n_ctx=n_ctx)
    return pl.pallas_call(
        kern,
        grid=(t // tm, SSM_CONV_DIM // tc),
        in_specs=[pl.BlockSpec((tm, tc), lambda i, j: (i, cb0 + j)),
                  pl.BlockSpec((8, tc), lambda i, j: (jnp.maximum(i * r8 - 1, 0), cb0 + j)),
                  pl.BlockSpec((8, tc), lambda i, j: (jnp.minimum((i + 1) * r8, t // 8 - 1), cb0 + j)),
                  pl.BlockSpec((SSM_CONV, tc), lambda i, j: (0, j)),
                  pl.BlockSpec((1, tc), lambda i, j: (0, j))],
        out_specs=pl.BlockSpec((tm, tc), lambda i, j: (i, j)),
        out_shape=jax.ShapeDtypeStruct((t, SSM_CONV_DIM), F32),
        scratch_shapes=[pltpu.VMEM((tm + 16, tc), F32)],
        compiler_params=_cparams("arbitrary", "arbitrary"),
        name="ssm_conv",
    )(p, p, p, conv_w, conv_b)


def _dt_kernel(p_ref, b_ref, o_ref):
    x = p_ref[...] + b_ref[...]
    sp = jnp.maximum(x, 0.0) + jnp.log1p(jnp.exp(-jnp.abs(x)))
    o_ref[0] = sp
    o_ref[1] = pltpu.roll(sp, HEAD_DIM - SSM_HEADS, 1)


def _ssm_dt(p, bias_row, *, col0):
    t = p.shape[0]
    tm = 2 * ROW_TILE
    assert col0 % 128 == 0
    return pl.pallas_call(
        _dt_kernel,
        grid=(t // tm,),
        in_specs=[pl.BlockSpec((tm, 128), lambda i: (i, col0 // 128)),
                  pl.BlockSpec((1, 128), lambda i: (0, 0))],
        out_specs=pl.BlockSpec((2, tm, 128), lambda i: (0, i, 0)),
        out_shape=jax.ShapeDtypeStruct((2, t, 128), F32),
        compiler_params=_cparams("arbitrary"),
        name="ssm_dt",
    )(p, bias_row)


def _ssd_kernel(x_ref, bc_ref, dt_ref, a_ref, e_ref, y_ref, h_ref):
    d = pl.program_id(1)
    lc = SSM_CHUNK
    gw = SSM_INNER // SSM_GROUPS
    hpg = SSM_HEADS // SSM_GROUPS

    @pl.when(pl.program_id(2) == 0)
    def _():
        h_ref[...] = jnp.zeros(h_ref.shape, F32)

    dt = dt_ref[0]
    w = dt * a_ref[0]
    row = lax.broadcasted_iota(I32, (lc, lc), 0)
    col = lax.broadcasted_iota(I32, (lc, lc), 1)
    fwd = d == 0
    sgn = jnp.where(fwd, 1, -1)
    allowed = (row - col) * sgn >= 0
    allowed_t = (col - row) * sgn >= 0
    cs = _hi_dot(allowed.astype(F32), w)
    cs_t = _hi_dot(w.T, allowed_t.astype(F32))
    dt_t = dt.T
    tot = jnp.where(fwd, cs[lc - 1:lc, :], cs[0:1, :])
    e = e_ref[...]

    def expand(a):
        hi = a.astype(BF16)
        lo = (a - hi.astype(F32)).astype(BF16)
        return _dot(hi, e) + _dot(lo, e)

    x = x_ref[...]
    xw = (x * expand(jnp.exp(tot - cs) * dt)).astype(BF16)
    off_scale = expand(jnp.exp(cs))
    state_decay = expand(jnp.exp(jnp.broadcast_to(tot, (lc, 128))))
    xb = x.astype(BF16)
    for g in range(SSM_GROUPS):
        bm = bc_ref[:, g * SSM_STATE:(g + 1) * SSM_STATE]
        cm = bc_ref[:, (SSM_GROUPS + g) * SSM_STATE:(SSM_GROUPS + g + 1) * SSM_STATE].astype(BF16)
        cb = _dot_nt(cm, bm.astype(BF16))
        h_prev = h_ref[g]
        y_off = _dot(cm, h_prev.astype(BF16))
        s_chunk = _dot(bm.T.astype(BF16), xw[:, g * gw:(g + 1) * gw])
        h_ref[g] = state_decay[:, g * gw:(g + 1) * gw] * h_prev + s_chunk
        pieces = []
        for r in range(hpg):
            hh = g * hpg + r
            seg = cs[:, hh:hh + 1] - cs_t[hh:hh + 1, :]
            dec = jnp.exp(jnp.where(allowed, seg, -jnp.inf))
            mix = (cb * dec * dt_t[hh:hh + 1, :]).astype(BF16)
            pieces.append(_dot(mix, xb[:, hh * SSM_HEAD_DIM:(hh + 1) * SSM_HEAD_DIM]))
        y_ref[0, :, g * gw:(g + 1) * gw] = (jnp.concatenate(pieces, axis=1)
                                           + y_off * off_scale[:, g * gw:(g + 1) * gw])


def _ssd_scan(xbc, dt2, a_rows, expand, *, n_lat, seq, n_ctx, bsz):
    t = xbc.shape[0]
    lc = SSM_CHUNK
    ctx_chunks, lat_chunks = n_ctx // lc, seq // lc
    gw = SSM_INNER // SSM_GROUPS

    def rowblk(b, d, s):
        in_ctx = s < ctx_chunks
        j_ctx = jnp.where(d == 0, s, ctx_chunks - 1 - s)
        sl = s - ctx_chunks
        j_lat = jnp.where(d == 0, sl, lat_chunks - 1 - sl)
        return jnp.where(in_ctx, (n_lat + b * n_ctx) // lc + j_ctx, b * lat_chunks + j_lat)

    return pl.pallas_call(
        _ssd_kernel,
        grid=(bsz, 2, ctx_chunks + lat_chunks),
        in_specs=[pl.BlockSpec((lc, SSM_INNER), lambda b, d, s: (rowblk(b, d, s), 0)),
                  pl.BlockSpec((lc, 2 * SSM_GROUPS * SSM_STATE),
                               lambda b, d, s: (rowblk(b, d, s), SSM_INNER // (2 * SSM_GROUPS * SSM_STATE))),
                  pl.BlockSpec((1, lc, 128), lambda b, d, s: (d, rowblk(b, d, s), 0)),
                  pl.BlockSpec((1, 1, 128), lambda b, d, s: (d, 0, 0)),
                  pl.BlockSpec((128, SSM_INNER), lambda b, d, s: (0, 0))],
        out_specs=pl.BlockSpec((1, lc, SSM_INNER), lambda b, d, s: (d, rowblk(b, d, s), 0)),
        out_shape=jax.ShapeDtypeStruct((2, t, SSM_INNER), F32),
        scratch_shapes=[pltpu.VMEM((SSM_GROUPS, SSM_STATE, gw), F32)],
        compiler_params=_cparams("arbitrary", "arbitrary", "arbitrary"),
        name="ssd_scan",
    )(xbc, xbc, dt2, a_rows, expand)


def _ssd_finish_kernel(yf_ref, yb_ref, xs_ref, z0_ref, z1_ref, dsk_ref, g_ref, o_ref):
    gw = SSM_INNER // SSM_GROUPS
    zs = (z0_ref, z1_ref)
    for g in range(SSM_GROUPS):
        sl = slice(g * gw, (g + 1) * gw)
        y = yf_ref[0, :, sl] + yb_ref[0, :, sl] + dsk_ref[:, sl] * xs_ref[:, sl]
        y = y * _silu(zs[g][...])
        o_ref[:, sl] = _head_norm(y, g_ref[:, sl]).astype(o_ref.dtype)


def _ssd_finish(ydir, xbc, p, dsk_row, g_row, *, z_col0):
    t = xbc.shape[0]
    tm = ROW_TILE
    gw = SSM_INNER // SSM_GROUPS
    assert z_col0 % gw == 0 and SSM_GROUPS == 2
    zb = z_col0 // gw
    return pl.pallas_call(
        _ssd_finish_kernel,
        grid=(t // tm,),
        in_specs=[pl.BlockSpec((1, tm, SSM_INNER), lambda i: (0, i, 0)),
                  pl.BlockSpec((1, tm, SSM_INNER), lambda i: (1, i, 0)),
                  pl.BlockSpec((tm, SSM_INNER), lambda i: (i, 0)),
                  pl.BlockSpec((tm, gw), lambda i: (i, zb)),
                  pl.BlockSpec((tm, gw), lambda i: (i, zb + 1)),
                  pl.BlockSpec((1, SSM_INNER), lambda i: (0, 0)),
                  pl.BlockSpec((1, SSM_INNER), lambda i: (0, 0))],
        out_specs=pl.BlockSpec((tm, SSM_INNER), lambda i: (i, 0)),
        out_shape=jax.ShapeDtypeStruct((t, SSM_INNER), BF16),
        compiler_params=_cparams("arbitrary"),
        name="ssd_finish",
    )(ydir, ydir, xbc, p, p, dsk_row, g_row)


def _window_kernel(qt_ref, k_ref, vt_ref, sink_ref, o_ref, *, tq, group, n_ctx, seq):
    band = tq + 2 * WINDOW
    pos_len = n_ctx + seq
    q0 = pl.program_id(2) * tq
    start = pl.multiple_of(jnp.minimum(n_ctx + q0 - WINDOW, pos_len - band), 128)
    qt = qt_ref[0, 0, 0]
    n = group * tq
    s_cx = _dot(k_ref[0, 0, 0:n_ctx, :], qt)
    s_w = _dot(k_ref[0, 0, pl.ds(start, band), :], qt)
    kpos = start - n_ctx + lax.broadcasted_iota(I32, (band, n), 0)
    qpos = q0 + lax.broadcasted_iota(I32, (band, n), 1) % tq
    valid = (jnp.abs(qpos - kpos) <= WINDOW) & (kpos >= 0)
    s_w = jnp.where(valid, s_w, -jnp.inf)
    sink = sink_ref[0]
    m = jnp.maximum(jnp.maximum(jnp.max(s_cx, axis=0, keepdims=True), jnp.max(s_w, axis=0, keepdims=True)), sink)
    p_cx = jnp.exp(s_cx - m)
    p_w = jnp.exp(s_w - m)
    l = jnp.sum(p_cx, axis=0, keepdims=True) + jnp.sum(p_w, axis=0, keepdims=True) + jnp.exp(sink - m)
    acc = (_dot(vt_ref[0, 0, 0:HEAD_DIM, 0:n_ctx], p_cx.astype(BF16))
           + _dot(vt_ref[0, 0, 0:HEAD_DIM, pl.ds(start, band)], p_w.astype(BF16)))
    out = acc / l
    for h in range(group):
        o_ref[:, h * HEAD_DIM:(h + 1) * HEAD_DIM] = out[:, h * tq:(h + 1) * tq].T.astype(o_ref.dtype)


def _attention_c(qt, k, vt, sink_rows, *, n_lat, seq, n_ctx, bsz):
    n_kv, group, tq = k.shape[1], qt.shape[-1] // Q_TILE, Q_TILE
    pos_len = n_ctx + seq
    assert n_ctx >= WINDOW and n_ctx % 128 == 0 and pos_len >= tq + 2 * WINDOW
    return pl.pallas_call(
        functools.partial(_window_kernel, tq=tq, group=group, n_ctx=n_ctx, seq=seq),
        grid=(bsz, n_kv, seq // tq),
        in_specs=[pl.BlockSpec((1, 1, 1, HEAD_DIM, group * tq), lambda b, h, i: (b, h, n_ctx // tq + i, 0, 0)),
                  pl.BlockSpec((1, 1, pos_len, HEAD_DIM), lambda b, h, i: (b, h, 0, 0)),
                  pl.BlockSpec((1, 1, VT_ROWS, pos_len), lambda b, h, i: (b, h, 0, 0)),
                  pl.BlockSpec((1, 1, group * tq), lambda b, h, i: (h, 0, 0))],
        out_specs=pl.BlockSpec((tq, group * HEAD_DIM), lambda b, h, i: (b * (seq // tq) + i, h)),
        out_shape=jax.ShapeDtypeStruct((n_lat, n_kv * group * HEAD_DIM), BF16),
        compiler_params=_cparams("arbitrary", "arbitrary", "arbitrary"),
        name="attn_c_window",
    )(qt, k, vt, sink_rows)


def _prep_na(p, gq, gk, *, col0, n_lat, seq, n_ctx, bsz):
    t = p.shape[0]
    tm = ROW_TILE
    hw = NA_HEADS * HEAD_DIM
    assert col0 % hw == 0 or (2 * col0) % hw == 0
    pos_len = n_ctx + seq
    batch_of, pos_of = _pos_maps(tm, n_lat, seq, n_ctx)
    lat_tiles, seq_tiles, ctx_tiles = n_lat // tm, seq // tm, n_ctx // tm

    def qpos_of(i):
        return jnp.where(i < lat_tiles, i % seq_tiles, seq_tiles + (i - lat_tiles) % ctx_tiles)

    cw = hw // 2
    cb = col0 // cw
    shp = jax.ShapeDtypeStruct((bsz, NA_HEADS, pos_len, HEAD_DIM), BF16)
    kv_spec = pl.BlockSpec((1, NA_HEADS, tm, HEAD_DIM), lambda i: (batch_of(i), 0, pos_of(i), 0))
    q_spec = pl.BlockSpec((1, NA_HEADS, tm, HEAD_DIM), lambda i: (batch_of(i), 0, qpos_of(i), 0))

    def kern(q0, q1, k0, k1, v0, v1, gq_ref, gk_ref, q_ref, k_ref, v_ref):
        halves = NA_HEADS // 2
        for h in range(NA_HEADS):
            sl = slice((h % halves) * HEAD_DIM, (h % halves + 1) * HEAD_DIM)
            pq, pk, pv = ((q0, k0, v0) if h < halves else (q1, k1, v1))
            q_ref[0, h] = (_head_norm(pq[:, sl], gq_ref[...]) * (HEAD_DIM ** -0.5)).astype(BF16)
            k_ref[0, h] = _head_norm(pk[:, sl], gk_ref[...]).astype(BF16)
            v_ref[0, h] = pv[:, sl].astype(BF16)

    def col_spec(j):
        return pl.BlockSpec((tm, cw), lambda i: (i, cb + j))

    return pl.pallas_call(
        kern,
        grid=(t // tm,),
        in_specs=[col_spec(j) for j in range(6)] + [pl.BlockSpec((1, HEAD_DIM), lambda i: (0, 0)),
                                                    pl.BlockSpec((1, HEAD_DIM), lambda i: (0, 0))],
        out_specs=[q_spec, kv_spec, kv_spec],
        out_shape=[shp, shp, shp],
        compiler_params=_cparams("arbitrary"),
        name="prep_na",
    )(p, p, p, p, p, p, gq, gk)


def _na_kernel(q_ref, k_ref, v_ref, bias_ref, o_ref, *, n_ctx, rows):
    tq = NA_TILE_ROWS * GRID_W
    win = NA_WIN_ROWS * GRID_W
    ti = pl.program_id(2)
    w0 = jnp.clip(ti * NA_TILE_ROWS - NA_ROWS // 2, 0, rows - NA_WIN_ROWS)
    start = pl.multiple_of(n_ctx + w0 * GRID_W, GRID_W)
    n_tiles = rows // NA_TILE_ROWS
    gq = tq // 2
    sub = (NA_TILE_ROWS // 2 + NA_ROWS) * GRID_W
    shift = (NA_WIN_ROWS * GRID_W - sub)
    for g in range(2):
        off = jnp.where(ti == 0, 0, jnp.where(ti == n_tiles - 1, shift, g * shift))
        off = pl.multiple_of(off, 128)
        ks = pl.multiple_of(start + off, GRID_W)
        q = q_ref[0, 0, g * gq:(g + 1) * gq, :]
        s_cx = _dot_nt(q, k_ref[0, 0, 0:n_ctx, :])
        s_nb = _dot_nt(q, k_ref[0, 0, pl.ds(ks, sub), :]) + bias_ref[0, 0, g * gq:(g + 1) * gq, pl.ds(off, sub)]
        m = jnp.maximum(jnp.max(s_cx, axis=1, keepdims=True), jnp.max(s_nb, axis=1, keepdims=True))
        p_cx = jnp.exp(s_cx - m)
        p_nb = jnp.exp(s_nb - m)
        l = jnp.sum(p_cx, axis=1, keepdims=True) + jnp.sum(p_nb, axis=1, keepdims=True)
        acc = (_dot(p_cx.astype(BF16), v_ref[0, 0, 0:n_ctx, :])
               + _dot(p_nb.astype(BF16), v_ref[0, 0, pl.ds(ks, sub), :]))
        o_ref[g * gq:(g + 1) * gq, :] = (acc / l).astype(o_ref.dtype)


def _na_bias_tables(rpb, rows):
    n_tiles = rows // NA_TILE_ROWS
    tables = []
    for ti in (0, 1, n_tiles - 1):
        w0 = min(max(ti * NA_TILE_ROWS - NA_ROWS // 2, 0), rows - NA_WIN_ROWS)
        qr = ti * NA_TILE_ROWS + jnp.arange(NA_TILE_ROWS)
        r0 = jnp.clip(qr - NA_ROWS // 2, 0, rows - NA_ROWS)
        kr = w0 + jnp.arange(NA_WIN_ROWS)
        row_ok = (kr[None, :] >= r0[:, None]) & (kr[None, :] < r0[:, None] + NA_ROWS)
        dr = jnp.clip(kr[None, :] - qr[:, None] + NA_ROWS - 1, 0, 2 * NA_ROWS - 2)
        qc = jnp.arange(GRID_W)
        c0 = jnp.clip(qc - NA_COLS // 2, 0, GRID_W - NA_COLS)
        kc = jnp.arange(GRID_W)
        col_ok = (kc[None, :] >= c0[:, None]) & (kc[None, :] < c0[:, None] + NA_COLS)
        dc = jnp.clip(kc[None, :] - qc[:, None] + NA_COLS - 1, 0, 2 * NA_COLS - 2)
        sel_r = ((dr[:, :, None] == jnp.arange(2 * NA_ROWS - 1)) & row_ok[:, :, None]).astype(F32)
        sel_c = ((dc[:, :, None] == jnp.arange(2 * NA_COLS - 1)) & col_ok[:, :, None]).astype(F32)
        b = jnp.einsum("rki,hij,cqj->hrckq", sel_r, rpb.astype(F32), sel_c, precision=HI)
        ok = row_ok[:, None, :, None] & col_ok[None, :, None, :]
        b = jnp.where(ok[None], b, -jnp.inf)
        tables.append(b.reshape(rpb.shape[0], NA_TILE_ROWS * GRID_W, NA_WIN_ROWS * GRID_W))
    return jnp.stack(tables, axis=0).astype(F32)


def _attention_d(q, k, v, bias, *, n_lat, seq, n_ctx, bsz):
    rows = seq // GRID_W
    tq = NA_TILE_ROWS * GRID_W
    win = NA_WIN_ROWS * GRID_W
    n_tiles = rows // NA_TILE_ROWS
    pos_len = n_ctx + seq
    assert rows % NA_TILE_ROWS == 0 and rows >= NA_WIN_ROWS and n_tiles >= 3

    def kind(i):
        return jnp.where(i == 0, 0, jnp.where(i == n_tiles - 1, 2, 1))

    return pl.pallas_call(
        functools.partial(_na_kernel, n_ctx=n_ctx, rows=rows),
        grid=(bsz, NA_HEADS, n_tiles),
        in_specs=[pl.BlockSpec((1, 1, tq, HEAD_DIM), lambda b, h, i: (b, h, i, 0)),
                  pl.BlockSpec((1, 1, pos_len, HEAD_DIM), lambda b, h, i: (b, h, 0, 0)),
                  pl.BlockSpec((1, 1, pos_len, HEAD_DIM), lambda b, h, i: (b, h, 0, 0)),
                  pl.BlockSpec((1, 1, tq, win), lambda b, h, i: (kind(i), h, 0, 0))],
        out_specs=pl.BlockSpec((tq, HEAD_DIM), lambda b, h, i: (b * n_tiles + i, h)),
        out_shape=jax.ShapeDtypeStruct((n_lat, NA_HEADS * HEAD_DIM), BF16),
        compiler_params=_cparams("arbitrary", "arbitrary", "arbitrary"),
        name="attn_d_neighbourhood",
    )(q, k, v, bias)


def _pack_bf16_pairs(f):
    half = f.shape[1] // 2
    bits = lax.bitcast_convert_type(f.astype(BF16).astype(F32), U32)
    lo = lax.shift_right_logical(bits[:, :half], jnp.uint32(16))
    hi = bits[:, half:] & jnp.uint32(0xFFFF0000)
    return hi | lo


def _unpack_bf16_pairs(bits):
    lo = lax.bitcast_convert_type(lax.shift_left(bits, jnp.uint32(16)), F32).astype(BF16)
    hi = lax.bitcast_convert_type(bits & jnp.uint32(0xFFFF0000), F32).astype(BF16)
    return lo, hi


def _store_row_tiles(ref, val):
    rows, width = val.shape
    per = width // 128
    for s in range(per):
        ref[pl.ds(s, rows, stride=per), :] = val[:, s * 128:(s + 1) * 128]


def _load_row_tiles(ref, rows, per):
    return jnp.concatenate([ref[pl.ds(s, rows, stride=per), :] for s in range(per)], axis=1)


def _outproj_kernel(x_ref, y1_ref, y2_ref, w_ref, mod_ref, g_ref, wrh_ref, wrl_ref, xo_ref, fpk_ref, lg_ref):
    half = w_ref.shape[0] // 2
    tm = x_ref.shape[0]
    per = fpk_ref.shape[0] // tm
    rows = ROW_TILE
    for r0 in range(0, tm, rows):
        rs = slice(r0, r0 + rows)
        delta = _dot(y1_ref[rs, :], w_ref[0:half, :]) + _dot(y2_ref[rs, :], w_ref[half:, :])
        xn = x_ref[rs, :] + mod_ref[0, 2:3, :] * delta
        xo_ref[rs, :] = xn
        f = _norm_mod(xn, g_ref[...], mod_ref[0, 3:4, :], mod_ref[0, 4:5, :])
        f_hi = f.astype(BF16)
        f_lo = (f - f_hi.astype(F32)).astype(BF16)
        lg_ref[:, rs] = _dot_nt(wrh_ref[...], f_hi) + (_dot_nt(wrl_ref[...], f_hi) + _dot_nt(wrh_ref[...], f_lo))
        _store_row_tiles(fpk_ref.at[pl.ds(r0 * per, rows * per)], _pack_bf16_pairs(f))


def _outproj(x, y1, y2, w_bf16, mod, g, wr_t, *, n_rows, n_lat, seq, bsz):
    d = x.shape[1]
    tm = 2 * ROW_TILE
    hw = y1.shape[1]
    wr_hi = wr_t.astype(BF16)
    wr_lo = (wr_t - wr_hi.astype(F32)).astype(BF16)
    return pl.pallas_call(
        _outproj_kernel,
        grid=(n_rows // tm,),
        in_specs=[pl.BlockSpec((tm, d), lambda i: (i, 0)),
                  pl.BlockSpec((tm, hw), lambda i: (i, 0)),
                  pl.BlockSpec((tm, hw), lambda i: (i, 0)),
                  pl.BlockSpec((2 * hw, d), lambda i: (0, 0)),
                  pl.BlockSpec((1, 6, d), lambda i: (_group_of_tile(i, tm, n_lat, seq, bsz), 0, 0)),
                  pl.BlockSpec((1, d), lambda i: (0, 0)),
                  pl.BlockSpec((N_EXPERTS, d), lambda i: (0, 0)),
                  pl.BlockSpec((N_EXPERTS, d), lambda i: (0, 0))],
        out_specs=[pl.BlockSpec((tm, d), lambda i: (i, 0)),
                   pl.BlockSpec((tm * (d // 256), 128), lambda i: (i, 0)),
                   pl.BlockSpec((N_EXPERTS, tm), lambda i: (0, i))],
        out_shape=[jax.ShapeDtypeStruct((n_rows, d), F32),
                   jax.ShapeDtypeStruct((n_rows * (d // 256), 128), U32),
                   jax.ShapeDtypeStruct((N_EXPERTS, n_rows), F32)],
        compiler_params=_cparams("arbitrary"),
        name="outproj_residual_moe_in",
    )(x, y1, y2, w_bf16, mod, g, wr_hi, wr_lo)


def _route_kernel(lg_ref, b_ref, e_ref, r_ref, w_ref, cnt_ref, run_ref):
    tr = lg_ref.shape[1]
    per = N_EXPERTS // N_GROUPS
    neg = -jnp.inf

    @pl.when(pl.program_id(0) == 0)
    def _():
        run_ref[...] = jnp.zeros(run_ref.shape, F32)

    scores = jax.nn.sigmoid(lg_ref[...])
    sel = scores + b_ref[...]
    sel3 = sel.reshape(N_GROUPS, per, tr)
    mem = lax.broadcasted_iota(I32, (N_GROUPS, per, tr), 1)
    m1 = jnp.max(sel3, axis=1, keepdims=True)
    i1 = jnp.min(jnp.where(sel3 == m1, mem, per), axis=1, keepdims=True)
    m2 = jnp.max(jnp.where(mem == i1, neg, sel3), axis=1, keepdims=True)
    gs = (m1 + m2).reshape(N_GROUPS, tr)
    gid = lax.broadcasted_iota(I32, (N_GROUPS, tr), 0)
    keep = jnp.zeros((N_GROUPS, tr), F32)
    for _ in range(TOPK_GROUPS):
        gm = jnp.max(gs, axis=0, keepdims=True)
        gi = jnp.min(jnp.where(gs == gm, gid, N_GROUPS), axis=0, keepdims=True)
        hit = gid == gi
        keep = jnp.where(hit, 1.0, keep)
        gs = jnp.where(hit, neg, gs)
    keep3 = jnp.broadcast_to(keep.reshape(N_GROUPS, 1, tr), (N_GROUPS, per, tr))
    cand = jnp.where(keep3 > 0.5, sel3, neg).reshape(N_EXPERTS, tr)
    eid = lax.broadcasted_iota(I32, (N_EXPERTS, tr), 0)
    idxs, ws = [], []
    sel_f = jnp.zeros((N_EXPERTS, tr), F32)
    for _ in range(TOP_K):
        cm = jnp.max(cand, axis=0, keepdims=True)
        ci = jnp.min(jnp.where(cand == cm, eid, N_EXPERTS), axis=0, keepdims=True)
        hit = eid == ci
        idxs.append(ci)
        ws.append(jnp.sum(jnp.where(hit, scores, 0.0), axis=0, keepdims=True))
        sel_f = jnp.where(hit, 1.0, sel_f)
        cand = jnp.where(hit, neg, cand)
    before = (lax.broadcasted_iota(I32, (tr, tr), 0) < lax.broadcasted_iota(I32, (tr, tr), 1))
    rank = _dot(sel_f.astype(BF16), jnp.where(before, 1.0, 0.0).astype(BF16)) + run_ref[...]
    run_ref[...] = run_ref[...] + jnp.sum(sel_f, axis=1, keepdims=True)
    cnt_ref[...] = run_ref[...]
    wsum = ws[0]
    for kk in range(1, TOP_K):
        wsum = wsum + ws[kk]
    for kk in range(TOP_K):
        e_ref[kk:kk + 1, :] = idxs[kk]
        r_ref[kk:kk + 1, :] = jnp.sum(jnp.where(eid == idxs[kk], rank, 0.0), axis=0, keepdims=True).astype(I32)
        w_ref[kk:kk + 1, :] = ws[kk] / wsum * ROUTE_SCALE


def _route(logits_t, b_col):
    n_e, t = logits_t.shape
    tr = ROW_TILE
    return pl.pallas_call(
        _route_kernel,
        grid=(t // tr,),
        in_specs=[pl.BlockSpec((n_e, tr), lambda i: (0, i)),
                  pl.BlockSpec((n_e, 1), lambda i: (0, 0))],
        out_specs=[pl.BlockSpec((TOP_K, tr), lambda i: (0, i)),
                   pl.BlockSpec((TOP_K, tr), lambda i: (0, i)),
                   pl.BlockSpec((TOP_K, tr), lambda i: (0, i)),
                   pl.BlockSpec((n_e, 1), lambda i: (0, 0))],
        out_shape=[jax.ShapeDtypeStruct((TOP_K, t), I32),
                   jax.ShapeDtypeStruct((TOP_K, t), I32),
                   jax.ShapeDtypeStruct((TOP_K, t), F32),
                   jax.ShapeDtypeStruct((n_e, 1), F32)],
        scratch_shapes=[pltpu.VMEM((n_e, 1), F32)],
        compiler_params=_cparams("arbitrary"),
        name="moe_route",
    )(logits_t, b_col)


PK_TILES = 8
Y_TILES = 16


def _scatter_kernel(dest_ref, f_ref, xs_ref, sem, *, tm):
    def copy(t, kk):
        src = pl.multiple_of(t * PK_TILES, PK_TILES)
        dst = pl.multiple_of(dest_ref[0, 0, t * TOP_K + kk], PK_TILES)
        return pltpu.make_async_copy(f_ref.at[pl.ds(src, PK_TILES)], xs_ref.at[pl.ds(dst, PK_TILES)], sem)

    def issue(t, c):
        for kk in range(TOP_K):
            copy(t, kk).start()
        return c

    def drain(t, c):
        for kk in range(TOP_K):
            copy(t, kk).wait()
        return c

    lax.fori_loop(0, tm, issue, 0)
    lax.fori_loop(0, tm, drain, 0)


def _scatter_rows(dest3, fpk, n_slots):
    tm = ROW_TILE
    t = fpk.shape[0] // PK_TILES
    return pl.pallas_call(
        functools.partial(_scatter_kernel, tm=tm),
        grid=(t // tm,),
        in_specs=[pl.BlockSpec((1, 1, TOP_K * tm), lambda i: (i, 0, 0), memory_space=pltpu.SMEM),
                  pl.BlockSpec((tm * PK_TILES, 128), lambda i: (i, 0))],
        out_specs=pl.BlockSpec(memory_space=pl.ANY),
        out_shape=jax.ShapeDtypeStruct((n_slots * PK_TILES, 128), U32),
        scratch_shapes=[pltpu.SemaphoreType.DMA(())],
        compiler_params=_cparams("arbitrary"),
        name="moe_scatter_rows",
    )(dest3, fpk)


def _ffn(bits, w1, w3, w2):
    half = bits.shape[1]
    lo, hi = _unpack_bf16_pairs(bits)
    h1 = _dot(lo, w1[0:half, :]) + _dot(hi, w1[half:, :])
    h3 = _dot(lo, w3[0:half, :]) + _dot(hi, w3[half:, :])
    return _dot((_silu(h1) * h3).astype(BF16), w2[...])


def _expert_kernel(be_ref, bv_ref, nu_ref, xs_ref, w1_ref, w3_ref, w2_ref, y_ref, w1b, w3b, w2b):
    i = pl.program_id(0)

    @pl.when(i < nu_ref[0])
    def _():
        e = be_ref[i]
        prev = be_ref[jnp.maximum(i - 1, 0)]

        @pl.when((i == 0) | (e != prev))
        def _():
            w1b[...] = w1_ref[0, 0].astype(BF16)
            w3b[...] = w3_ref[0, 0].astype(BF16)
            w2b[...] = w2_ref[0, 0].astype(BF16)

        tm = MOE_BLOCK
        bits = _load_row_tiles(xs_ref, tm, PK_TILES)
        rows = lax.broadcasted_iota(I32, bits.shape, 0)
        bits = jnp.where(rows < bv_ref[i], bits, jnp.uint32(0))
        _store_row_tiles(y_ref, _pack_bf16_pairs(_ffn(bits, w1b, w3b, w2b)))


def _experts(blk_expert, blk_valid, n_used, xs, w1, w3, w2, layer):
    n_slots = xs.shape[0] // PK_TILES
    d, de = w1.shape[2], w1.shape[3]
    assert d == 2 * PK_TILES * 128
    tm = MOE_BLOCK
    grid_spec = pltpu.PrefetchScalarGridSpec(
        num_scalar_prefetch=3,
        grid=(n_slots // tm,),
        in_specs=[pl.BlockSpec((tm * PK_TILES, 128), lambda i, be, bv, nu: (i, 0)),
                  pl.BlockSpec((1, 1, d, de), lambda i, be, bv, nu: (layer, be[i], 0, 0)),
                  pl.BlockSpec((1, 1, d, de), lambda i, be, bv, nu: (layer, be[i], 0, 0)),
                  pl.BlockSpec((1, 1, de, d), lambda i, be, bv, nu: (layer, be[i], 0, 0))],
        out_specs=pl.BlockSpec((tm * PK_TILES, 128), lambda i, be, bv, nu: (i, 0)),
        scratch_shapes=[pltpu.VMEM((d, de), BF16), pltpu.VMEM((d, de), BF16), pltpu.VMEM((de, d), BF16)],
    )
    return pl.pallas_call(
        _expert_kernel,
        grid_spec=grid_spec,
        out_shape=jax.ShapeDtypeStruct((n_slots * PK_TILES, 128), U32),
        compiler_params=_cparams("arbitrary"),
        name="moe_experts",
    )(blk_expert, blk_valid, n_used, xs, w1, w3, w2)


def _shared_kernel(f_ref, w1_ref, w3_ref, w2_ref, y_ref):
    y_ref[...] = _ffn(_load_row_tiles(f_ref, y_ref.shape[0], PK_TILES), w1_ref, w3_ref, w2_ref)


def _shared_expert(fpk, w1, w3, w2):
    t = fpk.shape[0] // PK_TILES
    d, de = w1.shape
    tm = 2 * ROW_TILE
    return pl.pallas_call(
        _shared_kernel,
        grid=(t // tm,),
        in_specs=[pl.BlockSpec((tm * PK_TILES, 128), lambda i: (i, 0)),
                  pl.BlockSpec((d, de), lambda i: (0, 0)),
                  pl.BlockSpec((d, de), lambda i: (0, 0)),
                  pl.BlockSpec((de, d), lambda i: (0, 0))],
        out_specs=pl.BlockSpec((tm, d), lambda i: (i, 0)),
        out_shape=jax.ShapeDtypeStruct((t, d), F32),
        compiler_params=_cparams("arbitrary"),
        name="moe_shared_expert",
    )(fpk, w1, w3, w2)


def _combine_kernel(dest_ref, dest_next_ref, w_ref, x_ref, ysh_ref, mod_ref, y_ref, o_ref, ybuf, lo_ref, hi_ref, sems):
    tm = x_ref.shape[0]
    i = pl.program_id(0)
    n = pl.num_programs(0)
    slot = i % 2

    def copy(idx_ref, s, t, kk):
        src = pl.multiple_of(idx_ref[0, 0, t * TOP_K + kk], PK_TILES)
        dst = pl.multiple_of(t * PK_TILES, PK_TILES)
        return pltpu.make_async_copy(y_ref.at[pl.ds(src, PK_TILES)], ybuf.at[s, kk, pl.ds(dst, PK_TILES)],
                                     sems.at[s])

    def issue_tile(idx_ref, s):
        def issue(t, c):
            for kk in range(TOP_K):
                copy(idx_ref, s, t, kk).start()
            return c

        lax.fori_loop(0, tm, issue, 0)

    @pl.when(i == 0)
    def _():
        issue_tile(dest_ref, 0)

    @pl.when(i + 1 < n)
    def _():
        issue_tile(dest_next_ref, 1 - slot)

    def drain(t, c):
        for kk in range(TOP_K):
            copy(dest_ref, slot, t, kk).wait()
        return c

    lax.fori_loop(0, tm, drain, 0)

    def token(t, c):
        r = pl.multiple_of(t * PK_TILES, PK_TILES)
        a_lo = a_hi = None
        for kk in range(TOP_K):
            w = w_ref[0, 0, t * TOP_K + kk]
            bits = ybuf[slot, kk, pl.ds(r, PK_TILES), :]
            lo = lax.bitcast_convert_type(lax.shift_left(bits, jnp.uint32(16)), F32)
            hi = lax.bitcast_convert_type(bits & jnp.uint32(0xFFFF0000), F32)
            a_lo = w * lo if a_lo is None else a_lo + w * lo
            a_hi = w * hi if a_hi is None else a_hi + w * hi
        lo_ref[pl.ds(r, PK_TILES), :] = a_lo
        hi_ref[pl.ds(r, PK_TILES), :] = a_hi
        return c

    lax.fori_loop(0, tm, token, 0, unroll=4)
    half = PK_TILES * 128
    for c in range(PK_TILES):
        for base, acc_ref in ((0, lo_ref), (half, hi_ref)):
            sl = slice(base + c * 128, base + (c + 1) * 128)
            routed = acc_ref[pl.ds(c, tm, stride=PK_TILES), :]
            o_ref[:, sl] = x_ref[:, sl] + mod_ref[0, 5:6, sl] * (routed + ysh_ref[:, sl])


def _combine(dest3, x, ysh, w3, mod, y, *, tm, n_lat, seq, bsz):
    t, d = x.shape
    n = t // tm
    return pl.pallas_call(
        _combine_kernel,
        grid=(n,),
        in_specs=[pl.BlockSpec((1, 1, TOP_K * tm), lambda i: (i, 0, 0), memory_space=pltpu.SMEM),
                  pl.BlockSpec((1, 1, TOP_K * tm), lambda i: (jnp.minimum(i + 1, n - 1), 0, 0),
                               memory_space=pltpu.SMEM),
                  pl.BlockSpec((1, 1, TOP_K * tm), lambda i: (i, 0, 0), memory_space=pltpu.SMEM),
                  pl.BlockSpec((tm, d), lambda i: (i, 0)),
                  pl.BlockSpec((tm, d), lambda i: (i, 0)),
                  pl.BlockSpec((1, 6, d), lambda i: (_group_of_tile(i, tm, n_lat, seq, bsz), 0, 0)),
                  pl.BlockSpec(memory_space=pl.ANY)],
        out_specs=pl.BlockSpec((tm, d), lambda i: (i, 0)),
        out_shape=jax.ShapeDtypeStruct((t, d), F32),
        scratch_shapes=[pltpu.VMEM((2, TOP_K, tm * PK_TILES, 128), U32),
                        pltpu.VMEM((tm * PK_TILES, 128), F32), pltpu.VMEM((tm * PK_TILES, 128), F32),
                        pltpu.SemaphoreType.DMA((2,))],
        compiler_params=_cparams("arbitrary"),
        name="moe_combine",
    )(dest3, dest3, w3, x, ysh, mod, y)


def _moe(x_new, fpk, logits_t, mod, b_router, w_e1, w_e3, w_e2, w_s1, w_s3, w_s2, *, layer, n_lat, seq, bsz):
    t = x_new.shape[0]
    top_e, top_r, top_w, counts = _route(logits_t, b_router.reshape(N_EXPERTS, 1))
    counts = counts[:, 0].astype(I32)
    blocks = (counts + MOE_BLOCK - 1) // MOE_BLOCK
    blk_end = jnp.cumsum(blocks)
    blk_start = blk_end - blocks
    n_blocks = -(-(t * TOP_K) // MOE_BLOCK) + N_EXPERTS
    n_slots = n_blocks * MOE_BLOCK
    slot_start = blk_start * MOE_BLOCK
    dest = jnp.sum(jnp.where(top_e[None] == jnp.arange(N_EXPERTS, dtype=I32)[:, None, None],
                             slot_start[:, None, None], 0), axis=0) + top_r
    bi = jnp.arange(n_blocks, dtype=I32)
    blk_expert = jnp.minimum(jnp.sum((bi[:, None] >= blk_end[None, :]).astype(I32), axis=1), N_EXPERTS - 1)
    blk_valid = jnp.clip(counts[blk_expert] - (bi - blk_start[blk_expert]) * MOE_BLOCK, 0, MOE_BLOCK)
    n_used = blk_end[-1:].astype(I32)

    dest_tk = dest.T
    tm_s = ROW_TILE
    dest_s = (dest_tk * PK_TILES).reshape(t // tm_s, 1, tm_s * TOP_K)
    xs = _scatter_rows(dest_s, fpk, n_slots)
    y = _experts(blk_expert.astype(I32), blk_valid.astype(I32), n_used, xs, w_e1, w_e3, w_e2, layer)
    ysh = _shared_expert(fpk, w_s1.astype(BF16), w_s3.astype(BF16), w_s2.astype(BF16))
    tm_c = ROW_TILE
    dest_c = (dest_tk * PK_TILES).reshape(t // tm_c, 1, tm_c * TOP_K)
    w_c = top_w.T.reshape(t // tm_c, 1, tm_c * TOP_K)
    return _combine(dest_c, x_new, ysh, w_c, mod, y, tm=tm_c, n_lat=n_lat, seq=seq, bsz=bsz)


def _pad_cols(w, n):
    return jnp.pad(w, ((0, 0), (0, n - w.shape[1])))


def kernel(x, c, ctx, c_ctx, w_ada, b_ada, g_mix, g_ffn, w_in_even, gq_a, gk_a, conv_w, conv_b, a_log, dt_bias,
           d_skip, g_ssm, w_in_odd, gq_c, gk_c, sink_c, gq_d, gk_d, rpb_d, w_out, w_router, b_router, w_e1, w_e3,
           w_e2, w_s1, w_s3, w_s2):
    bsz, seq, d = x.shape
    n_ctx = ctx.shape[1]
    depth = w_ada.shape[0]
    n_lat = bsz * seq
    geo = dict(n_lat=n_lat, seq=seq, n_ctx=n_ctx, bsz=bsz)
    xs = jnp.concatenate([x.reshape(n_lat, d), ctx.reshape(bsz * n_ctx, d)], axis=0)
    cvec = jnp.zeros((8, d), F32).at[:bsz].set(c).at[bsz].set(c_ctx)
    cos, sin = _rope_tables(seq, n_ctx)
    expand = jnp.zeros((128, SSM_INNER), BF16).at[:SSM_HEADS].set(
        jnp.repeat(jnp.eye(SSM_HEADS, dtype=BF16), SSM_HEAD_DIM, axis=1))

    for layer in range(depth):
        last = layer == depth - 1
        i = layer // 2
        mod = _adaln(cvec, w_ada, b_ada[layer].reshape(1, -1), layer)[:bsz + 1].reshape(bsz + 1, 6, d)
        g1 = g_mix[layer].reshape(1, d)
        if layer % 2 == 0:
            n_in = w_in_even.shape[2]
            n_pad = -(-n_in // 384) * 384
            p = _norm_mod_matmul(xs, g1, mod, _pad_cols(w_in_even[i], n_pad).astype(BF16),
                                 n_lat=n_lat, seq=seq, bsz=bsz, tn=n_pad // 3)
            qt, k, vt = _prep_rope(p, cos, sin, gq_a[i].reshape(1, -1), gk_a[i].reshape(1, -1),
                                   q_scale=HEAD_DIM ** -0.5 * LOG2E, n_q=A_HEADS, n_kv=A_KV_HEADS, **geo)
            y1 = _attention_a(qt, k, vt, need_ctx=not last, **geo)
            xbc = _ssm_conv(p, conv_w[i], conv_b[i].reshape(1, -1), col0=A_IN + SSM_INNER,
                            n_lat=n_lat, seq=seq, n_ctx=n_ctx)
            dt_row = jnp.zeros((1, 128), F32).at[0, :2 * SSM_HEADS].set(dt_bias[i].reshape(-1))
            dt2 = _ssm_dt(p, dt_row, col0=A_IN + SSM_INNER + SSM_CONV_DIM)
            a_rows = jnp.zeros((2, 1, 128), F32).at[:, 0, :SSM_HEADS].set(-jnp.exp(a_log[i]))
            ydir = _ssd_scan(xbc, dt2, a_rows, expand, **geo)
            y2 = _ssd_finish(ydir, xbc, p, jnp.repeat(d_skip[i], SSM_HEAD_DIM).reshape(1, -1),
                             g_ssm[i].reshape(1, -1), z_col0=A_IN)
        else:
            p = _norm_mod_matmul(xs, g1, mod, w_in_odd[i].astype(BF16), n_lat=n_lat, seq=seq, bsz=bsz,
                                 tn=w_in_odd.shape[2] // 3)
            qt, k, vt = _prep_rope(p, cos, sin, gq_c[i].reshape(1, -1), gk_c[i].reshape(1, -1),
                                   q_scale=HEAD_DIM ** -0.5, n_q=A_HEADS, n_kv=A_KV_HEADS, **geo)
            group = A_HEADS // A_KV_HEADS
            sink_rows = jnp.repeat(sink_c[i].reshape(A_KV_HEADS, group), Q_TILE, axis=1).reshape(
                A_KV_HEADS, 1, group * Q_TILE)
            y1 = _attention_c(qt, k, vt, sink_rows, **geo)
            qd, kd, vd = _prep_na(p, gq_d[i].reshape(1, -1), gk_d[i].reshape(1, -1), col0=A_IN, **geo)
            y2 = _attention_d(qd, kd, vd, _na_bias_tables(rpb_d[i], seq // GRID_W), **geo)
        n_rows = n_lat if last else xs.shape[0]
        x_new, fpk, logits_t = _outproj(xs, y1, y2, w_out[layer].astype(BF16), mod, g_ffn[layer].reshape(1, d),
                                        w_router[layer].T, n_rows=n_rows, n_lat=n_lat, seq=seq, bsz=bsz)
        xs = _moe(x_new, fpk, logits_t, mod, b_router[layer], w_e1, w_e3, w_e2,
                  w_s1[layer], w_s3[layer], w_s2[layer], layer=layer, n_lat=n_lat, seq=seq, bsz=bsz)
    return xs[:n_lat].reshape(bsz, seq, d)
```

```python
import functools
import math

import jax
import jax.numpy as jnp
from jax import lax
from jax.experimental import pallas as pl
from jax.experimental.pallas import tpu as pltpu

F32 = jnp.float32
BF16 = jnp.bfloat16
I32 = jnp.int32
U32 = jnp.uint32

EPS = 1e-6
HEAD_DIM = 128
GRID_W = 64
ROPE_THETA = 10000.0
WINDOW = 128
NA_ROWS = 8
NA_COLS = 16
NA_TILE_ROWS = 8
NA_WIN_ROWS = 16
A_HEADS = 8
A_KV_HEADS = 2
SSM_HEADS = 16
SSM_HEAD_DIM = 64
SSM_INNER = SSM_HEADS * SSM_HEAD_DIM
SSM_GROUPS = 2
SSM_STATE = 128
SSM_CONV = 5
SSM_CONV_DIM = SSM_INNER + 2 * SSM_GROUPS * SSM_STATE
SSM_CHUNK = 128
A_IN = (A_HEADS + 2 * A_KV_HEADS) * HEAD_DIM
NA_HEADS = 8
N_EXPERTS = 64
TOP_K = 8
N_GROUPS = 8
TOPK_GROUPS = 4
ROUTE_SCALE = 2.5
MOE_BLOCK = 512

VMEM_LIMIT_BYTES = 56 * 1024 * 1024
ROW_TILE = 256
Q_TILE = 256
VT_ROWS = HEAD_DIM + 16
LOG2E = 1.4426950408889634
HI = lax.Precision.HIGHEST


def _cparams(*sem):
    return pltpu.CompilerParams(dimension_semantics=sem, vmem_limit_bytes=VMEM_LIMIT_BYTES)


def _silu(x):
    return x * jax.nn.sigmoid(x)


def _hi_dot(a, b):
    return jnp.dot(a, b, precision=HI, preferred_element_type=F32)


def _dot(a, b):
    return jnp.dot(a, b, preferred_element_type=F32)


def _dot_nt(a, b):
    return lax.dot_general(a, b, (((1,), (1,)), ((), ())), preferred_element_type=F32)


def _largest_tile(n, cap, mult=128):
    best = None
    for t in range(mult, cap + 1, mult):
        if n % t == 0:
            best = t
    assert best is not None, (n, cap)
    return best


def _adaln_kernel(c_ref, w_ref, b_ref, o_ref):
    o_ref[...] = _hi_dot(_silu(c_ref[...]), w_ref[0]) + b_ref[...]


def _adaln(cvec, w, b, layer):
    _, d, n = w.shape
    tn = 1024
    return pl.pallas_call(
        _adaln_kernel,
        grid=(n // tn,),
        in_specs=[pl.BlockSpec((8, d), lambda j: (0, 0)),
                  pl.BlockSpec((1, d, tn), lambda j: (layer, 0, j)),
                  pl.BlockSpec((1, tn), lambda j: (0, j))],
        out_specs=pl.BlockSpec((8, tn), lambda j: (0, j)),
        out_shape=jax.ShapeDtypeStruct((8, n), F32),
        compiler_params=_cparams("arbitrary"),
        name="adaln",
    )(cvec, w, b)


def _norm_mod(x, g, shift, scale):
    ms = jnp.mean(x * x, axis=-1, keepdims=True)
    return (x * lax.rsqrt(ms + EPS) * g) * (1.0 + scale) + shift


def _nmm_kernel(x_ref, g_ref, mod_ref, w_ref, o_ref, xn_ref):
    @pl.when(pl.program_id(1) == 0)
    def _():
        xn_ref[...] = _norm_mod(x_ref[...], g_ref[...], mod_ref[0, 0:1, :], mod_ref[0, 1:2, :]).astype(BF16)

    o_ref[...] = _dot(xn_ref[...], w_ref[...])


def _group_of_tile(i, tm, n_lat, seq, bsz):
    return jnp.where(i * tm < n_lat, (i * tm) // seq, bsz)


def _norm_mod_matmul(x, g, mod, w_bf16, *, n_lat, seq, bsz, tn):
    t, d = x.shape
    n = w_bf16.shape[1]
    tm = 2 * ROW_TILE
    assert t % tm == 0 and n % tn == 0 and n_lat % tm == 0
    return pl.pallas_call(
        _nmm_kernel,
        grid=(t // tm, n // tn),
        in_specs=[pl.BlockSpec((tm, d), lambda i, j: (i, 0)),
                  pl.BlockSpec((1, d), lambda i, j: (0, 0)),
                  pl.BlockSpec((1, 6, d), lambda i, j: (_group_of_tile(i, tm, n_lat, seq, bsz), 0, 0)),
                  pl.BlockSpec((d, tn), lambda i, j: (0, j))],
        out_specs=pl.BlockSpec((tm, tn), lambda i, j: (i, j)),
        out_shape=jax.ShapeDtypeStruct((t, n), F32),
        scratch_shapes=[pltpu.VMEM((tm, d), BF16)],
        compiler_params=_cparams("arbitrary", "arbitrary"),
        name="norm_mod_inproj",
    )(x, g, mod, w_bf16)


def _head_norm(x, g):
    ms = jnp.mean(x * x, axis=-1, keepdims=True)
    return x * lax.rsqrt(ms + EPS) * g


def _prep_rope_kernel(p_ref, cos_ref, sin_ref, gq_ref, gk_ref, qt_ref, k_ref, vt_ref, *, n_q, n_kv, tq, q_scale):
    cos = cos_ref[...]
    sin = sin_ref[...]
    lane = lax.broadcasted_iota(I32, cos.shape, 1)
    first_half = (lane % 64) < 32

    def norm_rope(x, g):
        y = _head_norm(x, g)
        swapped = jnp.where(first_half, pltpu.roll(y, 96, 1), pltpu.roll(y, 32, 1))
        return y * cos + swapped * sin

    group = n_q // n_kv
    for h in range(n_q):
        q = norm_rope(p_ref[:, h * HEAD_DIM:(h + 1) * HEAD_DIM], gq_ref[...]) * q_scale
        kvh, hh = divmod(h, group)
        qt_ref[0, kvh, 0, :, hh * tq:(hh + 1) * tq] = q.T.astype(BF16)
    for h in range(n_kv):
        c0 = (n_q + h) * HEAD_DIM
        k_ref[0, h] = norm_rope(p_ref[:, c0:c0 + HEAD_DIM], gk_ref[...]).astype(BF16)
        c1 = (n_q + n_kv + h) * HEAD_DIM
        vt_ref[0, h, 0:HEAD_DIM, :] = p_ref[:, c1:c1 + HEAD_DIM].T.astype(BF16)
        vt_ref[0, h, HEAD_DIM:VT_ROWS, :] = jnp.ones((VT_ROWS - HEAD_DIM, tq), BF16)


def _pos_maps(tm, n_lat, seq, n_ctx):
    lat_tiles, seq_tiles, ctx_tiles = n_lat // tm, seq // tm, n_ctx // tm

    def batch_of(i):
        return jnp.where(i < lat_tiles, i // seq_tiles, (i - lat_tiles) // ctx_tiles)

    def pos_of(i):
        return jnp.where(i < lat_tiles, ctx_tiles + i % seq_tiles, (i - lat_tiles) % ctx_tiles)

    return batch_of, pos_of


def _prep_rope(p, cos, sin, gq, gk, *, q_scale, n_q, n_kv, n_lat, seq, n_ctx, bsz):
    t = p.shape[0]
    tm = Q_TILE
    pos_len = n_ctx + seq
    group = n_q // n_kv
    width = (n_q + 2 * n_kv) * HEAD_DIM
    batch_of, pos_of = _pos_maps(tm, n_lat, seq, n_ctx)
    kern = functools.partial(_prep_rope_kernel, n_q=n_q, n_kv=n_kv, tq=tm, q_scale=q_scale)
    return pl.pallas_call(
        kern,
        grid=(t // tm,),
        in_specs=[pl.BlockSpec((tm, width), lambda i: (i, 0)),
                  pl.BlockSpec((tm, HEAD_DIM), lambda i: (pos_of(i), 0)),
                  pl.BlockSpec((tm, HEAD_DIM), lambda i: (pos_of(i), 0)),
                  pl.BlockSpec((1, HEAD_DIM), lambda i: (0, 0)),
                  pl.BlockSpec((1, HEAD_DIM), lambda i: (0, 0))],
        out_specs=[pl.BlockSpec((1, n_kv, 1, HEAD_DIM, group * tm), lambda i: (batch_of(i), 0, pos_of(i), 0, 0)),
                   pl.BlockSpec((1, n_kv, tm, HEAD_DIM), lambda i: (batch_of(i), 0, pos_of(i), 0)),
                   pl.BlockSpec((1, n_kv, VT_ROWS, tm), lambda i: (batch_of(i), 0, 0, pos_of(i)))],
        out_shape=[jax.ShapeDtypeStruct((bsz, n_kv, pos_len // tm, HEAD_DIM, group * tm), BF16),
                   jax.ShapeDtypeStruct((bsz, n_kv, pos_len, HEAD_DIM), BF16),
                   jax.ShapeDtypeStruct((bsz, n_kv, VT_ROWS, pos_len), BF16)],
        compiler_params=_cparams("arbitrary"),
        name="prep_rope",
    )(p, cos, sin, gq, gk)


def _rope_tables(seq, n_ctx):
    t = jnp.arange(seq, dtype=I32)
    row = (t // GRID_W).astype(F32)
    col = (t % GRID_W).astype(F32)
    n_freq = HEAD_DIM // 4
    inv = ROPE_THETA ** (-jnp.arange(n_freq, dtype=F32) / n_freq)
    ar, ac = row[:, None] * inv, col[:, None] * inv
    cos = jnp.concatenate([jnp.cos(ar), jnp.cos(ar), jnp.cos(ac), jnp.cos(ac)], axis=1)
    sin = jnp.concatenate([-jnp.sin(ar), jnp.sin(ar), -jnp.sin(ac), jnp.sin(ac)], axis=1)
    cos = jnp.concatenate([jnp.ones((n_ctx, HEAD_DIM), F32), cos], axis=0)
    sin = jnp.concatenate([jnp.zeros((n_ctx, HEAD_DIM), F32), sin], axis=0)
    return cos, sin


def _flash_t_kernel(qt_ref, k_ref, vt_ref, o_ref, m_ref, l_ref, acc_ref, sa_ref, sb_ref, *, tk, n_k, tq, group):
    qt = qt_ref[0, 0, 0]
    m_ref[...] = jnp.full(m_ref.shape, -jnp.inf, F32)
    l_ref[...] = jnp.zeros(l_ref.shape, F32)
    acc_ref[...] = jnp.zeros(acc_ref.shape, F32)

    def scores(j, dst):
        off = pl.multiple_of(jnp.minimum(j, n_k - 1) * tk, tk)
        dst[...] = _dot(k_ref[0, 0, pl.ds(off, tk), :], qt)

    def consume(j, src):
        off = pl.multiple_of(j * tk, tk)
        s = src[...]
        m_old = m_ref[...]
        m_new = jnp.maximum(m_old, jnp.max(s, axis=0, keepdims=True))
        alpha = jnp.exp2(m_old - m_new)
        p = jnp.exp2(s - m_new).astype(BF16)
        r = _dot(vt_ref[0, 0, :, pl.ds(off, tk)], p)
        acc_ref[...] = alpha * acc_ref[...] + r[0:HEAD_DIM]
        l_ref[...] = alpha * l_ref[...] + r[HEAD_DIM:HEAD_DIM + 1]
        m_ref[...] = m_new

    scores(0, sa_ref)

    def body(jj, carry):
        j = 2 * jj
        scores(j + 1, sb_ref)
        consume(j, sa_ref)
        scores(j + 2, sa_ref)
        consume(j + 1, sb_ref)
        return carry

    lax.fori_loop(0, n_k // 2, body, 0)
    if n_k % 2:
        consume(n_k - 1, sa_ref)
    out = acc_ref[...] / l_ref[...]
    for h in range(group):
        o_ref[:, h * HEAD_DIM:(h + 1) * HEAD_DIM] = out[:, h * tq:(h + 1) * tq].T.astype(o_ref.dtype)


def _attention_a(qt, k, vt, *, n_lat, seq, n_ctx, bsz, need_ctx):
    n_kv, group, tq = k.shape[1], qt.shape[-1] // Q_TILE, Q_TILE
    pos_len = n_ctx + seq
    width = n_kv * group * HEAD_DIM
    tk = _largest_tile(pos_len, 1280)

    def scratch(tkk):
        n = group * tq
        return [pltpu.VMEM((1, n), F32), pltpu.VMEM((1, n), F32), pltpu.VMEM((HEAD_DIM, n), F32),
                pltpu.VMEM((tkk, n), F32), pltpu.VMEM((tkk, n), F32)]

    y = pl.pallas_call(
        functools.partial(_flash_t_kernel, tk=tk, n_k=pos_len // tk, tq=tq, group=group),
        grid=(bsz, n_kv, seq // tq),
        in_specs=[pl.BlockSpec((1, 1, 1, HEAD_DIM, group * tq), lambda b, h, i: (b, h, n_ctx // tq + i, 0, 0)),
                  pl.BlockSpec((1, 1, pos_len, HEAD_DIM), lambda b, h, i: (b, h, 0, 0)),
                  pl.BlockSpec((1, 1, VT_ROWS, pos_len), lambda b, h, i: (b, h, 0, 0))],
        out_specs=pl.BlockSpec((tq, group * HEAD_DIM), lambda b, h, i: (b * (seq // tq) + i, h)),
        out_shape=jax.ShapeDtypeStruct((n_lat, width), BF16),
        scratch_shapes=scratch(tk),
        compiler_params=_cparams("arbitrary", "arbitrary", "arbitrary"),
        name="attn_a_latent",
    )(qt, k, vt)
    if not need_ctx:
        return y
    tkc = _largest_tile(n_ctx, 768)
    y_ctx = pl.pallas_call(
        functools.partial(_flash_t_kernel, tk=tkc, n_k=n_ctx // tkc, tq=tq, group=group),
        grid=(bsz, n_kv, n_ctx // tq),
        in_specs=[pl.BlockSpec((1, 1, 1, HEAD_DIM, group * tq), lambda b, h, i: (b, h, i, 0, 0)),
                  pl.BlockSpec((1, 1, n_ctx, HEAD_DIM), lambda b, h, i: (b, h, 0, 0)),
                  pl.BlockSpec((1, 1, VT_ROWS, n_ctx), lambda b, h, i: (b, h, 0, 0))],
        out_specs=pl.BlockSpec((tq, group * HEAD_DIM), lambda b, h, i: (b * (n_ctx // tq) + i, h)),
        out_shape=jax.ShapeDtypeStruct((bsz * n_ctx, width), BF16),
        scratch_shapes=scratch(tkc),
        compiler_params=_cparams("arbitrary", "arbitrary", "arbitrary"),
        name="attn_a_context",
    )(qt, k, vt)
    return jnp.concatenate([y, y_ctx], axis=0)


def _segment_edges(i, tm, n_lat, seq, n_ctx):
    lat_tiles, seq_tiles, ctx_tiles = n_lat // tm, seq // tm, n_ctx // tm
    is_lat = i < lat_tiles
    pos = jnp.where(is_lat, i % seq_tiles, (i - lat_tiles) % ctx_tiles)
    last = jnp.where(is_lat, seq_tiles, ctx_tiles) - 1
    return pos == 0, pos == last


def _conv_kernel(cur_ref, prev_ref, next_ref, w_ref, b_ref, o_ref, ext_ref, *, tm, n_lat, seq, n_ctx):
    first, last = _segment_edges(pl.program_id(0), tm, n_lat, seq, n_ctx)
    ext_ref[0:8, :] = jnp.where(first, 0.0, prev_ref[...])
    ext_ref[8:8 + tm, :] = cur_ref[...]
    ext_ref[8 + tm:16 + tm, :] = jnp.where(last, 0.0, next_ref[...])
    acc = jnp.broadcast_to(b_ref[...], (tm, b_ref.shape[1]))
    half = SSM_CONV // 2
    for kk in range(SSM_CONV):
        acc = acc + w_ref[kk:kk + 1, :] * ext_ref[pl.ds(8 - half + kk, tm), :]
    o_ref[...] = _silu(acc)


def _ssm_conv(p, conv_w, conv_b, *, col0, n_lat, seq, n_ctx):
    t = p.shape[0]
    tm, tc = ROW_TILE, 512
    assert col0 % tc == 0 and SSM_CONV_DIM % tc == 0
    cb0 = col0 // tc
    r8 = tm // 8
    kern = functools.partial(_conv_kernel, tm=tm, n_lat=n_lat, seq=seq, n_ctx=n_ctx)
    return pl.pallas_call(
        kern,
        grid=(t // tm, SSM_CONV_DIM // tc),
        in_specs=[pl.BlockSpec((tm, tc), lambda i, j: (i, cb0 + j)),
                  pl.BlockSpec((8, tc), lambda i, j: (jnp.maximum(i * r8 - 1, 0), cb0 + j)),
                  pl.BlockSpec((8, tc), lambda i, j: (jnp.minimum((i + 1) * r8, t // 8 - 1), cb0 + j)),
                  pl.BlockSpec((SSM_CONV, tc), lambda i, j: (0, j)),
                  pl.BlockSpec((1, tc), lambda i, j: (0, j))],
        out_specs=pl.BlockSpec((tm, tc), lambda i, j: (i, j)),
        out_shape=jax.ShapeDtypeStruct((t, SSM_CONV_DIM), F32),
        scratch_shapes=[pltpu.VMEM((tm + 16, tc), F32)],
        compiler_params=_cparams("arbitrary", "arbitrary"),
        name="ssm_conv",
    )(p, p, p, conv_w, conv_b)


def _dt_kernel(p_ref, b_ref, o_ref):
    x = p_ref[...] + b_ref[...]
    sp = jnp.maximum(x, 0.0) + jnp.log1p(jnp.exp(-jnp.abs(x)))
    o_ref[0] = sp
    o_ref[1] = pltpu.roll(sp, HEAD_DIM - SSM_HEADS, 1)


def _ssm_dt(p, bias_row, *, col0):
    t = p.shape[0]
    tm = 2 * ROW_TILE
    assert col0 % 128 == 0
    return pl.pallas_call(
        _dt_kernel,
        grid=(t // tm,),
        in_specs=[pl.BlockSpec((tm, 128), lambda i: (i, col0 // 128)),
                  pl.BlockSpec((1, 128), lambda i: (0, 0))],
        out_specs=pl.BlockSpec((2, tm, 128), lambda i: (0, i, 0)),
        out_shape=jax.ShapeDtypeStruct((2, t, 128), F32),
        compiler_params=_cparams("arbitrary"),
        name="ssm_dt",
    )(p, bias_row)


def _ssd_kernel(x_ref, bc_ref, dt_ref, a_ref, e_ref, y_ref, h_ref):
    d = pl.program_id(1)
    lc = SSM_CHUNK
    gw = SSM_INNER // SSM_GROUPS
    hpg = SSM_HEADS // SSM_GROUPS

    @pl.when(pl.program_id(2) == 0)
    def _():
        h_ref[...] = jnp.zeros(h_ref.shape, F32)

    dt = dt_ref[0]
    w = dt * a_ref[0]
    row = lax.broadcasted_iota(I32, (lc, lc), 0)
    col = lax.broadcasted_iota(I32, (lc, lc), 1)
    fwd = d == 0
    sgn = jnp.where(fwd, 1, -1)
    allowed = (row - col) * sgn >= 0
    allowed_t = (col - row) * sgn >= 0
    cs = _hi_dot(allowed.astype(F32), w)
    cs_t = _hi_dot(w.T, allowed_t.astype(F32))
    dt_t = dt.T
    tot = jnp.where(fwd, cs[lc - 1:lc, :], cs[0:1, :])
    e = e_ref[...]

    def expand(a):
        hi = a.astype(BF16)
        lo = (a - hi.astype(F32)).astype(BF16)
        return _dot(hi, e) + _dot(lo, e)

    x = x_ref[...]
    xw = (x * expand(jnp.exp(tot - cs) * dt)).astype(BF16)
    off_scale = expand(jnp.exp(cs))
    state_decay = expand(jnp.exp(jnp.broadcast_to(tot, (16, 128))))[0:1, :]
    xb = x.astype(BF16)
    for g in range(SSM_GROUPS):
        bm = bc_ref[:, g * SSM_STATE:(g + 1) * SSM_STATE]
        cm = bc_ref[:, (SSM_GROUPS + g) * SSM_STATE:(SSM_GROUPS + g + 1) * SSM_STATE].astype(BF16)
        cb = _dot_nt(cm, bm.astype(BF16))
        h_prev = h_ref[g]
        y_off = _dot(cm, h_prev.astype(BF16))
        s_chunk = _dot(bm.T.astype(BF16), xw[:, g * gw:(g + 1) * gw])
        h_ref[g] = state_decay[:, g * gw:(g + 1) * gw] * h_prev + s_chunk
        pieces = []
        for r in range(hpg):
            hh = g * hpg + r
            seg = cs[:, hh:hh + 1] - cs_t[hh:hh + 1, :]
            dec = jnp.exp(jnp.where(allowed, seg, -jnp.inf))
            mix = (cb * dec * dt_t[hh:hh + 1, :]).astype(BF16)
            pieces.append(_dot(mix, xb[:, hh * SSM_HEAD_DIM:(hh + 1) * SSM_HEAD_DIM]))
        y_ref[0, :, g * gw:(g + 1) * gw] = (jnp.concatenate(pieces, axis=1)
                                           + y_off * off_scale[:, g * gw:(g + 1) * gw])


def _ssd_scan(xbc, dt2, a_rows, expand, *, n_lat, seq, n_ctx, bsz):
    t = xbc.shape[0]
    lc = SSM_CHUNK
    ctx_chunks, lat_chunks = n_ctx // lc, seq // lc
    gw = SSM_INNER // SSM_GROUPS

    def rowblk(b, d, s):
        in_ctx = s < ctx_chunks
        j_ctx = jnp.where(d == 0, s, ctx_chunks - 1 - s)
        sl = s - ctx_chunks
        j_lat = jnp.where(d == 0, sl, lat_chunks - 1 - sl)
        return jnp.where(in_ctx, (n_lat + b * n_ctx) // lc + j_ctx, b * lat_chunks + j_lat)

    return pl.pallas_call(
        _ssd_kernel,
        grid=(bsz, 2, ctx_chunks + lat_chunks),
        in_specs=[pl.BlockSpec((lc, SSM_INNER), lambda b, d, s: (rowblk(b, d, s), 0)),
                  pl.BlockSpec((lc, 2 * SSM_GROUPS * SSM_STATE),
                               lambda b, d, s: (rowblk(b, d, s), SSM_INNER // (2 * SSM_GROUPS * SSM_STATE))),
                  pl.BlockSpec((1, lc, 128), lambda b, d, s: (d, rowblk(b, d, s), 0)),
                  pl.BlockSpec((1, 1, 128), lambda b, d, s: (d, 0, 0)),
                  pl.BlockSpec((128, SSM_INNER), lambda b, d, s: (0, 0))],
        out_specs=pl.BlockSpec((1, lc, SSM_INNER), lambda b, d, s: (d, rowblk(b, d, s), 0)),
        out_shape=jax.ShapeDtypeStruct((2, t, SSM_INNER), F32),
        scratch_shapes=[pltpu.VMEM((SSM_GROUPS, SSM_STATE, gw), F32)],
        compiler_params=_cparams("arbitrary", "arbitrary", "arbitrary"),
        name="ssd_scan",
    )(xbc, xbc, dt2, a_rows, expand)


def _ssd_finish_kernel(yf_ref, yb_ref, xs_ref, z0_ref, z1_ref, dsk_ref, g_ref, o_ref):
    gw = SSM_INNER // SSM_GROUPS
    zs = (z0_ref, z1_ref)
    for g in range(SSM_GROUPS):
        sl = slice(g * gw, (g + 1) * gw)
        y = yf_ref[0, :, sl] + yb_ref[0, :, sl] + dsk_ref[:, sl] * xs_ref[:, sl]
        y = y * _silu(zs[g][...])
        o_ref[:, sl] = _head_norm(y, g_ref[:, sl]).astype(o_ref.dtype)


def _ssd_finish(ydir, xbc, p, dsk_row, g_row, *, z_col0):
    t = xbc.shape[0]
    tm = ROW_TILE
    gw = SSM_INNER // SSM_GROUPS
    assert z_col0 % gw == 0 and SSM_GROUPS == 2
    zb = z_col0 // gw
    return pl.pallas_call(
        _ssd_finish_kernel,
        grid=(t // tm,),
        in_specs=[pl.BlockSpec((1, tm, SSM_INNER), lambda i: (0, i, 0)),
                  pl.BlockSpec((1, tm, SSM_INNER), lambda i: (1, i, 0)),
                  pl.BlockSpec((tm, SSM_INNER), lambda i: (i, 0)),
                  pl.BlockSpec((tm, gw), lambda i: (i, zb)),
                  pl.BlockSpec((tm, gw), lambda i: (i, zb + 1)),
                  pl.BlockSpec((1, SSM_INNER), lambda i: (0, 0)),
                  pl.BlockSpec((1, SSM_INNER), lambda i: (0, 0))],
        out_specs=pl.BlockSpec((tm, SSM_INNER), lambda i: (i, 0)),
        out_shape=jax.ShapeDtypeStruct((t, SSM_INNER), BF16),
        compiler_params=_cparams("arbitrary"),
        name="ssd_finish",
    )(ydir, ydir, xbc, p, p, dsk_row, g_row)


def _window_kernel(qt_ref, k_ref, vt_ref, sink_ref, o_ref, *, tq, group, n_ctx, seq):
    band = tq + 2 * WINDOW
    pos_len = n_ctx + seq
    q0 = pl.program_id(2) * tq
    start = pl.multiple_of(jnp.minimum(n_ctx + q0 - WINDOW, pos_len - band), 128)
    qt = qt_ref[0, 0, 0]
    n = group * tq
    s_cx = _dot(k_ref[0, 0, 0:n_ctx, :], qt)
    s_w = _dot(k_ref[0, 0, pl.ds(start, band), :], qt)
    kpos = start - n_ctx + lax.broadcasted_iota(I32, (band, n), 0)
    qpos = q0 + lax.broadcasted_iota(I32, (band, n), 1) % tq
    valid = (jnp.abs(qpos - kpos) <= WINDOW) & (kpos >= 0)
    s_w = jnp.where(valid, s_w, -jnp.inf)
    sink = sink_ref[0]
    m = jnp.maximum(jnp.maximum(jnp.max(s_cx, axis=0, keepdims=True), jnp.max(s_w, axis=0, keepdims=True)), sink)
    p_cx = jnp.exp(s_cx - m)
    p_w = jnp.exp(s_w - m)
    l = jnp.sum(p_cx, axis=0, keepdims=True) + jnp.sum(p_w, axis=0, keepdims=True) + jnp.exp(sink - m)
    acc = (_dot(vt_ref[0, 0, 0:HEAD_DIM, 0:n_ctx], p_cx.astype(BF16))
           + _dot(vt_ref[0, 0, 0:HEAD_DIM, pl.ds(start, band)], p_w.astype(BF16)))
    out = acc / l
    for h in range(group):
        o_ref[:, h * HEAD_DIM:(h + 1) * HEAD_DIM] = out[:, h * tq:(h + 1) * tq].T.astype(o_ref.dtype)


def _attention_c(qt, k, vt, sink_rows, *, n_lat, seq, n_ctx, bsz):
    n_kv, group, tq = k.shape[1], qt.shape[-1] // Q_TILE, Q_TILE
    pos_len = n_ctx + seq
    assert n_ctx >= WINDOW and n_ctx % 128 == 0 and pos_len >= tq + 2 * WINDOW
    return pl.pallas_call(
        functools.partial(_window_kernel, tq=tq, group=group, n_ctx=n_ctx, seq=seq),
        grid=(bsz, n_kv, seq // tq),
        in_specs=[pl.BlockSpec((1, 1, 1, HEAD_DIM, group * tq), lambda b, h, i: (b, h, n_ctx // tq + i, 0, 0)),
                  pl.BlockSpec((1, 1, pos_len, HEAD_DIM), lambda b, h, i: (b, h, 0, 0)),
                  pl.BlockSpec((1, 1, VT_ROWS, pos_len), lambda b, h, i: (b, h, 0, 0)),
                  pl.BlockSpec((1, 1, group * tq), lambda b, h, i: (h, 0, 0))],
        out_specs=pl.BlockSpec((tq, group * HEAD_DIM), lambda b, h, i: (b * (seq // tq) + i, h)),
        out_shape=jax.ShapeDtypeStruct((n_lat, n_kv * group * HEAD_DIM), BF16),
        compiler_params=_cparams("arbitrary", "arbitrary", "arbitrary"),
        name="attn_c_window",
    )(qt, k, vt, sink_rows)


def _prep_na(p, gq, gk, *, col0, n_lat, seq, n_ctx, bsz):
    t = p.shape[0]
    tm = ROW_TILE
    hw = NA_HEADS * HEAD_DIM
    assert col0 % hw == 0 or (2 * col0) % hw == 0
    pos_len = n_ctx + seq
    batch_of, pos_of = _pos_maps(tm, n_lat, seq, n_ctx)
    lat_tiles, seq_tiles, ctx_tiles = n_lat // tm, seq // tm, n_ctx // tm

    def qpos_of(i):
        return jnp.where(i < lat_tiles, i % seq_tiles, seq_tiles + (i - lat_tiles) % ctx_tiles)

    cw = hw // 2
    cb = col0 // cw
    shp = jax.ShapeDtypeStruct((bsz, NA_HEADS, pos_len, HEAD_DIM), BF16)
    kv_spec = pl.BlockSpec((1, NA_HEADS, tm, HEAD_DIM), lambda i: (batch_of(i), 0, pos_of(i), 0))
    q_spec = pl.BlockSpec((1, NA_HEADS, tm, HEAD_DIM), lambda i: (batch_of(i), 0, qpos_of(i), 0))

    def kern(q0, q1, k0, k1, v0, v1, gq_ref, gk_ref, q_ref, k_ref, v_ref):
        halves = NA_HEADS // 2
        for h in range(NA_HEADS):
            sl = slice((h % halves) * HEAD_DIM, (h % halves + 1) * HEAD_DIM)
            pq, pk, pv = ((q0, k0, v0) if h < halves else (q1, k1, v1))
            q_ref[0, h] = (_head_norm(pq[:, sl], gq_ref[...]) * (HEAD_DIM ** -0.5)).astype(BF16)
            k_ref[0, h] = _head_norm(pk[:, sl], gk_ref[...]).astype(BF16)
            v_ref[0, h] = pv[:, sl].astype(BF16)

    def col_spec(j):
        return pl.BlockSpec((tm, cw), lambda i: (i, cb + j))

    return pl.pallas_call(
        kern,
        grid=(t // tm,),
        in_specs=[col_spec(j) for j in range(6)] + [pl.BlockSpec((1, HEAD_DIM), lambda i: (0, 0)),
                                                    pl.BlockSpec((1, HEAD_DIM), lambda i: (0, 0))],
        out_specs=[q_spec, kv_spec, kv_spec],
        out_shape=[shp, shp, shp],
        compiler_params=_cparams("arbitrary"),
        name="prep_na",
    )(p, p, p, p, p, p, gq, gk)


def _na_kernel(q_ref, k_ref, v_ref, bias_ref, o_ref, *, n_ctx, rows):
    tq = NA_TILE_ROWS * GRID_W
    win = NA_WIN_ROWS * GRID_W
    ti = pl.program_id(2)
    w0 = jnp.clip(ti * NA_TILE_ROWS - NA_ROWS // 2, 0, rows - NA_WIN_ROWS)
    start = pl.multiple_of(n_ctx + w0 * GRID_W, GRID_W)
    n_tiles = rows // NA_TILE_ROWS
    gq = tq // 2
    sub = (NA_TILE_ROWS // 2 + NA_ROWS) * GRID_W
    shift = (NA_WIN_ROWS * GRID_W - sub)
    for g in range(2):
        off = jnp.where(ti == 0, 0, jnp.where(ti == n_tiles - 1, shift, g * shift))
        off = pl.multiple_of(off, 128)
        ks = pl.multiple_of(start + off, GRID_W)
        q = q_ref[0, 0, g * gq:(g + 1) * gq, :]
        s_cx = _dot_nt(q, k_ref[0, 0, 0:n_ctx, :])
        s_nb = _dot_nt(q, k_ref[0, 0, pl.ds(ks, sub), :]) + bias_ref[0, 0, g * gq:(g + 1) * gq, pl.ds(off, sub)]
        m = jnp.maximum(jnp.max(s_cx, axis=1, keepdims=True), jnp.max(s_nb, axis=1, keepdims=True))
        p_cx = jnp.exp(s_cx - m)
        p_nb = jnp.exp(s_nb - m)
        l = jnp.sum(p_cx, axis=1, keepdims=True) + jnp.sum(p_nb, axis=1, keepdims=True)
        acc = (_dot(p_cx.astype(BF16), v_ref[0, 0, 0:n_ctx, :])
               + _dot(p_nb.astype(BF16), v_ref[0, 0, pl.ds(ks, sub), :]))
        o_ref[g * gq:(g + 1) * gq, :] = (acc / l).astype(o_ref.dtype)


def _na_bias_tables(rpb, rows):
    n_tiles = rows // NA_TILE_ROWS
    tables = []
    for ti in (0, 1, n_tiles - 1):
        w0 = min(max(ti * NA_TILE_ROWS - NA_ROWS // 2, 0), rows - NA_WIN_ROWS)
        qr = ti * NA_TILE_ROWS + jnp.arange(NA_TILE_ROWS)
        r0 = jnp.clip(qr - NA_ROWS // 2, 0, rows - NA_ROWS)
        kr = w0 + jnp.arange(NA_WIN_ROWS)
        row_ok = (kr[None, :] >= r0[:, None]) & (kr[None, :] < r0[:, None] + NA_ROWS)
        dr = jnp.clip(kr[None, :] - qr[:, None] + NA_ROWS - 1, 0, 2 * NA_ROWS - 2)
        qc = jnp.arange(GRID_W)
        c0 = jnp.clip(qc - NA_COLS // 2, 0, GRID_W - NA_COLS)
        kc = jnp.arange(GRID_W)
        col_ok = (kc[None, :] >= c0[:, None]) & (kc[None, :] < c0[:, None] + NA_COLS)
        dc = jnp.clip(kc[None, :] - qc[:, None] + NA_COLS - 1, 0, 2 * NA_COLS - 2)
        sel_r = ((dr[:, :, None] == jnp.arange(2 * NA_ROWS - 1)) & row_ok[:, :, None]).astype(F32)
        sel_c = ((dc[:, :, None] == jnp.arange(2 * NA_COLS - 1)) & col_ok[:, :, None]).astype(F32)
        b = jnp.einsum("rki,hij,cqj->hrckq", sel_r, rpb.astype(F32), sel_c, precision=HI)
        ok = row_ok[:, None, :, None] & col_ok[None, :, None, :]
        b = jnp.where(ok[None], b, -jnp.inf)
        tables.append(b.reshape(rpb.shape[0], NA_TILE_ROWS * GRID_W, NA_WIN_ROWS * GRID_W))
    return jnp.stack(tables, axis=0).astype(F32)


def _attention_d(q, k, v, bias, *, n_lat, seq, n_ctx, bsz):
    rows = seq // GRID_W
    tq = NA_TILE_ROWS * GRID_W
    win = NA_WIN_ROWS * GRID_W
    n_tiles = rows // NA_TILE_ROWS
    pos_len = n_ctx + seq
    assert rows % NA_TILE_ROWS == 0 and rows >= NA_WIN_ROWS and n_tiles >= 3

    def kind(i):
        return jnp.where(i == 0, 0, jnp.where(i == n_tiles - 1, 2, 1))

    return pl.pallas_call(
        functools.partial(_na_kernel, n_ctx=n_ctx, rows=rows),
        grid=(bsz, NA_HEADS, n_tiles),
        in_specs=[pl.BlockSpec((1, 1, tq, HEAD_DIM), lambda b, h, i: (b, h, i, 0)),
                  pl.BlockSpec((1, 1, pos_len, HEAD_DIM), lambda b, h, i: (b, h, 0, 0)),
                  pl.BlockSpec((1, 1, pos_len, HEAD_DIM), lambda b, h, i: (b, h, 0, 0)),
                  pl.BlockSpec((1, 1, tq, win), lambda b, h, i: (kind(i), h, 0, 0))],
        out_specs=pl.BlockSpec((tq, HEAD_DIM), lambda b, h, i: (b * n_tiles + i, h)),
        out_shape=jax.ShapeDtypeStruct((n_lat, NA_HEADS * HEAD_DIM), BF16),
        compiler_params=_cparams("arbitrary", "arbitrary", "arbitrary"),
        name="attn_d_neighbourhood",
    )(q, k, v, bias)


def _pack_bf16_pairs(f):
    half = f.shape[1] // 2
    bits = lax.bitcast_convert_type(f.astype(BF16).astype(F32), U32)
    lo = lax.shift_right_logical(bits[:, :half], jnp.uint32(16))
    hi = bits[:, half:] & jnp.uint32(0xFFFF0000)
    return hi | lo


def _unpack_bf16_pairs(bits):
    lo = lax.bitcast_convert_type(lax.shift_left(bits, jnp.uint32(16)), F32).astype(BF16)
    hi = lax.bitcast_convert_type(bits & jnp.uint32(0xFFFF0000), F32).astype(BF16)
    return lo, hi


def _store_row_tiles(ref, val):
    rows, width = val.shape
    per = width // 128
    for s in range(per):
        ref[pl.ds(s, rows, stride=per), :] = val[:, s * 128:(s + 1) * 128]


def _load_row_tiles(ref, rows, per):
    return jnp.concatenate([ref[pl.ds(s, rows, stride=per), :] for s in range(per)], axis=1)


def _outproj_kernel(x_ref, y1_ref, y2_ref, w_ref, mod_ref, g_ref, wrh_ref, wrl_ref, xo_ref, fpk_ref, lg_ref):
    half = w_ref.shape[0] // 2
    tm = x_ref.shape[0]
    per = fpk_ref.shape[0] // tm
    rows = ROW_TILE
    for r0 in range(0, tm, rows):
        rs = slice(r0, r0 + rows)
        delta = _dot(y1_ref[rs, :], w_ref[0:half, :]) + _dot(y2_ref[rs, :], w_ref[half:, :])
        xn = x_ref[rs, :] + mod_ref[0, 2:3, :] * delta
        xo_ref[rs, :] = xn
        f = _norm_mod(xn, g_ref[...], mod_ref[0, 3:4, :], mod_ref[0, 4:5, :])
        f_hi = f.astype(BF16)
        f_lo = (f - f_hi.astype(F32)).astype(BF16)
        lg_ref[:, rs] = _dot_nt(wrh_ref[...], f_hi) + (_dot_nt(wrl_ref[...], f_hi) + _dot_nt(wrh_ref[...], f_lo))
        _store_row_tiles(fpk_ref.at[pl.ds(r0 * per, rows * per)], _pack_bf16_pairs(f))


def _outproj(x, y1, y2, w_bf16, mod, g, wr_t, *, n_rows, n_lat, seq, bsz):
    d = x.shape[1]
    tm = 2 * ROW_TILE
    hw = y1.shape[1]
    wr_hi = wr_t.astype(BF16)
    wr_lo = (wr_t - wr_hi.astype(F32)).astype(BF16)
    return pl.pallas_call(
        _outproj_kernel,
        grid=(n_rows // tm,),
        in_specs=[pl.BlockSpec((tm, d), lambda i: (i, 0)),
                  pl.BlockSpec((tm, hw), lambda i: (i, 0)),
                  pl.BlockSpec((tm, hw), lambda i: (i, 0)),
                  pl.BlockSpec((2 * hw, d), lambda i: (0, 0)),
                  pl.BlockSpec((1, 6, d), lambda i: (_group_of_tile(i, tm, n_lat, seq, bsz), 0, 0)),
                  pl.BlockSpec((1, d), lambda i: (0, 0)),
                  pl.BlockSpec((N_EXPERTS, d), lambda i: (0, 0)),
                  pl.BlockSpec((N_EXPERTS, d), lambda i: (0, 0))],
        out_specs=[pl.BlockSpec((tm, d), lambda i: (i, 0)),
                   pl.BlockSpec((tm * (d // 256), 128), lambda i: (i, 0)),
                   pl.BlockSpec((N_EXPERTS, tm), lambda i: (0, i))],
        out_shape=[jax.ShapeDtypeStruct((n_rows, d), F32),
                   jax.ShapeDtypeStruct((n_rows * (d // 256), 128), U32),
                   jax.ShapeDtypeStruct((N_EXPERTS, n_rows), F32)],
        compiler_params=_cparams("arbitrary"),
        name="outproj_residual_moe_in",
    )(x, y1, y2, w_bf16, mod, g, wr_hi, wr_lo)


def _route_kernel(lg_ref, b_ref, e_ref, r_ref, w_ref, cnt_ref, run_ref):
    tr = lg_ref.shape[1]
    per = N_EXPERTS // N_GROUPS
    neg = -jnp.inf

    @pl.when(pl.program_id(0) == 0)
    def _():
        run_ref[...] = jnp.zeros(run_ref.shape, F32)

    scores = jax.nn.sigmoid(lg_ref[...])
    sel = scores + b_ref[...]
    sel3 = sel.reshape(N_GROUPS, per, tr)
    mem = lax.broadcasted_iota(I32, (N_GROUPS, per, tr), 1)
    m1 = jnp.max(sel3, axis=1, keepdims=True)
    i1 = jnp.min(jnp.where(sel3 == m1, mem, per), axis=1, keepdims=True)
    m2 = jnp.max(jnp.where(mem == i1, neg, sel3), axis=1, keepdims=True)
    gs = (m1 + m2).reshape(N_GROUPS, tr)
    gid = lax.broadcasted_iota(I32, (N_GROUPS, tr), 0)
    keep = jnp.zeros((N_GROUPS, tr), F32)
    for _ in range(TOPK_GROUPS):
        gm = jnp.max(gs, axis=0, keepdims=True)
        gi = jnp.min(jnp.where(gs == gm, gid, N_GROUPS), axis=0, keepdims=True)
        hit = gid == gi
        keep = jnp.where(hit, 1.0, keep)
        gs = jnp.where(hit, neg, gs)
    keep3 = jnp.broadcast_to(keep.reshape(N_GROUPS, 1, tr), (N_GROUPS, per, tr))
    cand = jnp.where(keep3 > 0.5, sel3, neg).reshape(N_EXPERTS, tr)
    eid = lax.broadcasted_iota(I32, (N_EXPERTS, tr), 0)
    idxs, ws = [], []
    sel_f = jnp.zeros((N_EXPERTS, tr), F32)
    for _ in range(TOP_K):
        cm = jnp.max(cand, axis=0, keepdims=True)
        ci = jnp.min(jnp.where(cand == cm, eid, N_EXPERTS), axis=0, keepdims=True)
        hit = eid == ci
        idxs.append(ci)
        ws.append(jnp.sum(jnp.where(hit, scores, 0.0), axis=0, keepdims=True))
        sel_f = jnp.where(hit, 1.0, sel_f)
        cand = jnp.where(hit, neg, cand)
    before = (lax.broadcasted_iota(I32, (tr, tr), 0) < lax.broadcasted_iota(I32, (tr, tr), 1))
    rank = _dot(sel_f.astype(BF16), jnp.where(before, 1.0, 0.0).astype(BF16)) + run_ref[...]
    run_ref[...] = run_ref[...] + jnp.sum(sel_f, axis=1, keepdims=True)
    cnt_ref[...] = run_ref[...]
    wsum = ws[0]
    for kk in range(1, TOP_K):
        wsum = wsum + ws[kk]
    for kk in range(TOP_K):
        e_ref[kk:kk + 1, :] = idxs[kk]
        r_ref[kk:kk + 1, :] = jnp.sum(jnp.where(eid == idxs[kk], rank, 0.0), axis=0, keepdims=True).astype(I32)
        w_ref[kk:kk + 1, :] = ws[kk] / wsum * ROUTE_SCALE


def _route(logits_t, b_col):
    n_e, t = logits_t.shape
    tr = ROW_TILE
    return pl.pallas_call(
        _route_kernel,
        grid=(t // tr,),
        in_specs=[pl.BlockSpec((n_e, tr), lambda i: (0, i)),
                  pl.BlockSpec((n_e, 1), lambda i: (0, 0))],
        out_specs=[pl.BlockSpec((TOP_K, tr), lambda i: (0, i)),
                   pl.BlockSpec((TOP_K, tr), lambda i: (0, i)),
                   pl.BlockSpec((TOP_K, tr), lambda i: (0, i)),
                   pl.BlockSpec((n_e, 1), lambda i: (0, 0))],
        out_shape=[jax.ShapeDtypeStruct((TOP_K, t), I32),
                   jax.ShapeDtypeStruct((TOP_K, t), I32),
                   jax.ShapeDtypeStruct((TOP_K, t), F32),
                   jax.ShapeDtypeStruct((n_e, 1), F32)],
        scratch_shapes=[pltpu.VMEM((n_e, 1), F32)],
        compiler_params=_cparams("arbitrary"),
        name="moe_route",
    )(logits_t, b_col)


PK_TILES = 8


def _ffn(bits, w1, w3, w2):
    half = bits.shape[1]
    lo, hi = _unpack_bf16_pairs(bits)
    h1 = _dot(lo, w1[0:half, :]) + _dot(hi, w1[half:, :])
    h3 = _dot(lo, w3[0:half, :]) + _dot(hi, w3[half:, :])
    return _dot((_silu(h1) * h3).astype(BF16), w2[...])


def _expert_kernel(be_ref, bv_ref, nu_ref, xs_ref, w1_ref, w3_ref, w2_ref, y_ref, w1b, w3b, w2b):
    i = pl.program_id(0)

    @pl.when(i < nu_ref[0])
    def _():
        e = be_ref[i]
        prev = be_ref[jnp.maximum(i - 1, 0)]

        @pl.when((i == 0) | (e != prev))
        def _():
            w1b[...] = w1_ref[0, 0].astype(BF16)
            w3b[...] = w3_ref[0, 0].astype(BF16)
            w2b[...] = w2_ref[0, 0].astype(BF16)

        tm = MOE_BLOCK
        bits = _load_row_tiles(xs_ref, tm, PK_TILES)
        rows = lax.broadcasted_iota(I32, bits.shape, 0)
        bits = jnp.where(rows < bv_ref[i], bits, jnp.uint32(0))
        _store_row_tiles(y_ref, _pack_bf16_pairs(_ffn(bits, w1b, w3b, w2b)))


def _experts(blk_expert, blk_valid, n_used, xs, w1, w3, w2, layer):
    n_slots = xs.shape[0] // PK_TILES
    d, de = w1.shape[2], w1.shape[3]
    assert d == 2 * PK_TILES * 128
    tm = MOE_BLOCK
    grid_spec = pltpu.PrefetchScalarGridSpec(
        num_scalar_prefetch=3,
        grid=(n_slots // tm,),
        in_specs=[pl.BlockSpec((tm * PK_TILES, 128), lambda i, be, bv, nu: (i, 0)),
                  pl.BlockSpec((1, 1, d, de), lambda i, be, bv, nu: (layer, be[i], 0, 0)),
                  pl.BlockSpec((1, 1, d, de), lambda i, be, bv, nu: (layer, be[i], 0, 0)),
                  pl.BlockSpec((1, 1, de, d), lambda i, be, bv, nu: (layer, be[i], 0, 0))],
        out_specs=pl.BlockSpec((tm * PK_TILES, 128), lambda i, be, bv, nu: (i, 0)),
        scratch_shapes=[pltpu.VMEM((d, de), BF16), pltpu.VMEM((d, de), BF16), pltpu.VMEM((de, d), BF16)],
    )
    return pl.pallas_call(
        _expert_kernel,
        grid_spec=grid_spec,
        out_shape=jax.ShapeDtypeStruct((n_slots * PK_TILES, 128), U32),
        compiler_params=_cparams("arbitrary"),
        name="moe_experts",
    )(blk_expert, blk_valid, n_used, xs, w1, w3, w2)


def _dispatch_shared_kernel(dest_ref, f_ref, w1_ref, w3_ref, w2_ref, xs_ref, y_ref, sem):
    tm = y_ref.shape[0]

    def copy(t, kk):
        src = pl.multiple_of(t * PK_TILES, PK_TILES)
        dst = pl.multiple_of(dest_ref[0, 0, t * TOP_K + kk], PK_TILES)
        return pltpu.make_async_copy(f_ref.at[pl.ds(src, PK_TILES)], xs_ref.at[pl.ds(dst, PK_TILES)], sem)

    def issue(t, c):
        for kk in range(TOP_K):
            copy(t, kk).start()
        return c

    def drain(t, c):
        for kk in range(TOP_K):
            copy(t, kk).wait()
        return c

    lax.fori_loop(0, tm, issue, 0)
    y_ref[...] = _ffn(_load_row_tiles(f_ref, tm, PK_TILES), w1_ref, w3_ref, w2_ref)
    lax.fori_loop(0, tm, drain, 0)


def _dispatch_and_shared(dest3, fpk, w1, w3, w2, n_slots, *, tm):
    t = fpk.shape[0] // PK_TILES
    d, de = w1.shape
    return pl.pallas_call(
        _dispatch_shared_kernel,
        grid=(t // tm,),
        in_specs=[pl.BlockSpec((1, 1, TOP_K * tm), lambda i: (i, 0, 0), memory_space=pltpu.SMEM),
                  pl.BlockSpec((tm * PK_TILES, 128), lambda i: (i, 0)),
                  pl.BlockSpec((d, de), lambda i: (0, 0)),
                  pl.BlockSpec((d, de), lambda i: (0, 0)),
                  pl.BlockSpec((de, d), lambda i: (0, 0))],
        out_specs=[pl.BlockSpec(memory_space=pl.ANY),
                   pl.BlockSpec((tm, d), lambda i: (i, 0))],
        out_shape=[jax.ShapeDtypeStruct((n_slots * PK_TILES, 128), U32),
                   jax.ShapeDtypeStruct((t, d), F32)],
        scratch_shapes=[pltpu.SemaphoreType.DMA(())],
        compiler_params=_cparams("arbitrary"),
        name="moe_dispatch_shared",
    )(dest3, fpk, w1, w3, w2)


def _combine_kernel(dest_ref, dest_next_ref, w_ref, x_ref, ysh_ref, mod_ref, y_ref, o_ref, ybuf, lo_ref, hi_ref, sems):
    tm = x_ref.shape[0]
    i = pl.program_id(0)
    n = pl.num_programs(0)
    slot = i % 2

    def copy(idx_ref, s, t, kk):
        src = pl.multiple_of(idx_ref[0, 0, t * TOP_K + kk], PK_TILES)
        dst = pl.multiple_of(t * PK_TILES, PK_TILES)
        return pltpu.make_async_copy(y_ref.at[pl.ds(src, PK_TILES)], ybuf.at[s, kk, pl.ds(dst, PK_TILES)],
                                     sems.at[s])

    def issue_tile(idx_ref, s):
        def issue(t, c):
            for kk in range(TOP_K):
                copy(idx_ref, s, t, kk).start()
            return c

        lax.fori_loop(0, tm, issue, 0)

    @pl.when(i == 0)
    def _():
        issue_tile(dest_ref, 0)

    @pl.when(i + 1 < n)
    def _():
        issue_tile(dest_next_ref, 1 - slot)

    def drain(t, c):
        for kk in range(TOP_K):
            copy(dest_ref, slot, t, kk).wait()
        return c

    lax.fori_loop(0, tm, drain, 0)

    def token(t, c):
        r = pl.multiple_of(t * PK_TILES, PK_TILES)
        a_lo = a_hi = None
        for kk in range(TOP_K):
            w = w_ref[0, 0, t * TOP_K + kk]
            bits = ybuf[slot, kk, pl.ds(r, PK_TILES), :]
            lo = lax.bitcast_convert_type(lax.shift_left(bits, jnp.uint32(16)), F32)
            hi = lax.bitcast_convert_type(bits & jnp.uint32(0xFFFF0000), F32)
            a_lo = w * lo if a_lo is None else a_lo + w * lo
            a_hi = w * hi if a_hi is None else a_hi + w * hi
        lo_ref[pl.ds(r, PK_TILES), :] = a_lo
        hi_ref[pl.ds(r, PK_TILES), :] = a_hi
        return c

    lax.fori_loop(0, tm, token, 0, unroll=4)
    half = PK_TILES * 128
    for c in range(PK_TILES):
        for base, acc_ref in ((0, lo_ref), (half, hi_ref)):
            sl = slice(base + c * 128, base + (c + 1) * 128)
            routed = acc_ref[pl.ds(c, tm, stride=PK_TILES), :]
            o_ref[:, sl] = x_ref[:, sl] + mod_ref[0, 5:6, sl] * (routed + ysh_ref[:, sl])


def _combine(dest3, x, ysh, w3, mod, y, *, tm, n_lat, seq, bsz):
    t, d = x.shape
    n = t // tm
    return pl.pallas_call(
        _combine_kernel,
        grid=(n,),
        in_specs=[pl.BlockSpec((1, 1, TOP_K * tm), lambda i: (i, 0, 0), memory_space=pltpu.SMEM),
                  pl.BlockSpec((1, 1, TOP_K * tm), lambda i: (jnp.minimum(i + 1, n - 1), 0, 0),
                               memory_space=pltpu.SMEM),
                  pl.BlockSpec((1, 1, TOP_K * tm), lambda i: (i, 0, 0), memory_space=pltpu.SMEM),
                  pl.BlockSpec((tm, d), lambda i: (i, 0)),
                  pl.BlockSpec((tm, d), lambda i: (i, 0)),
                  pl.BlockSpec((1, 6, d), lambda i: (_group_of_tile(i, tm, n_lat, seq, bsz), 0, 0)),
                  pl.BlockSpec(memory_space=pl.ANY)],
        out_specs=pl.BlockSpec((tm, d), lambda i: (i, 0)),
        out_shape=jax.ShapeDtypeStruct((t, d), F32),
        scratch_shapes=[pltpu.VMEM((2, TOP_K, tm * PK_TILES, 128), U32),
                        pltpu.VMEM((tm * PK_TILES, 128), F32), pltpu.VMEM((tm * PK_TILES, 128), F32),
                        pltpu.SemaphoreType.DMA((2,))],
        compiler_params=_cparams("arbitrary"),
        name="moe_combine",
    )(dest3, dest3, w3, x, ysh, mod, y)


def _moe(x_new, fpk, logits_t, mod, b_router, w_e1, w_e3, w_e2, w_s1, w_s3, w_s2, *, layer, n_lat, seq, bsz):
    t = x_new.shape[0]
    top_e, top_r, top_w, counts = _route(logits_t, b_router.reshape(N_EXPERTS, 1))
    counts = counts[:, 0].astype(I32)
    blocks = (counts + MOE_BLOCK - 1) // MOE_BLOCK
    blk_end = jnp.cumsum(blocks)
    blk_start = blk_end - blocks
    n_blocks = -(-(t * TOP_K) // MOE_BLOCK) + N_EXPERTS
    n_slots = n_blocks * MOE_BLOCK
    slot_start = blk_start * MOE_BLOCK
    dest = jnp.sum(jnp.where(top_e[None] == jnp.arange(N_EXPERTS, dtype=I32)[:, None, None],
                             slot_start[:, None, None], 0), axis=0) + top_r
    bi = jnp.arange(n_blocks, dtype=I32)
    blk_expert = jnp.minimum(jnp.sum((bi[:, None] >= blk_end[None, :]).astype(I32), axis=1), N_EXPERTS - 1)
    blk_valid = jnp.clip(counts[blk_expert] - (bi - blk_start[blk_expert]) * MOE_BLOCK, 0, MOE_BLOCK)
    n_used = blk_end[-1:].astype(I32)

    dest_tk = dest.T
    tm_s = 2 * ROW_TILE
    dest_s = (dest_tk * PK_TILES).reshape(t // tm_s, 1, tm_s * TOP_K)
    xs, ysh = _dispatch_and_shared(dest_s, fpk, w_s1.astype(BF16), w_s3.astype(BF16), w_s2.astype(BF16), n_slots,
                                   tm=tm_s)
    y = _experts(blk_expert.astype(I32), blk_valid.astype(I32), n_used, xs, w_e1, w_e3, w_e2, layer)
    tm_c = ROW_TILE
    dest_c = (dest_tk * PK_TILES).reshape(t // tm_c, 1, tm_c * TOP_K)
    w_c = top_w.T.reshape(t // tm_c, 1, tm_c * TOP_K)
    return _combine(dest_c, x_new, ysh, w_c, mod, y, tm=tm_c, n_lat=n_lat, seq=seq, bsz=bsz)


def _pad_cols(w, n):
    return jnp.pad(w, ((0, 0), (0, n - w.shape[1])))


def kernel(x, c, ctx, c_ctx, w_ada, b_ada, g_mix, g_ffn, w_in_even, gq_a, gk_a, conv_w, conv_b, a_log, dt_bias,
           d_skip, g_ssm, w_in_odd, gq_c, gk_c, sink_c, gq_d, gk_d, rpb_d, w_out, w_router, b_router, w_e1, w_e3,
           w_e2, w_s1, w_s3, w_s2):
    bsz, seq, d = x.shape
    n_ctx = ctx.shape[1]
    depth = w_ada.shape[0]
    n_lat = bsz * seq
    geo = dict(n_lat=n_lat, seq=seq, n_ctx=n_ctx, bsz=bsz)
    xs = jnp.concatenate([x.reshape(n_lat, d), ctx.reshape(bsz * n_ctx, d)], axis=0)
    cvec = jnp.zeros((8, d), F32).at[:bsz].set(c).at[bsz].set(c_ctx)
    cos, sin = _rope_tables(seq, n_ctx)
    expand = jnp.zeros((128, SSM_INNER), BF16).at[:SSM_HEADS].set(
        jnp.repeat(jnp.eye(SSM_HEADS, dtype=BF16), SSM_HEAD_DIM, axis=1))

    for layer in range(depth):
        last = layer == depth - 1
        i = layer // 2
        mod = _adaln(cvec, w_ada, b_ada[layer].reshape(1, -1), layer)[:bsz + 1].reshape(bsz + 1, 6, d)
        g1 = g_mix[layer].reshape(1, d)
        if layer % 2 == 0:
            n_in = w_in_even.shape[2]
            n_pad = -(-n_in // 384) * 384
            p = _norm_mod_matmul(xs, g1, mod, _pad_cols(w_in_even[i], n_pad).astype(BF16),
                                 n_lat=n_lat, seq=seq, bsz=bsz, tn=n_pad // 3)
            qt, k, vt = _prep_rope(p, cos, sin, gq_a[i].reshape(1, -1), gk_a[i].reshape(1, -1),
                                   q_scale=HEAD_DIM ** -0.5 * LOG2E, n_q=A_HEADS, n_kv=A_KV_HEADS, **geo)
            y1 = _attention_a(qt, k, vt, need_ctx=not last, **geo)
            xbc = _ssm_conv(p, conv_w[i], conv_b[i].reshape(1, -1), col0=A_IN + SSM_INNER,
                            n_lat=n_lat, seq=seq, n_ctx=n_ctx)
            dt_row = jnp.zeros((1, 128), F32).at[0, :2 * SSM_HEADS].set(dt_bias[i].reshape(-1))
            dt2 = _ssm_dt(p, dt_row, col0=A_IN + SSM_INNER + SSM_CONV_DIM)
            a_rows = jnp.zeros((2, 1, 128), F32).at[:, 0, :SSM_HEADS].set(-jnp.exp(a_log[i]))
            ydir = _ssd_scan(xbc, dt2, a_rows, expand, **geo)
            y2 = _ssd_finish(ydir, xbc, p, jnp.repeat(d_skip[i], SSM_HEAD_DIM).reshape(1, -1),
                             g_ssm[i].reshape(1, -1), z_col0=A_IN)
        else:
            p = _norm_mod_matmul(xs, g1, mod, w_in_odd[i].astype(BF16), n_lat=n_lat, seq=seq, bsz=bsz,
                                 tn=w_in_odd.shape[2] // 3)
            qt, k, vt = _prep_rope(p, cos, sin, gq_c[i].reshape(1, -1), gk_c[i].reshape(1, -1),
                                   q_scale=HEAD_DIM ** -0.5, n_q=A_HEADS, n_kv=A_KV_HEADS, **geo)
            group = A_HEADS // A_KV_HEADS
            sink_rows = jnp.repeat(sink_c[i].reshape(A_KV_HEADS, group), Q_TILE, axis=1).reshape(
                A_KV_HEADS, 1, group * Q_TILE)
            y1 = _attention_c(qt, k, vt, sink_rows, **geo)
            qd, kd, vd = _prep_na(p, gq_d[i].reshape(1, -1), gk_d[i].reshape(1, -1), col0=A_IN, **geo)
            y2 = _attention_d(qd, kd, vd, _na_bias_tables(rpb_d[i], seq // GRID_W), **geo)
        n_rows = n_lat if last else xs.shape[0]
        x_new, fpk, logits_t = _outproj(xs, y1, y2, w_out[layer].astype(BF16), mod, g_ffn[layer].reshape(1, d),
                                        w_router[layer].T, n_rows=n_rows, n_lat=n_lat, seq=seq, bsz=bsz)
        xs = _moe(x_new, fpk, logits_t, mod, b_router[layer], w_e1, w_e3, w_e2,
                  w_s1[layer], w_s3[layer], w_s2[layer], layer=layer, n_lat=n_lat, seq=seq, bsz=bsz)
    return xs[:n_lat].reshape(bsz, seq, d)
```

```python
import functools
import math

import jax
import jax.numpy as jnp
from jax import lax
from jax.experimental import pallas as pl
from jax.experimental.pallas import tpu as pltpu

F32 = jnp.float32
BF16 = jnp.bfloat16
I32 = jnp.int32
U32 = jnp.uint32

EPS = 1e-6
HEAD_DIM = 128
GRID_W = 64
ROPE_THETA = 10000.0
WINDOW = 128
NA_ROWS = 8
NA_COLS = 16
NA_TILE_ROWS = 8
NA_WIN_ROWS = 16
A_HEADS = 8
A_KV_HEADS = 2
SSM_HEADS = 16
SSM_HEAD_DIM = 64
SSM_INNER = SSM_HEADS * SSM_HEAD_DIM
SSM_GROUPS = 2
SSM_STATE = 128
SSM_CONV = 5
SSM_CONV_DIM = SSM_INNER + 2 * SSM_GROUPS * SSM_STATE
SSM_CHUNK = 128
A_IN = (A_HEADS + 2 * A_KV_HEADS) * HEAD_DIM
NA_HEADS = 8
N_EXPERTS = 64
TOP_K = 8
N_GROUPS = 8
TOPK_GROUPS = 4
ROUTE_SCALE = 2.5
MOE_BLOCK = 512

VMEM_LIMIT_BYTES = 56 * 1024 * 1024
ROW_TILE = 256
Q_TILE = 256
VT_ROWS = HEAD_DIM + 16
LOG2E = 1.4426950408889634
HI = lax.Precision.HIGHEST


def _cparams(*sem):
    return pltpu.CompilerParams(dimension_semantics=sem, vmem_limit_bytes=VMEM_LIMIT_BYTES)


def _silu(x):
    return x * jax.nn.sigmoid(x)


def _hi_dot(a, b):
    return jnp.dot(a, b, precision=HI, preferred_element_type=F32)


def _dot(a, b):
    return jnp.dot(a, b, preferred_element_type=F32)


def _dot_nt(a, b):
    return lax.dot_general(a, b, (((1,), (1,)), ((), ())), preferred_element_type=F32)


def _largest_tile(n, cap, mult=128):
    best = None
    for t in range(mult, cap + 1, mult):
        if n % t == 0:
            best = t
    assert best is not None, (n, cap)
    return best


def _adaln_kernel(c_ref, w_ref, b_ref, o_ref):
    o_ref[...] = _hi_dot(_silu(c_ref[...]), w_ref[0]) + b_ref[...]


def _adaln(cvec, w, b, layer):
    _, d, n = w.shape
    tn = 1024
    return pl.pallas_call(
        _adaln_kernel,
        grid=(n // tn,),
        in_specs=[pl.BlockSpec((8, d), lambda j: (0, 0)),
                  pl.BlockSpec((1, d, tn), lambda j: (layer, 0, j)),
                  pl.BlockSpec((1, tn), lambda j: (0, j))],
        out_specs=pl.BlockSpec((8, tn), lambda j: (0, j)),
        out_shape=jax.ShapeDtypeStruct((8, n), F32),
        compiler_params=_cparams("arbitrary"),
        name="adaln",
    )(cvec, w, b)


def _norm_mod(x, g, shift, scale):
    ms = jnp.mean(x * x, axis=-1, keepdims=True)
    return (x * lax.rsqrt(ms + EPS) * g) * (1.0 + scale) + shift


def _nmm_kernel(x_ref, g_ref, mod_ref, w_ref, o_ref, xn_ref):
    i, j = pl.program_id(0), pl.program_id(1)
    last_j = pl.num_programs(1) - 1
    slot = i % 2

    def normalised():
        return _norm_mod(x_ref[...], g_ref[...], mod_ref[0, 0:1, :], mod_ref[0, 1:2, :]).astype(BF16)

    @pl.when((i == 0) & (j == 0))
    def _():
        xn_ref[0] = normalised()

    @pl.when(j < last_j)
    def _():
        o_ref[...] = _dot(xn_ref[slot], w_ref[...])

    @pl.when(j == last_j)
    def _():
        o_ref[...] = _dot(xn_ref[slot], w_ref[...])
        xn_ref[1 - slot] = normalised()


def _group_of_tile(i, tm, n_lat, seq, bsz):
    return jnp.where(i * tm < n_lat, (i * tm) // seq, bsz)


def _norm_mod_matmul(x, g, mod, w_bf16, *, n_lat, seq, bsz, tn):
    t, d = x.shape
    n = w_bf16.shape[1]
    tm = 2 * ROW_TILE
    assert t % tm == 0 and n % tn == 0 and n_lat % tm == 0
    n_i, n_j = t // tm, n // tn
    assert n_j >= 2

    def row_tile(i, j):
        return jnp.minimum(i + (j == n_j - 1).astype(jnp.int32), n_i - 1)

    return pl.pallas_call(
        _nmm_kernel,
        grid=(n_i, n_j),
        in_specs=[pl.BlockSpec((tm, d), lambda i, j: (row_tile(i, j), 0)),
                  pl.BlockSpec((1, d), lambda i, j: (0, 0)),
                  pl.BlockSpec((1, 6, d), lambda i, j: (_group_of_tile(row_tile(i, j), tm, n_lat, seq, bsz), 0, 0)),
                  pl.BlockSpec((d, tn), lambda i, j: (0, j))],
        out_specs=pl.BlockSpec((tm, tn), lambda i, j: (i, j)),
        out_shape=jax.ShapeDtypeStruct((t, n), F32),
        scratch_shapes=[pltpu.VMEM((2, tm, d), BF16)],
        compiler_params=_cparams("arbitrary", "arbitrary"),
        name="norm_mod_inproj",
    )(x, g, mod, w_bf16)


def _head_norm(x, g):
    ms = jnp.mean(x * x, axis=-1, keepdims=True)
    return x * lax.rsqrt(ms + EPS) * g


def _prep_rope_kernel(p_ref, cos_ref, sin_ref, gq_ref, gk_ref, qt_ref, k_ref, vt_ref, *, n_q, n_kv, tq, q_scale):
    cos = cos_ref[...]
    sin = sin_ref[...]
    lane = lax.broadcasted_iota(I32, cos.shape, 1)
    first_half = (lane % 64) < 32

    def norm_rope(x, g):
        y = _head_norm(x, g)
        swapped = jnp.where(first_half, pltpu.roll(y, 96, 1), pltpu.roll(y, 32, 1))
        return y * cos + swapped * sin

    group = n_q // n_kv
    for h in range(n_q):
        q = norm_rope(p_ref[:, h * HEAD_DIM:(h + 1) * HEAD_DIM], gq_ref[...]) * q_scale
        kvh, hh = divmod(h, group)
        qt_ref[0, kvh, 0, :, hh * tq:(hh + 1) * tq] = q.T.astype(BF16)
    for h in range(n_kv):
        c0 = (n_q + h) * HEAD_DIM
        k_ref[0, h] = norm_rope(p_ref[:, c0:c0 + HEAD_DIM], gk_ref[...]).astype(BF16)
        c1 = (n_q + n_kv + h) * HEAD_DIM
        vt_ref[0, h, 0:HEAD_DIM, :] = p_ref[:, c1:c1 + HEAD_DIM].T.astype(BF16)
        vt_ref[0, h, HEAD_DIM:VT_ROWS, :] = jnp.ones((VT_ROWS - HEAD_DIM, tq), BF16)


def _pos_maps(tm, n_lat, seq, n_ctx):
    lat_tiles, seq_tiles, ctx_tiles = n_lat // tm, seq // tm, n_ctx // tm

    def batch_of(i):
        return jnp.where(i < lat_tiles, i // seq_tiles, (i - lat_tiles) // ctx_tiles)

    def pos_of(i):
        return jnp.where(i < lat_tiles, ctx_tiles + i % seq_tiles, (i - lat_tiles) % ctx_tiles)

    return batch_of, pos_of


def _prep_rope(p, cos, sin, gq, gk, *, q_scale, n_q, n_kv, n_lat, seq, n_ctx, bsz):
    t = p.shape[0]
    tm = Q_TILE
    pos_len = n_ctx + seq
    group = n_q // n_kv
    width = (n_q + 2 * n_kv) * HEAD_DIM
    batch_of, pos_of = _pos_maps(tm, n_lat, seq, n_ctx)
    kern = functools.partial(_prep_rope_kernel, n_q=n_q, n_kv=n_kv, tq=tm, q_scale=q_scale)
    return pl.pallas_call(
        kern,
        grid=(t // tm,),
        in_specs=[pl.BlockSpec((tm, width), lambda i: (i, 0)),
                  pl.BlockSpec((tm, HEAD_DIM), lambda i: (pos_of(i), 0)),
                  pl.BlockSpec((tm, HEAD_DIM), lambda i: (pos_of(i), 0)),
                  pl.BlockSpec((1, HEAD_DIM), lambda i: (0, 0)),
                  pl.BlockSpec((1, HEAD_DIM), lambda i: (0, 0))],
        out_specs=[pl.BlockSpec((1, n_kv, 1, HEAD_DIM, group * tm), lambda i: (batch_of(i), 0, pos_of(i), 0, 0)),
                   pl.BlockSpec((1, n_kv, tm, HEAD_DIM), lambda i: (batch_of(i), 0, pos_of(i), 0)),
                   pl.BlockSpec((1, n_kv, VT_ROWS, tm), lambda i: (batch_of(i), 0, 0, pos_of(i)))],
        out_shape=[jax.ShapeDtypeStruct((bsz, n_kv, pos_len // tm, HEAD_DIM, group * tm), BF16),
                   jax.ShapeDtypeStruct((bsz, n_kv, pos_len, HEAD_DIM), BF16),
                   jax.ShapeDtypeStruct((bsz, n_kv, VT_ROWS, pos_len), BF16)],
        compiler_params=_cparams("arbitrary"),
        name="prep_rope",
    )(p, cos, sin, gq, gk)


def _rope_tables(seq, n_ctx):
    t = jnp.arange(seq, dtype=I32)
    row = (t // GRID_W).astype(F32)
    col = (t % GRID_W).astype(F32)
    n_freq = HEAD_DIM // 4
    inv = ROPE_THETA ** (-jnp.arange(n_freq, dtype=F32) / n_freq)
    ar, ac = row[:, None] * inv, col[:, None] * inv
    cos = jnp.concatenate([jnp.cos(ar), jnp.cos(ar), jnp.cos(ac), jnp.cos(ac)], axis=1)
    sin = jnp.concatenate([-jnp.sin(ar), jnp.sin(ar), -jnp.sin(ac), jnp.sin(ac)], axis=1)
    cos = jnp.concatenate([jnp.ones((n_ctx, HEAD_DIM), F32), cos], axis=0)
    sin = jnp.concatenate([jnp.zeros((n_ctx, HEAD_DIM), F32), sin], axis=0)
    return cos, sin


def _flash_t_kernel(qt_ref, k_ref, vt_ref, o_ref, m_ref, l_ref, acc_ref, sa_ref, sb_ref, *, tk, n_k, tq, group):
    qt = qt_ref[0, 0, 0]
    m_ref[...] = jnp.full(m_ref.shape, -jnp.inf, F32)
    l_ref[...] = jnp.zeros(l_ref.shape, F32)
    acc_ref[...] = jnp.zeros(acc_ref.shape, F32)

    def scores(j, dst):
        off = pl.multiple_of(jnp.minimum(j, n_k - 1) * tk, tk)
        dst[...] = _dot(k_ref[0, 0, pl.ds(off, tk), :], qt)

    def consume(j, src):
        off = pl.multiple_of(j * tk, tk)
        s = src[...]
        m_old = m_ref[...]
        m_new = jnp.maximum(m_old, jnp.max(s, axis=0, keepdims=True))
        alpha = jnp.exp2(m_old - m_new)
        p = jnp.exp2(s - m_new).astype(BF16)
        r = _dot(vt_ref[0, 0, :, pl.ds(off, tk)], p)
        acc_ref[...] = alpha * acc_ref[...] + r[0:HEAD_DIM]
        l_ref[...] = alpha * l_ref[...] + r[HEAD_DIM:HEAD_DIM + 1]
        m_ref[...] = m_new

    scores(0, sa_ref)

    def body(jj, carry):
        j = 2 * jj
        scores(j + 1, sb_ref)
        consume(j, sa_ref)
        scores(j + 2, sa_ref)
        consume(j + 1, sb_ref)
        return carry

    lax.fori_loop(0, n_k // 2, body, 0)
    if n_k % 2:
        consume(n_k - 1, sa_ref)
    out = acc_ref[...] / l_ref[...]
    for h in range(group):
        o_ref[:, h * HEAD_DIM:(h + 1) * HEAD_DIM] = out[:, h * tq:(h + 1) * tq].T.astype(o_ref.dtype)


def _attention_a(qt, k, vt, *, n_lat, seq, n_ctx, bsz, need_ctx):
    n_kv, group, tq = k.shape[1], qt.shape[-1] // Q_TILE, Q_TILE
    pos_len = n_ctx + seq
    width = n_kv * group * HEAD_DIM
    tk = _largest_tile(pos_len, 1280)

    def scratch(tkk):
        n = group * tq
        return [pltpu.VMEM((1, n), F32), pltpu.VMEM((1, n), F32), pltpu.VMEM((HEAD_DIM, n), F32),
                pltpu.VMEM((tkk, n), F32), pltpu.VMEM((tkk, n), F32)]

    y = pl.pallas_call(
        functools.partial(_flash_t_kernel, tk=tk, n_k=pos_len // tk, tq=tq, group=group),
        grid=(bsz, n_kv, seq // tq),
        in_specs=[pl.BlockSpec((1, 1, 1, HEAD_DIM, group * tq), lambda b, h, i: (b, h, n_ctx // tq + i, 0, 0)),
                  pl.BlockSpec((1, 1, pos_len, HEAD_DIM), lambda b, h, i: (b, h, 0, 0)),
                  pl.BlockSpec((1, 1, VT_ROWS, pos_len), lambda b, h, i: (b, h, 0, 0))],
        out_specs=pl.BlockSpec((tq, group * HEAD_DIM), lambda b, h, i: (b * (seq // tq) + i, h)),
        out_shape=jax.ShapeDtypeStruct((n_lat, width), BF16),
        scratch_shapes=scratch(tk),
        compiler_params=_cparams("arbitrary", "arbitrary", "arbitrary"),
        name="attn_a_latent",
    )(qt, k, vt)
    if not need_ctx:
        return y
    tkc = _largest_tile(n_ctx, 768)
    y_ctx = pl.pallas_call(
        functools.partial(_flash_t_kernel, tk=tkc, n_k=n_ctx // tkc, tq=tq, group=group),
        grid=(bsz, n_kv, n_ctx // tq),
        in_specs=[pl.BlockSpec((1, 1, 1, HEAD_DIM, group * tq), lambda b, h, i: (b, h, i, 0, 0)),
                  pl.BlockSpec((1, 1, n_ctx, HEAD_DIM), lambda b, h, i: (b, h, 0, 0)),
                  pl.BlockSpec((1, 1, VT_ROWS, n_ctx), lambda b, h, i: (b, h, 0, 0))],
        out_specs=pl.BlockSpec((tq, group * HEAD_DIM), lambda b, h, i: (b * (n_ctx // tq) + i, h)),
        out_shape=jax.ShapeDtypeStruct((bsz * n_ctx, width), BF16),
        scratch_shapes=scratch(tkc),
        compiler_params=_cparams("arbitrary", "arbitrary", "arbitrary"),
        name="attn_a_context",
    )(qt, k, vt)
    return jnp.concatenate([y, y_ctx], axis=0)


def _segment_edges(i, tm, n_lat, seq, n_ctx):
    lat_tiles, seq_tiles, ctx_tiles = n_lat // tm, seq // tm, n_ctx // tm
    is_lat = i < lat_tiles
    pos = jnp.where(is_lat, i % seq_tiles, (i - lat_tiles) % ctx_tiles)
    last = jnp.where(is_lat, seq_tiles, ctx_tiles) - 1
    return pos == 0, pos == last


def _conv_kernel(cur_ref, prev_ref, next_ref, w_ref, b_ref, o_ref, ext_ref, *, tm, n_lat, seq, n_ctx):
    first, last = _segment_edges(pl.program_id(0), tm, n_lat, seq, n_ctx)
    ext_ref[0:8, :] = jnp.where(first, 0.0, prev_ref[...])
    ext_ref[8:8 + tm, :] = cur_ref[...]
    ext_ref[8 + tm:16 + tm, :] = jnp.where(last, 0.0, next_ref[...])
    acc = jnp.broadcast_to(b_ref[...], (tm, b_ref.shape[1]))
    half = SSM_CONV // 2
    for kk in range(SSM_CONV):
        acc = acc + w_ref[kk:kk + 1, :] * ext_ref[pl.ds(8 - half + kk, tm), :]
    o_ref[...] = _silu(acc)


def _ssm_conv(p, conv_w, conv_b, *, col0, n_lat, seq, n_ctx):
    t = p.shape[0]
    tm, tc = ROW_TILE, 512
    assert col0 % tc == 0 and SSM_CONV_DIM % tc == 0
    cb0 = col0 // tc
    r8 = tm // 8
    kern = functools.partial(_conv_kernel, tm=tm, n_lat=n_lat, seq=seq, n_ctx=n_ctx)
    return pl.pallas_call(
        kern,
        grid=(t // tm, SSM_CONV_DIM // tc),
        in_specs=[pl.BlockSpec((tm, tc), lambda i, j: (i, cb0 + j)),
                  pl.BlockSpec((8, tc), lambda i, j: (jnp.maximum(i * r8 - 1, 0), cb0 + j)),
                  pl.BlockSpec((8, tc), lambda i, j: (jnp.minimum((i + 1) * r8, t // 8 - 1), cb0 + j)),
                  pl.BlockSpec((SSM_CONV, tc), lambda i, j: (0, j)),
                  pl.BlockSpec((1, tc), lambda i, j: (0, j))],
        out_specs=pl.BlockSpec((tm, tc), lambda i, j: (i, j)),
        out_shape=jax.ShapeDtypeStruct((t, SSM_CONV_DIM), F32),
        scratch_shapes=[pltpu.VMEM((tm + 16, tc), F32)],
        compiler_params=_cparams("arbitrary", "arbitrary"),
        name="ssm_conv",
    )(p, p, p, conv_w, conv_b)


def _dt_kernel(p_ref, b_ref, o_ref):
    x = p_ref[...] + b_ref[...]
    sp = jnp.maximum(x, 0.0) + jnp.log1p(jnp.exp(-jnp.abs(x)))
    o_ref[0] = sp
    o_ref[1] = pltpu.roll(sp, HEAD_DIM - SSM_HEADS, 1)


def _ssm_dt(p, bias_row, *, col0):
    t = p.shape[0]
    tm = 2 * ROW_TILE
    assert col0 % 128 == 0
    return pl.pallas_call(
        _dt_kernel,
        grid=(t // tm,),
        in_specs=[pl.BlockSpec((tm, 128), lambda i: (i, col0 // 128)),
                  pl.BlockSpec((1, 128), lambda i: (0, 0))],
        out_specs=pl.BlockSpec((2, tm, 128), lambda i: (0, i, 0)),
        out_shape=jax.ShapeDtypeStruct((2, t, 128), F32),
        compiler_params=_cparams("arbitrary"),
        name="ssm_dt",
    )(p, bias_row)


def _ssd_kernel(x_ref, bc_ref, dt_ref, a_ref, e_ref, y_ref, h_ref):
    d = pl.program_id(1)
    lc = SSM_CHUNK
    gw = SSM_INNER // SSM_GROUPS
    hpg = SSM_HEADS // SSM_GROUPS

    @pl.when(pl.program_id(2) == 0)
    def _():
        h_ref[...] = jnp.zeros(h_ref.shape, F32)

    dt = dt_ref[0]
    w = dt * a_ref[0]
    row = lax.broadcasted_iota(I32, (lc, lc), 0)
    col = lax.broadcasted_iota(I32, (lc, lc), 1)
    fwd = d == 0
    sgn = jnp.where(fwd, 1, -1)
    allowed = (row - col) * sgn >= 0
    allowed_t = (col - row) * sgn >= 0
    cs = _hi_dot(allowed.astype(F32), w)
    cs_t = _hi_dot(w.T, allowed_t.astype(F32))
    dt_t = dt.T
    tot = jnp.where(fwd, cs[lc - 1:lc, :], cs[0:1, :])
    e = e_ref[...]

    def expand(a):
        hi = a.astype(BF16)
        lo = (a - hi.astype(F32)).astype(BF16)
        return _dot(hi, e) + _dot(lo, e)

    x = x_ref[...]
    xw = (x * expand(jnp.exp(tot - cs) * dt)).astype(BF16)
    off_scale = expand(jnp.exp(cs))
    state_decay = expand(jnp.exp(jnp.broadcast_to(tot, (16, 128))))[0:1, :]
    xb = x.astype(BF16)
    for g in range(SSM_GROUPS):
        bm = bc_ref[:, g * SSM_STATE:(g + 1) * SSM_STATE]
        cm = bc_ref[:, (SSM_GROUPS + g) * SSM_STATE:(SSM_GROUPS + g + 1) * SSM_STATE].astype(BF16)
        cb = _dot_nt(cm, bm.astype(BF16))
        h_prev = h_ref[g]
        y_off = _dot(cm, h_prev.astype(BF16))
        s_chunk = _dot(bm.T.astype(BF16), xw[:, g * gw:(g + 1) * gw])
        h_ref[g] = state_decay[:, g * gw:(g + 1) * gw] * h_prev + s_chunk
        pieces = []
        for r in range(hpg):
            hh = g * hpg + r
            seg = cs[:, hh:hh + 1] - cs_t[hh:hh + 1, :]
            dec = jnp.exp(jnp.where(allowed, seg, -jnp.inf))
            mix = (cb * dec * dt_t[hh:hh + 1, :]).astype(BF16)
            pieces.append(_dot(mix, xb[:, hh * SSM_HEAD_DIM:(hh + 1) * SSM_HEAD_DIM]))
        y_ref[0, :, g * gw:(g + 1) * gw] = (jnp.concatenate(pieces, axis=1)
                                           + y_off * off_scale[:, g * gw:(g + 1) * gw])


def _ssd_scan(xbc, dt2, a_rows, expand, *, n_lat, seq, n_ctx, bsz):
    t = xbc.shape[0]
    lc = SSM_CHUNK
    ctx_chunks, lat_chunks = n_ctx // lc, seq // lc
    gw = SSM_INNER // SSM_GROUPS

    def rowblk(b, d, s):
        in_ctx = s < ctx_chunks
        j_ctx = jnp.where(d == 0, s, ctx_chunks - 1 - s)
        sl = s - ctx_chunks
        j_lat = jnp.where(d == 0, sl, lat_chunks - 1 - sl)
        return jnp.where(in_ctx, (n_lat + b * n_ctx) // lc + j_ctx, b * lat_chunks + j_lat)

    return pl.pallas_call(
        _ssd_kernel,
        grid=(bsz, 2, ctx_chunks + lat_chunks),
        in_specs=[pl.BlockSpec((lc, SSM_INNER), lambda b, d, s: (rowblk(b, d, s), 0)),
                  pl.BlockSpec((lc, 2 * SSM_GROUPS * SSM_STATE),
                               lambda b, d, s: (rowblk(b, d, s), SSM_INNER // (2 * SSM_GROUPS * SSM_STATE))),
                  pl.BlockSpec((1, lc, 128), lambda b, d, s: (d, rowblk(b, d, s), 0)),
                  pl.BlockSpec((1, 1, 128), lambda b, d, s: (d, 0, 0)),
                  pl.BlockSpec((128, SSM_INNER), lambda b, d, s: (0, 0))],
        out_specs=pl.BlockSpec((1, lc, SSM_INNER), lambda b, d, s: (d, rowblk(b, d, s), 0)),
        out_shape=jax.ShapeDtypeStruct((2, t, SSM_INNER), F32),
        scratch_shapes=[pltpu.VMEM((SSM_GROUPS, SSM_STATE, gw), F32)],
        compiler_params=_cparams("arbitrary", "arbitrary", "arbitrary"),
        name="ssd_scan",
    )(xbc, xbc, dt2, a_rows, expand)


def _ssd_finish_kernel(yf_ref, yb_ref, xs_ref, z0_ref, z1_ref, dsk_ref, g_ref, o_ref):
    gw = SSM_INNER // SSM_GROUPS
    zs = (z0_ref, z1_ref)
    for g in range(SSM_GROUPS):
        sl = slice(g * gw, (g + 1) * gw)
        y = yf_ref[0, :, sl] + yb_ref[0, :, sl] + dsk_ref[:, sl] * xs_ref[:, sl]
        y = y * _silu(zs[g][...])
        o_ref[:, sl] = _head_norm(y, g_ref[:, sl]).astype(o_ref.dtype)


def _ssd_finish(ydir, xbc, p, dsk_row, g_row, *, z_col0):
    t = xbc.shape[0]
    tm = ROW_TILE
    gw = SSM_INNER // SSM_GROUPS
    assert z_col0 % gw == 0 and SSM_GROUPS == 2
    zb = z_col0 // gw
    return pl.pallas_call(
        _ssd_finish_kernel,
        grid=(t // tm,),
        in_specs=[pl.BlockSpec((1, tm, SSM_INNER), lambda i: (0, i, 0)),
                  pl.BlockSpec((1, tm, SSM_INNER), lambda i: (1, i, 0)),
                  pl.BlockSpec((tm, SSM_INNER), lambda i: (i, 0)),
                  pl.BlockSpec((tm, gw), lambda i: (i, zb)),
                  pl.BlockSpec((tm, gw), lambda i: (i, zb + 1)),
                  pl.BlockSpec((1, SSM_INNER), lambda i: (0, 0)),
                  pl.BlockSpec((1, SSM_INNER), lambda i: (0, 0))],
        out_specs=pl.BlockSpec((tm, SSM_INNER), lambda i: (i, 0)),
        out_shape=jax.ShapeDtypeStruct((t, SSM_INNER), BF16),
        compiler_params=_cparams("arbitrary"),
        name="ssd_finish",
    )(ydir, ydir, xbc, p, p, dsk_row, g_row)


def _window_kernel(qt_ref, k_ref, vt_ref, sink_ref, o_ref, *, tq, group, n_ctx, seq):
    band = tq + 2 * WINDOW
    pos_len = n_ctx + seq
    q0 = pl.program_id(2) * tq
    start = pl.multiple_of(jnp.minimum(n_ctx + q0 - WINDOW, pos_len - band), 128)
    qt = qt_ref[0, 0, 0]
    n = group * tq
    s_cx = _dot(k_ref[0, 0, 0:n_ctx, :], qt)
    s_w = _dot(k_ref[0, 0, pl.ds(start, band), :], qt)
    kpos = start - n_ctx + lax.broadcasted_iota(I32, (band, n), 0)
    qpos = q0 + lax.broadcasted_iota(I32, (band, n), 1) % tq
    valid = (jnp.abs(qpos - kpos) <= WINDOW) & (kpos >= 0)
    s_w = jnp.where(valid, s_w, -jnp.inf)
    sink = sink_ref[0]
    m = jnp.maximum(jnp.maximum(jnp.max(s_cx, axis=0, keepdims=True), jnp.max(s_w, axis=0, keepdims=True)), sink)
    p_cx = jnp.exp(s_cx - m)
    p_w = jnp.exp(s_w - m)
    l = jnp.sum(p_cx, axis=0, keepdims=True) + jnp.sum(p_w, axis=0, keepdims=True) + jnp.exp(sink - m)
    acc = (_dot(vt_ref[0, 0, 0:HEAD_DIM, 0:n_ctx], p_cx.astype(BF16))
           + _dot(vt_ref[0, 0, 0:HEAD_DIM, pl.ds(start, band)], p_w.astype(BF16)))
    out = acc / l
    for h in range(group):
        o_ref[:, h * HEAD_DIM:(h + 1) * HEAD_DIM] = out[:, h * tq:(h + 1) * tq].T.astype(o_ref.dtype)


def _attention_c(qt, k, vt, sink_rows, *, n_lat, seq, n_ctx, bsz):
    n_kv, group, tq = k.shape[1], qt.shape[-1] // Q_TILE, Q_TILE
    pos_len = n_ctx + seq
    assert n_ctx >= WINDOW and n_ctx % 128 == 0 and pos_len >= tq + 2 * WINDOW
    return pl.pallas_call(
        functools.partial(_window_kernel, tq=tq, group=group, n_ctx=n_ctx, seq=seq),
        grid=(bsz, n_kv, seq // tq),
        in_specs=[pl.BlockSpec((1, 1, 1, HEAD_DIM, group * tq), lambda b, h, i: (b, h, n_ctx // tq + i, 0, 0)),
                  pl.BlockSpec((1, 1, pos_len, HEAD_DIM), lambda b, h, i: (b, h, 0, 0)),
                  pl.BlockSpec((1, 1, VT_ROWS, pos_len), lambda b, h, i: (b, h, 0, 0)),
                  pl.BlockSpec((1, 1, group * tq), lambda b, h, i: (h, 0, 0))],
        out_specs=pl.BlockSpec((tq, group * HEAD_DIM), lambda b, h, i: (b * (seq // tq) + i, h)),
        out_shape=jax.ShapeDtypeStruct((n_lat, n_kv * group * HEAD_DIM), BF16),
        compiler_params=_cparams("arbitrary", "arbitrary", "arbitrary"),
        name="attn_c_window",
    )(qt, k, vt, sink_rows)


def _prep_na(p, gq, gk, *, col0, n_lat, seq, n_ctx, bsz):
    t = p.shape[0]
    tm = ROW_TILE
    hw = NA_HEADS * HEAD_DIM
    assert col0 % hw == 0 or (2 * col0) % hw == 0
    pos_len = n_ctx + seq
    batch_of, pos_of = _pos_maps(tm, n_lat, seq, n_ctx)
    lat_tiles, seq_tiles, ctx_tiles = n_lat // tm, seq // tm, n_ctx // tm

    def qpos_of(i):
        return jnp.where(i < lat_tiles, i % seq_tiles, seq_tiles + (i - lat_tiles) % ctx_tiles)

    cw = hw // 2
    cb = col0 // cw
    shp = jax.ShapeDtypeStruct((bsz, NA_HEADS, pos_len, HEAD_DIM), BF16)
    kv_spec = pl.BlockSpec((1, NA_HEADS, tm, HEAD_DIM), lambda i: (batch_of(i), 0, pos_of(i), 0))
    q_spec = pl.BlockSpec((1, NA_HEADS, tm, HEAD_DIM), lambda i: (batch_of(i), 0, qpos_of(i), 0))

    def kern(q0, q1, k0, k1, v0, v1, gq_ref, gk_ref, q_ref, k_ref, v_ref):
        halves = NA_HEADS // 2
        for h in range(NA_HEADS):
            sl = slice((h % halves) * HEAD_DIM, (h % halves + 1) * HEAD_DIM)
            pq, pk, pv = ((q0, k0, v0) if h < halves else (q1, k1, v1))
            q_ref[0, h] = (_head_norm(pq[:, sl], gq_ref[...]) * (HEAD_DIM ** -0.5)).astype(BF16)
            k_ref[0, h] = _head_norm(pk[:, sl], gk_ref[...]).astype(BF16)
            v_ref[0, h] = pv[:, sl].astype(BF16)

    def col_spec(j):
        return pl.BlockSpec((tm, cw), lambda i: (i, cb + j))

    return pl.pallas_call(
        kern,
        grid=(t // tm,),
        in_specs=[col_spec(j) for j in range(6)] + [pl.BlockSpec((1, HEAD_DIM), lambda i: (0, 0)),
                                                    pl.BlockSpec((1, HEAD_DIM), lambda i: (0, 0))],
        out_specs=[q_spec, kv_spec, kv_spec],
        out_shape=[shp, shp, shp],
        compiler_params=_cparams("arbitrary"),
        name="prep_na",
    )(p, p, p, p, p, p, gq, gk)


def _na_kernel(q_ref, k_ref, v_ref, bias_ref, o_ref, *, n_ctx, rows):
    tq = NA_TILE_ROWS * GRID_W
    win = NA_WIN_ROWS * GRID_W
    ti = pl.program_id(2)
    w0 = jnp.clip(ti * NA_TILE_ROWS - NA_ROWS // 2, 0, rows - NA_WIN_ROWS)
    start = pl.multiple_of(n_ctx + w0 * GRID_W, GRID_W)
    n_tiles = rows // NA_TILE_ROWS
    gq = tq // 2
    sub = (NA_TILE_ROWS // 2 + NA_ROWS) * GRID_W
    shift = (NA_WIN_ROWS * GRID_W - sub)
    for g in range(2):
        off = jnp.where(ti == 0, 0, jnp.where(ti == n_tiles - 1, shift, g * shift))
        off = pl.multiple_of(off, 128)
        ks = pl.multiple_of(start + off, GRID_W)
        q = q_ref[0, 0, g * gq:(g + 1) * gq, :]
        s_cx = _dot_nt(q, k_ref[0, 0, 0:n_ctx, :])
        s_nb = _dot_nt(q, k_ref[0, 0, pl.ds(ks, sub), :]) + bias_ref[0, 0, g * gq:(g + 1) * gq, pl.ds(off, sub)]
        m = jnp.maximum(jnp.max(s_cx, axis=1, keepdims=True), jnp.max(s_nb, axis=1, keepdims=True))
        p_cx = jnp.exp(s_cx - m)
        p_nb = jnp.exp(s_nb - m)
        l = jnp.sum(p_cx, axis=1, keepdims=True) + jnp.sum(p_nb, axis=1, keepdims=True)
        acc = (_dot(p_cx.astype(BF16), v_ref[0, 0, 0:n_ctx, :])
               + _dot(p_nb.astype(BF16), v_ref[0, 0, pl.ds(ks, sub), :]))
        o_ref[g * gq:(g + 1) * gq, :] = (acc / l).astype(o_ref.dtype)


def _na_bias_tables(rpb, rows):
    n_tiles = rows // NA_TILE_ROWS
    tables = []
    for ti in (0, 1, n_tiles - 1):
        w0 = min(max(ti * NA_TILE_ROWS - NA_ROWS // 2, 0), rows - NA_WIN_ROWS)
        qr = ti * NA_TILE_ROWS + jnp.arange(NA_TILE_ROWS)
        r0 = jnp.clip(qr - NA_ROWS // 2, 0, rows - NA_ROWS)
        kr = w0 + jnp.arange(NA_WIN_ROWS)
        row_ok = (kr[None, :] >= r0[:, None]) & (kr[None, :] < r0[:, None] + NA_ROWS)
        dr = jnp.clip(kr[None, :] - qr[:, None] + NA_ROWS - 1, 0, 2 * NA_ROWS - 2)
        qc = jnp.arange(GRID_W)
        c0 = jnp.clip(qc - NA_COLS // 2, 0, GRID_W - NA_COLS)
        kc = jnp.arange(GRID_W)
        col_ok = (kc[None, :] >= c0[:, None]) & (kc[None, :] < c0[:, None] + NA_COLS)
        dc = jnp.clip(kc[None, :] - qc[:, None] + NA_COLS - 1, 0, 2 * NA_COLS - 2)
        sel_r = ((dr[:, :, None] == jnp.arange(2 * NA_ROWS - 1)) & row_ok[:, :, None]).astype(F32)
        sel_c = ((dc[:, :, None] == jnp.arange(2 * NA_COLS - 1)) & col_ok[:, :, None]).astype(F32)
        b = jnp.einsum("rki,hij,cqj->hrckq", sel_r, rpb.astype(F32), sel_c, precision=HI)
        ok = row_ok[:, None, :, None] & col_ok[None, :, None, :]
        b = jnp.where(ok[None], b, -jnp.inf)
        tables.append(b.reshape(rpb.shape[0], NA_TILE_ROWS * GRID_W, NA_WIN_ROWS * GRID_W))
    return jnp.stack(tables, axis=0).astype(F32)


def _attention_d(q, k, v, bias, *, n_lat, seq, n_ctx, bsz):
    rows = seq // GRID_W
    tq = NA_TILE_ROWS * GRID_W
    win = NA_WIN_ROWS * GRID_W
    n_tiles = rows // NA_TILE_ROWS
    pos_len = n_ctx + seq
    assert rows % NA_TILE_ROWS == 0 and rows >= NA_WIN_ROWS and n_tiles >= 3

    def kind(i):
        return jnp.where(i == 0, 0, jnp.where(i == n_tiles - 1, 2, 1))

    return pl.pallas_call(
        functools.partial(_na_kernel, n_ctx=n_ctx, rows=rows),
        grid=(bsz, NA_HEADS, n_tiles),
        in_specs=[pl.BlockSpec((1, 1, tq, HEAD_DIM), lambda b, h, i: (b, h, i, 0)),
                  pl.BlockSpec((1, 1, pos_len, HEAD_DIM), lambda b, h, i: (b, h, 0, 0)),
                  pl.BlockSpec((1, 1, pos_len, HEAD_DIM), lambda b, h, i: (b, h, 0, 0)),
                  pl.BlockSpec((1, 1, tq, win), lambda b, h, i: (kind(i), h, 0, 0))],
        out_specs=pl.BlockSpec((tq, HEAD_DIM), lambda b, h, i: (b * n_tiles + i, h)),
        out_shape=jax.ShapeDtypeStruct((n_lat, NA_HEADS * HEAD_DIM), BF16),
        compiler_params=_cparams("arbitrary", "arbitrary", "arbitrary"),
        name="attn_d_neighbourhood",
    )(q, k, v, bias)


def _pack_bf16_pairs(f):
    half = f.shape[1] // 2
    bits = lax.bitcast_convert_type(f.astype(BF16).astype(F32), U32)
    lo = lax.shift_right_logical(bits[:, :half], jnp.uint32(16))
    hi = bits[:, half:] & jnp.uint32(0xFFFF0000)
    return hi | lo


def _unpack_bf16_pairs(bits):
    lo = lax.bitcast_convert_type(lax.shift_left(bits, jnp.uint32(16)), F32).astype(BF16)
    hi = lax.bitcast_convert_type(bits & jnp.uint32(0xFFFF0000), F32).astype(BF16)
    return lo, hi


def _store_row_tiles(ref, val):
    rows, width = val.shape
    per = width // 128
    for s in range(per):
        ref[pl.ds(s, rows, stride=per), :] = val[:, s * 128:(s + 1) * 128]


def _load_row_tiles(ref, rows, per):
    return jnp.concatenate([ref[pl.ds(s, rows, stride=per), :] for s in range(per)], axis=1)


def _outproj_kernel(x_ref, y1_ref, y2_ref, w_ref, mod_ref, g_ref, wrh_ref, wrl_ref, xo_ref, fpk_ref, lg_ref):
    half = w_ref.shape[0] // 2
    tm = x_ref.shape[0]
    per = fpk_ref.shape[0] // tm
    rows = ROW_TILE
    for r0 in range(0, tm, rows):
        rs = slice(r0, r0 + rows)
        delta = _dot(y1_ref[rs, :], w_ref[0:half, :]) + _dot(y2_ref[rs, :], w_ref[half:, :])
        xn = x_ref[rs, :] + mod_ref[0, 2:3, :] * delta
        xo_ref[rs, :] = xn
        f = _norm_mod(xn, g_ref[...], mod_ref[0, 3:4, :], mod_ref[0, 4:5, :])
        f_hi = f.astype(BF16)
        f_lo = (f - f_hi.astype(F32)).astype(BF16)
        lg_ref[:, rs] = _dot_nt(wrh_ref[...], f_hi) + (_dot_nt(wrl_ref[...], f_hi) + _dot_nt(wrh_ref[...], f_lo))
        _store_row_tiles(fpk_ref.at[pl.ds(r0 * per, rows * per)], _pack_bf16_pairs(f))


def _outproj(x, y1, y2, w_bf16, mod, g, wr_t, *, n_rows, n_lat, seq, bsz):
    d = x.shape[1]
    tm = 2 * ROW_TILE
    hw = y1.shape[1]
    wr_hi = wr_t.astype(BF16)
    wr_lo = (wr_t - wr_hi.astype(F32)).astype(BF16)
    return pl.pallas_call(
        _outproj_kernel,
        grid=(n_rows // tm,),
        in_specs=[pl.BlockSpec((tm, d), lambda i: (i, 0)),
                  pl.BlockSpec((tm, hw), lambda i: (i, 0)),
                  pl.BlockSpec((tm, hw), lambda i: (i, 0)),
                  pl.BlockSpec((2 * hw, d), lambda i: (0, 0)),
                  pl.BlockSpec((1, 6, d), lambda i: (_group_of_tile(i, tm, n_lat, seq, bsz), 0, 0)),
                  pl.BlockSpec((1, d), lambda i: (0, 0)),
                  pl.BlockSpec((N_EXPERTS, d), lambda i: (0, 0)),
                  pl.BlockSpec((N_EXPERTS, d), lambda i: (0, 0))],
        out_specs=[pl.BlockSpec((tm, d), lambda i: (i, 0)),
                   pl.BlockSpec((tm * (d // 256), 128), lambda i: (i, 0)),
                   pl.BlockSpec((N_EXPERTS, tm), lambda i: (0, i))],
        out_shape=[jax.ShapeDtypeStruct((n_rows, d), F32),
                   jax.ShapeDtypeStruct((n_rows * (d // 256), 128), U32),
                   jax.ShapeDtypeStruct((N_EXPERTS, n_rows), F32)],
        compiler_params=_cparams("arbitrary"),
        name="outproj_residual_moe_in",
    )(x, y1, y2, w_bf16, mod, g, wr_hi, wr_lo)


def _route_kernel(lg_ref, b_ref, e_ref, r_ref, w_ref, cnt_ref, run_ref):
    tr = lg_ref.shape[1]
    per = N_EXPERTS // N_GROUPS
    neg = -jnp.inf

    @pl.when(pl.program_id(0) == 0)
    def _():
        run_ref[...] = jnp.zeros(run_ref.shape, F32)

    scores = jax.nn.sigmoid(lg_ref[...])
    sel = scores + b_ref[...]
    sel3 = sel.reshape(N_GROUPS, per, tr)
    mem = lax.broadcasted_iota(I32, (N_GROUPS, per, tr), 1)
    m1 = jnp.max(sel3, axis=1, keepdims=True)
    i1 = jnp.min(jnp.where(sel3 == m1, mem, per), axis=1, keepdims=True)
    m2 = jnp.max(jnp.where(mem == i1, neg, sel3), axis=1, keepdims=True)
    gs = (m1 + m2).reshape(N_GROUPS, tr)
    gid = lax.broadcasted_iota(I32, (N_GROUPS, tr), 0)
    keep = jnp.zeros((N_GROUPS, tr), F32)
    for _ in range(TOPK_GROUPS):
        gm = jnp.max(gs, axis=0, keepdims=True)
        gi = jnp.min(jnp.where(gs == gm, gid, N_GROUPS), axis=0, keepdims=True)
        hit = gid == gi
        keep = jnp.where(hit, 1.0, keep)
        gs = jnp.where(hit, neg, gs)
    keep3 = jnp.broadcast_to(keep.reshape(N_GROUPS, 1, tr), (N_GROUPS, per, tr))
    cand = jnp.where(keep3 > 0.5, sel3, neg).reshape(N_EXPERTS, tr)
    eid = lax.broadcasted_iota(I32, (N_EXPERTS, tr), 0)
    idxs, ws = [], []
    sel_f = jnp.zeros((N_EXPERTS, tr), F32)
    for _ in range(TOP_K):
        cm = jnp.max(cand, axis=0, keepdims=True)
        ci = jnp.min(jnp.where(cand == cm, eid, N_EXPERTS), axis=0, keepdims=True)
        hit = eid == ci
        idxs.append(ci)
        ws.append(jnp.sum(jnp.where(hit, scores, 0.0), axis=0, keepdims=True))
        sel_f = jnp.where(hit, 1.0, sel_f)
        cand = jnp.where(hit, neg, cand)
    before = (lax.broadcasted_iota(I32, (tr, tr), 0) < lax.broadcasted_iota(I32, (tr, tr), 1))
    rank = _dot(sel_f.astype(BF16), jnp.where(before, 1.0, 0.0).astype(BF16)) + run_ref[...]
    run_ref[...] = run_ref[...] + jnp.sum(sel_f, axis=1, keepdims=True)
    cnt_ref[...] = run_ref[...]
    wsum = ws[0]
    for kk in range(1, TOP_K):
        wsum = wsum + ws[kk]
    for kk in range(TOP_K):
        e_ref[kk:kk + 1, :] = idxs[kk]
        r_ref[kk:kk + 1, :] = jnp.sum(jnp.where(eid == idxs[kk], rank, 0.0), axis=0, keepdims=True).astype(I32)
        w_ref[kk:kk + 1, :] = ws[kk] / wsum * ROUTE_SCALE


def _route(logits_t, b_col):
    n_e, t = logits_t.shape
    tr = ROW_TILE
    return pl.pallas_call(
        _route_kernel,
        grid=(t // tr,),
        in_specs=[pl.BlockSpec((n_e, tr), lambda i: (0, i)),
                  pl.BlockSpec((n_e, 1), lambda i: (0, 0))],
        out_specs=[pl.BlockSpec((TOP_K, tr), lambda i: (0, i)),
                   pl.BlockSpec((TOP_K, tr), lambda i: (0, i)),
                   pl.BlockSpec((TOP_K, tr), lambda i: (0, i)),
                   pl.BlockSpec((n_e, 1), lambda i: (0, 0))],
        out_shape=[jax.ShapeDtypeStruct((TOP_K, t), I32),
                   jax.ShapeDtypeStruct((TOP_K, t), I32),
                   jax.ShapeDtypeStruct((TOP_K, t), F32),
                   jax.ShapeDtypeStruct((n_e, 1), F32)],
        scratch_shapes=[pltpu.VMEM((n_e, 1), F32)],
        compiler_params=_cparams("arbitrary"),
        name="moe_route",
    )(logits_t, b_col)


PK_TILES = 8


def _ffn(bits, w1, w3, w2):
    half = bits.shape[1]
    lo, hi = _unpack_bf16_pairs(bits)
    h1 = _dot(lo, w1[0:half, :]) + _dot(hi, w1[half:, :])
    h3 = _dot(lo, w3[0:half, :]) + _dot(hi, w3[half:, :])
    return _dot((_silu(h1) * h3).astype(BF16), w2[...])


def _expert_kernel(be_ref, bv_ref, nu_ref, xs_ref, w1_ref, w3_ref, w2_ref, y_ref, w1b, w3b, w2b):
    i = pl.program_id(0)

    @pl.when(i < nu_ref[0])
    def _():
        e = be_ref[i]
        prev = be_ref[jnp.maximum(i - 1, 0)]

        @pl.when((i == 0) | (e != prev))
        def _():
            w1b[...] = w1_ref[0, 0].astype(BF16)
            w3b[...] = w3_ref[0, 0].astype(BF16)
            w2b[...] = w2_ref[0, 0].astype(BF16)

        tm = MOE_BLOCK
        bits = _load_row_tiles(xs_ref, tm, PK_TILES)
        rows = lax.broadcasted_iota(I32, bits.shape, 0)
        bits = jnp.where(rows < bv_ref[i], bits, jnp.uint32(0))
        _store_row_tiles(y_ref, _pack_bf16_pairs(_ffn(bits, w1b, w3b, w2b)))


def _experts(blk_expert, blk_valid, n_used, xs, w1, w3, w2, layer):
    n_slots = xs.shape[0] // PK_TILES
    d, de = w1.shape[2], w1.shape[3]
    assert d == 2 * PK_TILES * 128
    tm = MOE_BLOCK
    grid_spec = pltpu.PrefetchScalarGridSpec(
        num_scalar_prefetch=3,
        grid=(n_slots // tm,),
        in_specs=[pl.BlockSpec((tm * PK_TILES, 128), lambda i, be, bv, nu: (i, 0)),
                  pl.BlockSpec((1, 1, d, de), lambda i, be, bv, nu: (layer, be[i], 0, 0)),
                  pl.BlockSpec((1, 1, d, de), lambda i, be, bv, nu: (layer, be[i], 0, 0)),
                  pl.BlockSpec((1, 1, de, d), lambda i, be, bv, nu: (layer, be[i], 0, 0))],
        out_specs=pl.BlockSpec((tm * PK_TILES, 128), lambda i, be, bv, nu: (i, 0)),
        scratch_shapes=[pltpu.VMEM((d, de), BF16), pltpu.VMEM((d, de), BF16), pltpu.VMEM((de, d), BF16)],
    )
    return pl.pallas_call(
        _expert_kernel,
        grid_spec=grid_spec,
        out_shape=jax.ShapeDtypeStruct((n_slots * PK_TILES, 128), U32),
        compiler_params=_cparams("arbitrary"),
        name="moe_experts",
    )(blk_expert, blk_valid, n_used, xs, w1, w3, w2)


def _scatter_kernel(dest_ref, f_ref, xs_ref, sem, *, tm):
    def copy(t, kk):
        src = pl.multiple_of(t * PK_TILES, PK_TILES)
        dst = pl.multiple_of(dest_ref[0, 0, t * TOP_K + kk], PK_TILES)
        return pltpu.make_async_copy(f_ref.at[pl.ds(src, PK_TILES)], xs_ref.at[pl.ds(dst, PK_TILES)], sem)

    def issue(t, c):
        for kk in range(TOP_K):
            copy(t, kk).start()
        return c

    def drain(t, c):
        for kk in range(TOP_K):
            copy(t, kk).wait()
        return c

    lax.fori_loop(0, tm, issue, 0)
    lax.fori_loop(0, tm, drain, 0)


def _scatter_rows(dest3, fpk, n_slots):
    tm = dest3.shape[2] // TOP_K
    t = fpk.shape[0] // PK_TILES
    return pl.pallas_call(
        functools.partial(_scatter_kernel, tm=tm),
        grid=(t // tm,),
        in_specs=[pl.BlockSpec((1, 1, TOP_K * tm), lambda i: (i, 0, 0), memory_space=pltpu.SMEM),
                  pl.BlockSpec((tm * PK_TILES, 128), lambda i: (i, 0))],
        out_specs=pl.BlockSpec(memory_space=pl.ANY),
        out_shape=jax.ShapeDtypeStruct((n_slots * PK_TILES, 128), U32),
        scratch_shapes=[pltpu.SemaphoreType.DMA(())],
        compiler_params=_cparams("arbitrary"),
        name="moe_scatter_rows",
    )(dest3, fpk)


def _shared_kernel(f_ref, w1_ref, w3_ref, w2_ref, y_ref):
    y_ref[...] = _ffn(_load_row_tiles(f_ref, y_ref.shape[0], PK_TILES), w1_ref, w3_ref, w2_ref)


def _shared_expert(fpk, w1, w3, w2):
    t = fpk.shape[0] // PK_TILES
    d, de = w1.shape
    tm = 2 * ROW_TILE
    return pl.pallas_call(
        _shared_kernel,
        grid=(t // tm,),
        in_specs=[pl.BlockSpec((tm * PK_TILES, 128), lambda i: (i, 0)),
                  pl.BlockSpec((d, de), lambda i: (0, 0)),
                  pl.BlockSpec((d, de), lambda i: (0, 0)),
                  pl.BlockSpec((de, d), lambda i: (0, 0))],
        out_specs=pl.BlockSpec((tm, d), lambda i: (i, 0)),
        out_shape=jax.ShapeDtypeStruct((t, d), F32),
        compiler_params=_cparams("arbitrary"),
        name="moe_shared_expert",
    )(fpk, w1, w3, w2)


def _combine_kernel(dest_ref, dest_next_ref, w_ref, x_ref, ysh_ref, mod_ref, y_ref, o_ref, ybuf, lo_ref, hi_ref, sems):
    tm = x_ref.shape[0]
    i = pl.program_id(0)
    n = pl.num_programs(0)
    slot = i % 2

    def copy(idx_ref, s, t, kk):
        src = pl.multiple_of(idx_ref[0, 0, t * TOP_K + kk], PK_TILES)
        dst = pl.multiple_of(t * PK_TILES, PK_TILES)
        return pltpu.make_async_copy(y_ref.at[pl.ds(src, PK_TILES)], ybuf.at[s, kk, pl.ds(dst, PK_TILES)],
                                     sems.at[s])

    def issue_tile(idx_ref, s):
        def issue(t, c):
            for kk in range(TOP_K):
                copy(idx_ref, s, t, kk).start()
            return c

        lax.fori_loop(0, tm, issue, 0)

    @pl.when(i == 0)
    def _():
        issue_tile(dest_ref, 0)

    @pl.when(i + 1 < n)
    def _():
        issue_tile(dest_next_ref, 1 - slot)

    def drain(t, c):
        for kk in range(TOP_K):
            copy(dest_ref, slot, t, kk).wait()
        return c

    lax.fori_loop(0, tm, drain, 0)

    def token(t, c):
        r = pl.multiple_of(t * PK_TILES, PK_TILES)
        a_lo = a_hi = None
        for kk in range(TOP_K):
            w = w_ref[0, 0, t * TOP_K + kk]
            bits = ybuf[slot, kk, pl.ds(r, PK_TILES), :]
            lo = lax.bitcast_convert_type(lax.shift_left(bits, jnp.uint32(16)), F32)
            hi = lax.bitcast_convert_type(bits & jnp.uint32(0xFFFF0000), F32)
            a_lo = w * lo if a_lo is None else a_lo + w * lo
            a_hi = w * hi if a_hi is None else a_hi + w * hi
        lo_ref[pl.ds(r, PK_TILES), :] = a_lo
        hi_ref[pl.ds(r, PK_TILES), :] = a_hi
        return c

    lax.fori_loop(0, tm, token, 0, unroll=4)
    half = PK_TILES * 128
    for c in range(PK_TILES):
        for base, acc_ref in ((0, lo_ref), (half, hi_ref)):
            sl = slice(base + c * 128, base + (c + 1) * 128)
            routed = acc_ref[pl.ds(c, tm, stride=PK_TILES), :]
            o_ref[:, sl] = x_ref[:, sl] + mod_ref[0, 5:6, sl] * (routed + ysh_ref[:, sl])


def _combine(dest3, x, ysh, w3, mod, y, *, tm, n_lat, seq, bsz):
    t, d = x.shape
    n = t // tm
    return pl.pallas_call(
        _combine_kernel,
        grid=(n,),
        in_specs=[pl.BlockSpec((1, 1, TOP_K * tm), lambda i: (i, 0, 0), memory_space=pltpu.SMEM),
                  pl.BlockSpec((1, 1, TOP_K * tm), lambda i: (jnp.minimum(i + 1, n - 1), 0, 0),
                               memory_space=pltpu.SMEM),
                  pl.BlockSpec((1, 1, TOP_K * tm), lambda i: (i, 0, 0), memory_space=pltpu.SMEM),
                  pl.BlockSpec((tm, d), lambda i: (i, 0)),
                  pl.BlockSpec((tm, d), lambda i: (i, 0)),
                  pl.BlockSpec((1, 6, d), lambda i: (_group_of_tile(i, tm, n_lat, seq, bsz), 0, 0)),
                  pl.BlockSpec(memory_space=pl.ANY)],
        out_specs=pl.BlockSpec((tm, d), lambda i: (i, 0)),
        out_shape=jax.ShapeDtypeStruct((t, d), F32),
        scratch_shapes=[pltpu.VMEM((2, TOP_K, tm * PK_TILES, 128), U32),
                        pltpu.VMEM((tm * PK_TILES, 128), F32), pltpu.VMEM((tm * PK_TILES, 128), F32),
                        pltpu.SemaphoreType.DMA((2,))],
        compiler_params=_cparams("arbitrary"),
        name="moe_combine",
    )(dest3, dest3, w3, x, ysh, mod, y)


def _moe(x_new, fpk, logits_t, mod, b_router, w_e1, w_e3, w_e2, w_s1, w_s3, w_s2, *, layer, n_lat, seq, bsz):
    t = x_new.shape[0]
    top_e, top_r, top_w, counts = _route(logits_t, b_router.reshape(N_EXPERTS, 1))
    counts = counts[:, 0].astype(I32)
    blocks = (counts + MOE_BLOCK - 1) // MOE_BLOCK
    blk_end = jnp.cumsum(blocks)
    blk_start = blk_end - blocks
    n_blocks = -(-(t * TOP_K) // MOE_BLOCK) + N_EXPERTS
    n_slots = n_blocks * MOE_BLOCK
    slot_start = blk_start * MOE_BLOCK
    dest = jnp.sum(jnp.where(top_e[None] == jnp.arange(N_EXPERTS, dtype=I32)[:, None, None],
                             slot_start[:, None, None], 0), axis=0) + top_r
    bi = jnp.arange(n_blocks, dtype=I32)
    blk_expert = jnp.minimum(jnp.sum((bi[:, None] >= blk_end[None, :]).astype(I32), axis=1), N_EXPERTS - 1)
    blk_valid = jnp.clip(counts[blk_expert] - (bi - blk_start[blk_expert]) * MOE_BLOCK, 0, MOE_BLOCK)
    n_used = blk_end[-1:].astype(I32)

    dest_tk = dest.T
    tm_s = 2 * ROW_TILE
    dest_s = (dest_tk * PK_TILES).reshape(t // tm_s, 1, tm_s * TOP_K)
    xs = _scatter_rows(dest_s, fpk, n_slots)
    y = _experts(blk_expert.astype(I32), blk_valid.astype(I32), n_used, xs, w_e1, w_e3, w_e2, layer)
    ysh = _shared_expert(fpk, w_s1.astype(BF16), w_s3.astype(BF16), w_s2.astype(BF16))
    tm_c = ROW_TILE
    dest_c = (dest_tk * PK_TILES).reshape(t // tm_c, 1, tm_c * TOP_K)
    w_c = top_w.T.reshape(t // tm_c, 1, tm_c * TOP_K)
    return _combine(dest_c, x_new, ysh, w_c, mod, y, tm=tm_c, n_lat=n_lat, seq=seq, bsz=bsz)


def _pad_cols(w, n):
    return jnp.pad(w, ((0, 0), (0, n - w.shape[1])))


def kernel(x, c, ctx, c_ctx, w_ada, b_ada, g_mix, g_ffn, w_in_even, gq_a, gk_a, conv_w, conv_b, a_log, dt_bias,
           d_skip, g_ssm, w_in_odd, gq_c, gk_c, sink_c, gq_d, gk_d, rpb_d, w_out, w_router, b_router, w_e1, w_e3,
           w_e2, w_s1, w_s3, w_s2):
    bsz, seq, d = x.shape
    n_ctx = ctx.shape[1]
    depth = w_ada.shape[0]
    n_lat = bsz * seq
    geo = dict(n_lat=n_lat, seq=seq, n_ctx=n_ctx, bsz=bsz)
    xs = jnp.concatenate([x.reshape(n_lat, d), ctx.reshape(bsz * n_ctx, d)], axis=0)
    cvec = jnp.zeros((8, d), F32).at[:bsz].set(c).at[bsz].set(c_ctx)
    cos, sin = _rope_tables(seq, n_ctx)
    expand = jnp.zeros((128, SSM_INNER), BF16).at[:SSM_HEADS].set(
        jnp.repeat(jnp.eye(SSM_HEADS, dtype=BF16), SSM_HEAD_DIM, axis=1))

    for layer in range(depth):
        last = layer == depth - 1
        i = layer // 2
        mod = _adaln(cvec, w_ada, b_ada[layer].reshape(1, -1), layer)[:bsz + 1].reshape(bsz + 1, 6, d)
        g1 = g_mix[layer].reshape(1, d)
        if layer % 2 == 0:
            n_in = w_in_even.shape[2]
            n_pad = -(-n_in // 384) * 384
            p = _norm_mod_matmul(xs, g1, mod, _pad_cols(w_in_even[i], n_pad).astype(BF16),
                                 n_lat=n_lat, seq=seq, bsz=bsz, tn=n_pad // 3)
            qt, k, vt = _prep_rope(p, cos, sin, gq_a[i].reshape(1, -1), gk_a[i].reshape(1, -1),
                                   q_scale=HEAD_DIM ** -0.5 * LOG2E, n_q=A_HEADS, n_kv=A_KV_HEADS, **geo)
            y1 = _attention_a(qt, k, vt, need_ctx=not last, **geo)
            xbc = _ssm_conv(p, conv_w[i], conv_b[i].reshape(1, -1), col0=A_IN + SSM_INNER,
                            n_lat=n_lat, seq=seq, n_ctx=n_ctx)
            dt_row = jnp.zeros((1, 128), F32).at[0, :2 * SSM_HEADS].set(dt_bias[i].reshape(-1))
            dt2 = _ssm_dt(p, dt_row, col0=A_IN + SSM_INNER + SSM_CONV_DIM)
            a_rows = jnp.zeros((2, 1, 128), F32).at[:, 0, :SSM_HEADS].set(-jnp.exp(a_log[i]))
            ydir = _ssd_scan(xbc, dt2, a_rows, expand, **geo)
            y2 = _ssd_finish(ydir, xbc, p, jnp.repeat(d_skip[i], SSM_HEAD_DIM).reshape(1, -1),
                             g_ssm[i].reshape(1, -1), z_col0=A_IN)
        else:
            p = _norm_mod_matmul(xs, g1, mod, w_in_odd[i].astype(BF16), n_lat=n_lat, seq=seq, bsz=bsz,
                                 tn=w_in_odd.shape[2] // 3)
            qt, k, vt = _prep_rope(p, cos, sin, gq_c[i].reshape(1, -1), gk_c[i].reshape(1, -1),
                                   q_scale=HEAD_DIM ** -0.5, n_q=A_HEADS, n_kv=A_KV_HEADS, **geo)
            group = A_HEADS // A_KV_HEADS
            sink_rows = jnp.repeat(sink_c[i].reshape(A_KV_HEADS, group), Q_TILE, axis=1).reshape(
                A_KV_HEADS, 1, group * Q_TILE)
            y1 = _attention_c(qt, k, vt, sink_rows, **geo)
            qd, kd, vd = _prep_na(p, gq_d[i].reshape(1, -1), gk_d[i].reshape(1, -1), col0=A_IN, **geo)
            y2 = _attention_d(qd, kd, vd, _na_bias_tables(rpb_d[i], seq // GRID_W), **geo)
        n_rows = n_lat if last else xs.shape[0]
        x_new, fpk, logits_t = _outproj(xs, y1, y2, w_out[layer].astype(BF16), mod, g_ffn[layer].reshape(1, d),
                                        w_router[layer].T, n_rows=n_rows, n_lat=n_lat, seq=seq, bsz=bsz)
        xs = _moe(x_new, fpk, logits_t, mod, b_router[layer], w_e1, w_e3, w_e2,
                  w_s1[layer], w_s3[layer], w_s2[layer], layer=layer, n_lat=n_lat, seq=seq, bsz=bsz)
    return xs[:n_lat].reshape(bsz, seq, d)
```

```python
import functools
import math

import jax
import jax.numpy as jnp
from jax import lax
from jax.experimental import pallas as pl
from jax.experimental.pallas import tpu as pltpu

F32 = jnp.float32
BF16 = jnp.bfloat16
I32 = jnp.int32
U32 = jnp.uint32

EPS = 1e-6
HEAD_DIM = 128
GRID_W = 64
ROPE_THETA = 10000.0
WINDOW = 128
NA_ROWS = 8
NA_COLS = 16
NA_TILE_ROWS = 8
NA_WIN_ROWS = 16
A_HEADS = 8
A_KV_HEADS = 2
SSM_HEADS = 16
SSM_HEAD_DIM = 64
SSM_INNER = SSM_HEADS * SSM_HEAD_DIM
SSM_GROUPS = 2
SSM_STATE = 128
SSM_CONV = 5
SSM_CONV_DIM = SSM_INNER + 2 * SSM_GROUPS * SSM_STATE
SSM_CHUNK = 128
A_IN = (A_HEADS + 2 * A_KV_HEADS) * HEAD_DIM
NA_HEADS = 8
N_EXPERTS = 64
TOP_K = 8
N_GROUPS = 8
TOPK_GROUPS = 4
ROUTE_SCALE = 2.5
MOE_BLOCK = 512

VMEM_LIMIT_BYTES = 56 * 1024 * 1024
ROW_TILE = 256
Q_TILE = 256
VT_ROWS = HEAD_DIM + 16
LOG2E = 1.4426950408889634
HI = lax.Precision.HIGHEST


def _cparams(*sem):
    return pltpu.CompilerParams(dimension_semantics=sem, vmem_limit_bytes=VMEM_LIMIT_BYTES)


def _silu(x):
    return x * jax.nn.sigmoid(x)


def _hi_dot(a, b):
    return jnp.dot(a, b, precision=HI, preferred_element_type=F32)


def _dot(a, b):
    return jnp.dot(a, b, preferred_element_type=F32)


def _dot_nt(a, b):
    return lax.dot_general(a, b, (((1,), (1,)), ((), ())), preferred_element_type=F32)


def _largest_tile(n, cap, mult=128):
    best = None
    for t in range(mult, cap + 1, mult):
        if n % t == 0:
            best = t
    assert best is not None, (n, cap)
    return best


def _adaln_kernel(c_ref, w_ref, b_ref, o_ref):
    o_ref[...] = _hi_dot(_silu(c_ref[...]), w_ref[0]) + b_ref[...]


def _adaln(cvec, w, b, layer):
    _, d, n = w.shape
    tn = 1024
    return pl.pallas_call(
        _adaln_kernel,
        grid=(n // tn,),
        in_specs=[pl.BlockSpec((8, d), lambda j: (0, 0)),
                  pl.BlockSpec((1, d, tn), lambda j: (layer, 0, j)),
                  pl.BlockSpec((1, tn), lambda j: (0, j))],
        out_specs=pl.BlockSpec((8, tn), lambda j: (0, j)),
        out_shape=jax.ShapeDtypeStruct((8, n), F32),
        compiler_params=_cparams("arbitrary"),
        name="adaln",
    )(cvec, w, b)


def _norm_mod(x, g, shift, scale):
    ms = jnp.mean(x * x, axis=-1, keepdims=True)
    return (x * lax.rsqrt(ms + EPS) * g) * (1.0 + scale) + shift


def _nmm_kernel(x_ref, g_ref, mod_ref, w_ref, o_ref, xn_ref):
    i, j = pl.program_id(0), pl.program_id(1)
    last_j = pl.num_programs(1) - 1
    slot = i % 2

    def normalised():
        return _norm_mod(x_ref[...], g_ref[...], mod_ref[0, 0:1, :], mod_ref[0, 1:2, :]).astype(BF16)

    @pl.when((i == 0) & (j == 0))
    def _():
        xn_ref[0] = normalised()

    @pl.when(j < last_j)
    def _():
        o_ref[...] = _dot(xn_ref[slot], w_ref[...])

    @pl.when(j == last_j)
    def _():
        o_ref[...] = _dot(xn_ref[slot], w_ref[...])
        xn_ref[1 - slot] = normalised()


def _group_of_tile(i, tm, n_lat, seq, bsz):
    return jnp.where(i * tm < n_lat, (i * tm) // seq, bsz)


def _norm_mod_matmul(x, g, mod, w_bf16, *, n_lat, seq, bsz, tn):
    t, d = x.shape
    n = w_bf16.shape[1]
    tm = 2 * ROW_TILE
    assert t % tm == 0 and n % tn == 0 and n_lat % tm == 0
    n_i, n_j = t // tm, n // tn
    assert n_j >= 2

    def row_tile(i, j):
        return jnp.minimum(i + (j == n_j - 1).astype(jnp.int32), n_i - 1)

    return pl.pallas_call(
        _nmm_kernel,
        grid=(n_i, n_j),
        in_specs=[pl.BlockSpec((tm, d), lambda i, j: (row_tile(i, j), 0)),
                  pl.BlockSpec((1, d), lambda i, j: (0, 0)),
                  pl.BlockSpec((1, 6, d), lambda i, j: (_group_of_tile(row_tile(i, j), tm, n_lat, seq, bsz), 0, 0)),
                  pl.BlockSpec((d, tn), lambda i, j: (0, j))],
        out_specs=pl.BlockSpec((tm, tn), lambda i, j: (i, j)),
        out_shape=jax.ShapeDtypeStruct((t, n), F32),
        scratch_shapes=[pltpu.VMEM((2, tm, d), BF16)],
        compiler_params=_cparams("arbitrary", "arbitrary"),
        name="norm_mod_inproj",
    )(x, g, mod, w_bf16)


def _head_norm(x, g):
    ms = jnp.mean(x * x, axis=-1, keepdims=True)
    return x * lax.rsqrt(ms + EPS) * g


def _prep_rope_kernel(p_ref, cos_ref, sin_ref, gq_ref, gk_ref, qt_ref, k_ref, vt_ref, *, n_q, n_kv, tq, q_scale):
    cos = cos_ref[...]
    sin = sin_ref[...]
    lane = lax.broadcasted_iota(I32, cos.shape, 1)
    first_half = (lane % 64) < 32

    def norm_rope(x, g):
        y = _head_norm(x, g)
        swapped = jnp.where(first_half, pltpu.roll(y, 96, 1), pltpu.roll(y, 32, 1))
        return y * cos + swapped * sin

    group = n_q // n_kv
    for h in range(n_q):
        q = norm_rope(p_ref[:, h * HEAD_DIM:(h + 1) * HEAD_DIM], gq_ref[...]) * q_scale
        kvh, hh = divmod(h, group)
        qt_ref[0, kvh, 0, :, hh * tq:(hh + 1) * tq] = q.T.astype(BF16)
    for h in range(n_kv):
        c0 = (n_q + h) * HEAD_DIM
        k_ref[0, h] = norm_rope(p_ref[:, c0:c0 + HEAD_DIM], gk_ref[...]).astype(BF16)
        c1 = (n_q + n_kv + h) * HEAD_DIM
        vt_ref[0, h, 0:HEAD_DIM, :] = p_ref[:, c1:c1 + HEAD_DIM].T.astype(BF16)
        vt_ref[0, h, HEAD_DIM:VT_ROWS, :] = jnp.ones((VT_ROWS - HEAD_DIM, tq), BF16)


def _pos_maps(tm, n_lat, seq, n_ctx):
    lat_tiles, seq_tiles, ctx_tiles = n_lat // tm, seq // tm, n_ctx // tm

    def batch_of(i):
        return jnp.where(i < lat_tiles, i // seq_tiles, (i - lat_tiles) // ctx_tiles)

    def pos_of(i):
        return jnp.where(i < lat_tiles, ctx_tiles + i % seq_tiles, (i - lat_tiles) % ctx_tiles)

    return batch_of, pos_of


def _prep_rope(p, cos, sin, gq, gk, *, q_scale, n_q, n_kv, n_lat, seq, n_ctx, bsz):
    t = p.shape[0]
    tm = Q_TILE
    pos_len = n_ctx + seq
    group = n_q // n_kv
    width = (n_q + 2 * n_kv) * HEAD_DIM
    batch_of, pos_of = _pos_maps(tm, n_lat, seq, n_ctx)
    kern = functools.partial(_prep_rope_kernel, n_q=n_q, n_kv=n_kv, tq=tm, q_scale=q_scale)
    return pl.pallas_call(
        kern,
        grid=(t // tm,),
        in_specs=[pl.BlockSpec((tm, width), lambda i: (i, 0)),
                  pl.BlockSpec((tm, HEAD_DIM), lambda i: (pos_of(i), 0)),
                  pl.BlockSpec((tm, HEAD_DIM), lambda i: (pos_of(i), 0)),
                  pl.BlockSpec((1, HEAD_DIM), lambda i: (0, 0)),
                  pl.BlockSpec((1, HEAD_DIM), lambda i: (0, 0))],
        out_specs=[pl.BlockSpec((1, n_kv, 1, HEAD_DIM, group * tm), lambda i: (batch_of(i), 0, pos_of(i), 0, 0)),
                   pl.BlockSpec((1, n_kv, tm, HEAD_DIM), lambda i: (batch_of(i), 0, pos_of(i), 0)),
                   pl.BlockSpec((1, n_kv, VT_ROWS, tm), lambda i: (batch_of(i), 0, 0, pos_of(i)))],
        out_shape=[jax.ShapeDtypeStruct((bsz, n_kv, pos_len // tm, HEAD_DIM, group * tm), BF16),
                   jax.ShapeDtypeStruct((bsz, n_kv, pos_len, HEAD_DIM), BF16),
                   jax.ShapeDtypeStruct((bsz, n_kv, VT_ROWS, pos_len), BF16)],
        compiler_params=_cparams("arbitrary"),
        name="prep_rope",
    )(p, cos, sin, gq, gk)


def _rope_tables(seq, n_ctx):
    t = jnp.arange(seq, dtype=I32)
    row = (t // GRID_W).astype(F32)
    col = (t % GRID_W).astype(F32)
    n_freq = HEAD_DIM // 4
    inv = ROPE_THETA ** (-jnp.arange(n_freq, dtype=F32) / n_freq)
    ar, ac = row[:, None] * inv, col[:, None] * inv
    cos = jnp.concatenate([jnp.cos(ar), jnp.cos(ar), jnp.cos(ac), jnp.cos(ac)], axis=1)
    sin = jnp.concatenate([-jnp.sin(ar), jnp.sin(ar), -jnp.sin(ac), jnp.sin(ac)], axis=1)
    cos = jnp.concatenate([jnp.ones((n_ctx, HEAD_DIM), F32), cos], axis=0)
    sin = jnp.concatenate([jnp.zeros((n_ctx, HEAD_DIM), F32), sin], axis=0)
    return cos, sin


def _flash_t_kernel(qt_ref, k_ref, vt_ref, o_ref, m_ref, l_ref, acc_ref, sa_ref, sb_ref, *, tk, n_k, tq, group, n_qt):
    qt = jnp.concatenate([qt_ref[0, 0, t] for t in range(n_qt)], axis=1)
    m_ref[...] = jnp.full(m_ref.shape, -jnp.inf, F32)
    l_ref[...] = jnp.zeros(l_ref.shape, F32)
    acc_ref[...] = jnp.zeros(acc_ref.shape, F32)

    def scores(j, dst):
        off = pl.multiple_of(jnp.minimum(j, n_k - 1) * tk, tk)
        dst[...] = _dot(k_ref[0, 0, pl.ds(off, tk), :], qt)

    def consume(j, src):
        off = pl.multiple_of(j * tk, tk)
        s = src[...]
        m_old = m_ref[...]
        m_new = jnp.maximum(m_old, jnp.max(s, axis=0, keepdims=True))
        alpha = jnp.exp2(m_old - m_new)
        p = jnp.exp2(s - m_new).astype(BF16)
        r = _dot(vt_ref[0, 0, :, pl.ds(off, tk)], p)
        acc_ref[...] = alpha * acc_ref[...] + r[0:HEAD_DIM]
        l_ref[...] = alpha * l_ref[...] + r[HEAD_DIM:HEAD_DIM + 1]
        m_ref[...] = m_new

    scores(0, sa_ref)

    def body(jj, carry):
        j = 2 * jj
        scores(j + 1, sb_ref)
        consume(j, sa_ref)
        scores(j + 2, sa_ref)
        consume(j + 1, sb_ref)
        return carry

    lax.fori_loop(0, n_k // 2, body, 0)
    if n_k % 2:
        consume(n_k - 1, sa_ref)
    out = acc_ref[...] / l_ref[...]
    for t in range(n_qt):
        for h in range(group):
            c0 = (t * group + h) * tq
            o_ref[t * tq:(t + 1) * tq, h * HEAD_DIM:(h + 1) * HEAD_DIM] = out[:, c0:c0 + tq].T.astype(o_ref.dtype)


def _attention_a(qt, k, vt, *, n_lat, seq, n_ctx, bsz, need_ctx):
    n_kv, group, tq = k.shape[1], qt.shape[-1] // Q_TILE, Q_TILE
    pos_len = n_ctx + seq
    width = n_kv * group * HEAD_DIM
    tk = _largest_tile(pos_len, 1280)

    def scratch(tkk, n_qt):
        n = n_qt * group * tq
        return [pltpu.VMEM((1, n), F32), pltpu.VMEM((1, n), F32), pltpu.VMEM((HEAD_DIM, n), F32),
                pltpu.VMEM((tkk, n), F32), pltpu.VMEM((tkk, n), F32)]

    n_qt = 2 if (seq // tq) % 2 == 0 else 1
    tq2 = n_qt * tq
    qt_lat = qt[:, :, n_ctx // tq:]
    y = pl.pallas_call(
        functools.partial(_flash_t_kernel, tk=tk, n_k=pos_len // tk, tq=tq, group=group, n_qt=n_qt),
        grid=(bsz, n_kv, seq // tq2),
        in_specs=[pl.BlockSpec((1, 1, n_qt, HEAD_DIM, group * tq), lambda b, h, i: (b, h, i, 0, 0)),
                  pl.BlockSpec((1, 1, pos_len, HEAD_DIM), lambda b, h, i: (b, h, 0, 0)),
                  pl.BlockSpec((1, 1, VT_ROWS, pos_len), lambda b, h, i: (b, h, 0, 0))],
        out_specs=pl.BlockSpec((tq2, group * HEAD_DIM), lambda b, h, i: (b * (seq // tq2) + i, h)),
        out_shape=jax.ShapeDtypeStruct((n_lat, width), BF16),
        scratch_shapes=scratch(tk, n_qt),
        compiler_params=_cparams("arbitrary", "arbitrary", "arbitrary"),
        name="attn_a_latent",
    )(qt_lat, k, vt)
    if not need_ctx:
        return y
    tkc = _largest_tile(n_ctx, 768)
    y_ctx = pl.pallas_call(
        functools.partial(_flash_t_kernel, tk=tkc, n_k=n_ctx // tkc, tq=tq, group=group, n_qt=1),
        grid=(bsz, n_kv, n_ctx // tq),
        in_specs=[pl.BlockSpec((1, 1, 1, HEAD_DIM, group * tq), lambda b, h, i: (b, h, i, 0, 0)),
                  pl.BlockSpec((1, 1, n_ctx, HEAD_DIM), lambda b, h, i: (b, h, 0, 0)),
                  pl.BlockSpec((1, 1, VT_ROWS, n_ctx), lambda b, h, i: (b, h, 0, 0))],
        out_specs=pl.BlockSpec((tq, group * HEAD_DIM), lambda b, h, i: (b * (n_ctx // tq) + i, h)),
        out_shape=jax.ShapeDtypeStruct((bsz * n_ctx, width), BF16),
        scratch_shapes=scratch(tkc, 1),
        compiler_params=_cparams("arbitrary", "arbitrary", "arbitrary"),
        name="attn_a_context",
    )(qt, k, vt)
    return jnp.concatenate([y, y_ctx], axis=0)


def _segment_edges(i, tm, n_lat, seq, n_ctx):
    lat_tiles, seq_tiles, ctx_tiles = n_lat // tm, seq // tm, n_ctx // tm
    is_lat = i < lat_tiles
    pos = jnp.where(is_lat, i % seq_tiles, (i - lat_tiles) % ctx_tiles)
    last = jnp.where(is_lat, seq_tiles, ctx_tiles) - 1
    return pos == 0, pos == last


def _conv_kernel(cur_ref, prev_ref, next_ref, w_ref, b_ref, o_ref, ext_ref, *, tm, n_lat, seq, n_ctx):
    first, last = _segment_edges(pl.program_id(0), tm, n_lat, seq, n_ctx)
    ext_ref[0:8, :] = jnp.where(first, 0.0, prev_ref[...])
    ext_ref[8:8 + tm, :] = cur_ref[...]
    ext_ref[8 + tm:16 + tm, :] = jnp.where(last, 0.0, next_ref[...])
    acc = jnp.broadcast_to(b_ref[...], (tm, b_ref.shape[1]))
    half = SSM_CONV // 2
    for kk in range(SSM_CONV):
        acc = acc + w_ref[kk:kk + 1, :] * ext_ref[pl.ds(8 - half + kk, tm), :]
    o_ref[...] = _silu(acc)


def _ssm_conv(p, conv_w, conv_b, *, col0, n_lat, seq, n_ctx):
    t = p.shape[0]
    tm, tc = ROW_TILE, 512
    assert col0 % tc == 0 and SSM_CONV_DIM % tc == 0
    cb0 = col0 // tc
    r8 = tm // 8
    kern = functools.partial(_conv_kernel, tm=tm, n_lat=n_lat, seq=seq, n_ctx=n_ctx)
    return pl.pallas_call(
        kern,
        grid=(t // tm, SSM_CONV_DIM // tc),
        in_specs=[pl.BlockSpec((tm, tc), lambda i, j: (i, cb0 + j)),
                  pl.BlockSpec((8, tc), lambda i, j: (jnp.maximum(i * r8 - 1, 0), cb0 + j)),
                  pl.BlockSpec((8, tc), lambda i, j: (jnp.minimum((i + 1) * r8, t // 8 - 1), cb0 + j)),
                  pl.BlockSpec((SSM_CONV, tc), lambda i, j: (0, j)),
                  pl.BlockSpec((1, tc), lambda i, j: (0, j))],
        out_specs=pl.BlockSpec((tm, tc), lambda i, j: (i, j)),
        out_shape=jax.ShapeDtypeStruct((t, SSM_CONV_DIM), F32),
        scratch_shapes=[pltpu.VMEM((tm + 16, tc), F32)],
        compiler_params=_cparams("arbitrary", "arbitrary"),
        name="ssm_conv",
    )(p, p, p, conv_w, conv_b)


def _dt_kernel(p_ref, b_ref, o_ref):
    x = p_ref[...] + b_ref[...]
    sp = jnp.maximum(x, 0.0) + jnp.log1p(jnp.exp(-jnp.abs(x)))
    o_ref[0] = sp
    o_ref[1] = pltpu.roll(sp, HEAD_DIM - SSM_HEADS, 1)


def _ssm_dt(p, bias_row, *, col0):
    t = p.shape[0]
    tm = 2 * ROW_TILE
    assert col0 % 128 == 0
    return pl.pallas_call(
        _dt_kernel,
        grid=(t // tm,),
        in_specs=[pl.BlockSpec((tm, 128), lambda i: (i, col0 // 128)),
                  pl.BlockSpec((1, 128), lambda i: (0, 0))],
        out_specs=pl.BlockSpec((2, tm, 128), lambda i: (0, i, 0)),
        out_shape=jax.ShapeDtypeStruct((2, t, 128), F32),
        compiler_params=_cparams("arbitrary"),
        name="ssm_dt",
    )(p, bias_row)


def _ssd_kernel(x_ref, bc_ref, dt_ref, a_ref, e_ref, y_ref, h_ref):
    d = pl.program_id(1)
    lc = SSM_CHUNK
    gw = SSM_INNER // SSM_GROUPS
    hpg = SSM_HEADS // SSM_GROUPS

    @pl.when(pl.program_id(2) == 0)
    def _():
        h_ref[...] = jnp.zeros(h_ref.shape, F32)

    dt = dt_ref[0]
    w = dt * a_ref[0]
    row = lax.broadcasted_iota(I32, (lc, lc), 0)
    col = lax.broadcasted_iota(I32, (lc, lc), 1)
    fwd = d == 0
    sgn = jnp.where(fwd, 1, -1)
    allowed = (row - col) * sgn >= 0
    allowed_t = (col - row) * sgn >= 0
    cs = _hi_dot(allowed.astype(F32), w)
    cs_t = _hi_dot(w.T, allowed_t.astype(F32))
    dt_t = dt.T
    tot = jnp.where(fwd, cs[lc - 1:lc, :], cs[0:1, :])
    e = e_ref[...]

    def expand(a):
        hi = a.astype(BF16)
        lo = (a - hi.astype(F32)).astype(BF16)
        return _dot(hi, e) + _dot(lo, e)

    x = x_ref[...]
    xw = (x * expand(jnp.exp(tot - cs) * dt)).astype(BF16)
    off_scale = expand(jnp.exp(cs))
    state_decay = expand(jnp.exp(jnp.broadcast_to(tot, (16, 128))))[0:1, :]
    xb = x.astype(BF16)
    for g in range(SSM_GROUPS):
        bm = bc_ref[:, g * SSM_STATE:(g + 1) * SSM_STATE]
        cm = bc_ref[:, (SSM_GROUPS + g) * SSM_STATE:(SSM_GROUPS + g + 1) * SSM_STATE].astype(BF16)
        cb = _dot_nt(cm, bm.astype(BF16))
        h_prev = h_ref[g]
        y_off = _dot(cm, h_prev.astype(BF16))
        s_chunk = _dot(bm.T.astype(BF16), xw[:, g * gw:(g + 1) * gw])
        h_ref[g] = state_decay[:, g * gw:(g + 1) * gw] * h_prev + s_chunk
        pieces = []
        for r in range(hpg):
            hh = g * hpg + r
            seg = cs[:, hh:hh + 1] - cs_t[hh:hh + 1, :]
            dec = jnp.exp(jnp.where(allowed, seg, -jnp.inf))
            mix = (cb * dec * dt_t[hh:hh + 1, :]).astype(BF16)
            pieces.append(_dot(mix, xb[:, hh * SSM_HEAD_DIM:(hh + 1) * SSM_HEAD_DIM]))
        y_ref[0, :, g * gw:(g + 1) * gw] = (jnp.concatenate(pieces, axis=1)
                                           + y_off * off_scale[:, g * gw:(g + 1) * gw])


def _ssd_scan(xbc, dt2, a_rows, expand, *, n_lat, seq, n_ctx, bsz):
    t = xbc.shape[0]
    lc = SSM_CHUNK
    ctx_chunks, lat_chunks = n_ctx // lc, seq // lc
    gw = SSM_INNER // SSM_GROUPS

    def rowblk(b, d, s):
        in_ctx = s < ctx_chunks
        j_ctx = jnp.where(d == 0, s, ctx_chunks - 1 - s)
        sl = s - ctx_chunks
        j_lat = jnp.where(d == 0, sl, lat_chunks - 1 - sl)
        return jnp.where(in_ctx, (n_lat + b * n_ctx) // lc + j_ctx, b * lat_chunks + j_lat)

    return pl.pallas_call(
        _ssd_kernel,
        grid=(bsz, 2, ctx_chunks + lat_chunks),
        in_specs=[pl.BlockSpec((lc, SSM_INNER), lambda b, d, s: (rowblk(b, d, s), 0)),
                  pl.BlockSpec((lc, 2 * SSM_GROUPS * SSM_STATE),
                               lambda b, d, s: (rowblk(b, d, s), SSM_INNER // (2 * SSM_GROUPS * SSM_STATE))),
                  pl.BlockSpec((1, lc, 128), lambda b, d, s: (d, rowblk(b, d, s), 0)),
                  pl.BlockSpec((1, 1, 128), lambda b, d, s: (d, 0, 0)),
                  pl.BlockSpec((128, SSM_INNER), lambda b, d, s: (0, 0))],
        out_specs=pl.BlockSpec((1, lc, SSM_INNER), lambda b, d, s: (d, rowblk(b, d, s), 0)),
        out_shape=jax.ShapeDtypeStruct((2, t, SSM_INNER), F32),
        scratch_shapes=[pltpu.VMEM((SSM_GROUPS, SSM_STATE, gw), F32)],
        compiler_params=_cparams("arbitrary", "arbitrary", "arbitrary"),
        name="ssd_scan",
    )(xbc, xbc, dt2, a_rows, expand)


def _ssd_finish_kernel(yf_ref, yb_ref, xs_ref, z0_ref, z1_ref, dsk_ref, g_ref, o_ref):
    gw = SSM_INNER // SSM_GROUPS
    zs = (z0_ref, z1_ref)
    for g in range(SSM_GROUPS):
        sl = slice(g * gw, (g + 1) * gw)
        y = yf_ref[0, :, sl] + yb_ref[0, :, sl] + dsk_ref[:, sl] * xs_ref[:, sl]
        y = y * _silu(zs[g][...])
        o_ref[:, sl] = _head_norm(y, g_ref[:, sl]).astype(o_ref.dtype)


def _ssd_finish(ydir, xbc, p, dsk_row, g_row, *, z_col0):
    t = xbc.shape[0]
    tm = ROW_TILE
    gw = SSM_INNER // SSM_GROUPS
    assert z_col0 % gw == 0 and SSM_GROUPS == 2
    zb = z_col0 // gw
    return pl.pallas_call(
        _ssd_finish_kernel,
        grid=(t // tm,),
        in_specs=[pl.BlockSpec((1, tm, SSM_INNER), lambda i: (0, i, 0)),
                  pl.BlockSpec((1, tm, SSM_INNER), lambda i: (1, i, 0)),
                  pl.BlockSpec((tm, SSM_INNER), lambda i: (i, 0)),
                  pl.BlockSpec((tm, gw), lambda i: (i, zb)),
                  pl.BlockSpec((tm, gw), lambda i: (i, zb + 1)),
                  pl.BlockSpec((1, SSM_INNER), lambda i: (0, 0)),
                  pl.BlockSpec((1, SSM_INNER), lambda i: (0, 0))],
        out_specs=pl.BlockSpec((tm, SSM_INNER), lambda i: (i, 0)),
        out_shape=jax.ShapeDtypeStruct((t, SSM_INNER), BF16),
        compiler_params=_cparams("arbitrary"),
        name="ssd_finish",
    )(ydir, ydir, xbc, p, p, dsk_row, g_row)


def _window_kernel(qt_ref, k_ref, vt_ref, sink_ref, o_ref, *, tq, group, n_ctx, seq):
    band = tq + 2 * WINDOW
    pos_len = n_ctx + seq
    q0 = pl.program_id(2) * tq
    start = pl.multiple_of(jnp.minimum(n_ctx + q0 - WINDOW, pos_len - band), 128)
    qt = qt_ref[0, 0, 0]
    n = group * tq
    s_cx = _dot(k_ref[0, 0, 0:n_ctx, :], qt)
    s_w = _dot(k_ref[0, 0, pl.ds(start, band), :], qt)
    kpos = start - n_ctx + lax.broadcasted_iota(I32, (band, n), 0)
    qpos = q0 + lax.broadcasted_iota(I32, (band, n), 1) % tq
    valid = (jnp.abs(qpos - kpos) <= WINDOW) & (kpos >= 0)
    s_w = jnp.where(valid, s_w, -jnp.inf)
    sink = sink_ref[0]
    m = jnp.maximum(jnp.maximum(jnp.max(s_cx, axis=0, keepdims=True), jnp.max(s_w, axis=0, keepdims=True)), sink)
    p_cx = jnp.exp(s_cx - m)
    p_w = jnp.exp(s_w - m)
    l = jnp.sum(p_cx, axis=0, keepdims=True) + jnp.sum(p_w, axis=0, keepdims=True) + jnp.exp(sink - m)
    acc = (_dot(vt_ref[0, 0, 0:HEAD_DIM, 0:n_ctx], p_cx.astype(BF16))
           + _dot(vt_ref[0, 0, 0:HEAD_DIM, pl.ds(start, band)], p_w.astype(BF16)))
    out = acc / l
    for h in range(group):
        o_ref[:, h * HEAD_DIM:(h + 1) * HEAD_DIM] = out[:, h * tq:(h + 1) * tq].T.astype(o_ref.dtype)


def _attention_c(qt, k, vt, sink_rows, *, n_lat, seq, n_ctx, bsz):
    n_kv, group, tq = k.shape[1], qt.shape[-1] // Q_TILE, Q_TILE
    pos_len = n_ctx + seq
    assert n_ctx >= WINDOW and n_ctx % 128 == 0 and pos_len >= tq + 2 * WINDOW
    return pl.pallas_call(
        functools.partial(_window_kernel, tq=tq, group=group, n_ctx=n_ctx, seq=seq),
        grid=(bsz, n_kv, seq // tq),
        in_specs=[pl.BlockSpec((1, 1, 1, HEAD_DIM, group * tq), lambda b, h, i: (b, h, n_ctx // tq + i, 0, 0)),
                  pl.BlockSpec((1, 1, pos_len, HEAD_DIM), lambda b, h, i: (b, h, 0, 0)),
                  pl.BlockSpec((1, 1, VT_ROWS, pos_len), lambda b, h, i: (b, h, 0, 0)),
                  pl.BlockSpec((1, 1, group * tq), lambda b, h, i: (h, 0, 0))],
        out_specs=pl.BlockSpec((tq, group * HEAD_DIM), lambda b, h, i: (b * (seq // tq) + i, h)),
        out_shape=jax.ShapeDtypeStruct((n_lat, n_kv * group * HEAD_DIM), BF16),
        compiler_params=_cparams("arbitrary", "arbitrary", "arbitrary"),
        name="attn_c_window",
    )(qt, k, vt, sink_rows)


def _prep_na(p, gq, gk, *, col0, n_lat, seq, n_ctx, bsz):
    t = p.shape[0]
    tm = ROW_TILE
    hw = NA_HEADS * HEAD_DIM
    assert col0 % hw == 0 or (2 * col0) % hw == 0
    pos_len = n_ctx + seq
    batch_of, pos_of = _pos_maps(tm, n_lat, seq, n_ctx)
    lat_tiles, seq_tiles, ctx_tiles = n_lat // tm, seq // tm, n_ctx // tm

    def qpos_of(i):
        return jnp.where(i < lat_tiles, i % seq_tiles, seq_tiles + (i - lat_tiles) % ctx_tiles)

    cw = hw // 2
    cb = col0 // cw
    shp = jax.ShapeDtypeStruct((bsz, NA_HEADS, pos_len, HEAD_DIM), BF16)
    kv_spec = pl.BlockSpec((1, NA_HEADS, tm, HEAD_DIM), lambda i: (batch_of(i), 0, pos_of(i), 0))
    q_spec = pl.BlockSpec((1, NA_HEADS, tm, HEAD_DIM), lambda i: (batch_of(i), 0, qpos_of(i), 0))

    def kern(q0, q1, k0, k1, v0, v1, gq_ref, gk_ref, q_ref, k_ref, v_ref):
        halves = NA_HEADS // 2
        for h in range(NA_HEADS):
            sl = slice((h % halves) * HEAD_DIM, (h % halves + 1) * HEAD_DIM)
            pq, pk, pv = ((q0, k0, v0) if h < halves else (q1, k1, v1))
            q_ref[0, h] = (_head_norm(pq[:, sl], gq_ref[...]) * (HEAD_DIM ** -0.5)).astype(BF16)
            k_ref[0, h] = _head_norm(pk[:, sl], gk_ref[...]).astype(BF16)
            v_ref[0, h] = pv[:, sl].astype(BF16)

    def col_spec(j):
        return pl.BlockSpec((tm, cw), lambda i: (i, cb + j))

    return pl.pallas_call(
        kern,
        grid=(t // tm,),
        in_specs=[col_spec(j) for j in range(6)] + [pl.BlockSpec((1, HEAD_DIM), lambda i: (0, 0)),
                                                    pl.BlockSpec((1, HEAD_DIM), lambda i: (0, 0))],
        out_specs=[q_spec, kv_spec, kv_spec],
        out_shape=[shp, shp, shp],
        compiler_params=_cparams("arbitrary"),
        name="prep_na",
    )(p, p, p, p, p, p, gq, gk)


def _na_kernel(q_ref, k_ref, v_ref, bias_ref, o_ref, *, n_ctx, rows):
    tq = NA_TILE_ROWS * GRID_W
    win = NA_WIN_ROWS * GRID_W
    ti = pl.program_id(2)
    w0 = jnp.clip(ti * NA_TILE_ROWS - NA_ROWS // 2, 0, rows - NA_WIN_ROWS)
    start = pl.multiple_of(n_ctx + w0 * GRID_W, GRID_W)
    n_tiles = rows // NA_TILE_ROWS
    gq = tq // 2
    sub = (NA_TILE_ROWS // 2 + NA_ROWS) * GRID_W
    shift = (NA_WIN_ROWS * GRID_W - sub)
    for g in range(2):
        off = jnp.where(ti == 0, 0, jnp.where(ti == n_tiles - 1, shift, g * shift))
        off = pl.multiple_of(off, 128)
        ks = pl.multiple_of(start + off, GRID_W)
        q = q_ref[0, 0, g * gq:(g + 1) * gq, :]
        s_cx = _dot_nt(q, k_ref[0, 0, 0:n_ctx, :])
        s_nb = _dot_nt(q, k_ref[0, 0, pl.ds(ks, sub), :]) + bias_ref[0, 0, g * gq:(g + 1) * gq, pl.ds(off, sub)]
        m = jnp.maximum(jnp.max(s_cx, axis=1, keepdims=True), jnp.max(s_nb, axis=1, keepdims=True))
        p_cx = jnp.exp(s_cx - m)
        p_nb = jnp.exp(s_nb - m)
        l = jnp.sum(p_cx, axis=1, keepdims=True) + jnp.sum(p_nb, axis=1, keepdims=True)
        acc = (_dot(p_cx.astype(BF16), v_ref[0, 0, 0:n_ctx, :])
               + _dot(p_nb.astype(BF16), v_ref[0, 0, pl.ds(ks, sub), :]))
        o_ref[g * gq:(g + 1) * gq, :] = (acc / l).astype(o_ref.dtype)


def _na_bias_tables(rpb, rows):
    n_tiles = rows // NA_TILE_ROWS
    tables = []
    for ti in (0, 1, n_tiles - 1):
        w0 = min(max(ti * NA_TILE_ROWS - NA_ROWS // 2, 0), rows - NA_WIN_ROWS)
        qr = ti * NA_TILE_ROWS + jnp.arange(NA_TILE_ROWS)
        r0 = jnp.clip(qr - NA_ROWS // 2, 0, rows - NA_ROWS)
        kr = w0 + jnp.arange(NA_WIN_ROWS)
        row_ok = (kr[None, :] >= r0[:, None]) & (kr[None, :] < r0[:, None] + NA_ROWS)
        dr = jnp.clip(kr[None, :] - qr[:, None] + NA_ROWS - 1, 0, 2 * NA_ROWS - 2)
        qc = jnp.arange(GRID_W)
        c0 = jnp.clip(qc - NA_COLS // 2, 0, GRID_W - NA_COLS)
        kc = jnp.arange(GRID_W)
        col_ok = (kc[None, :] >= c0[:, None]) & (kc[None, :] < c0[:, None] + NA_COLS)
        dc = jnp.clip(kc[None, :] - qc[:, None] + NA_COLS - 1, 0, 2 * NA_COLS - 2)
        sel_r = ((dr[:, :, None] == jnp.arange(2 * NA_ROWS - 1)) & row_ok[:, :, None]).astype(F32)
        sel_c = ((dc[:, :, None] == jnp.arange(2 * NA_COLS - 1)) & col_ok[:, :, None]).astype(F32)
        b = jnp.einsum("rki,hij,cqj->hrckq", sel_r, rpb.astype(F32), sel_c, precision=HI)
        ok = row_ok[:, None, :, None] & col_ok[None, :, None, :]
        b = jnp.where(ok[None], b, -jnp.inf)
        tables.append(b.reshape(rpb.shape[0], NA_TILE_ROWS * GRID_W, NA_WIN_ROWS * GRID_W))
    return jnp.stack(tables, axis=0).astype(F32)


def _attention_d(q, k, v, bias, *, n_lat, seq, n_ctx, bsz):
    rows = seq // GRID_W
    tq = NA_TILE_ROWS * GRID_W
    win = NA_WIN_ROWS * GRID_W
    n_tiles = rows // NA_TILE_ROWS
    pos_len = n_ctx + seq
    assert rows % NA_TILE_ROWS == 0 and rows >= NA_WIN_ROWS and n_tiles >= 3

    def kind(i):
        return jnp.where(i == 0, 0, jnp.where(i == n_tiles - 1, 2, 1))

    return pl.pallas_call(
        functools.partial(_na_kernel, n_ctx=n_ctx, rows=rows),
        grid=(bsz, NA_HEADS, n_tiles),
        in_specs=[pl.BlockSpec((1, 1, tq, HEAD_DIM), lambda b, h, i: (b, h, i, 0)),
                  pl.BlockSpec((1, 1, pos_len, HEAD_DIM), lambda b, h, i: (b, h, 0, 0)),
                  pl.BlockSpec((1, 1, pos_len, HEAD_DIM), lambda b, h, i: (b, h, 0, 0)),
                  pl.BlockSpec((1, 1, tq, win), lambda b, h, i: (kind(i), h, 0, 0))],
        out_specs=pl.BlockSpec((tq, HEAD_DIM), lambda b, h, i: (b * n_tiles + i, h)),
        out_shape=jax.ShapeDtypeStruct((n_lat, NA_HEADS * HEAD_DIM), BF16),
        compiler_params=_cparams("arbitrary", "arbitrary", "arbitrary"),
        name="attn_d_neighbourhood",
    )(q, k, v, bias)


def _pack_bf16_pairs(f):
    half = f.shape[1] // 2
    bits = lax.bitcast_convert_type(f.astype(BF16).astype(F32), U32)
    lo = lax.shift_right_logical(bits[:, :half], jnp.uint32(16))
    hi = bits[:, half:] & jnp.uint32(0xFFFF0000)
    return hi | lo


def _unpack_bf16_pairs(bits):
    lo = lax.bitcast_convert_type(lax.shift_left(bits, jnp.uint32(16)), F32).astype(BF16)
    hi = lax.bitcast_convert_type(bits & jnp.uint32(0xFFFF0000), F32).astype(BF16)
    return lo, hi


def _store_row_tiles(ref, val):
    rows, width = val.shape
    per = width // 128
    for s in range(per):
        ref[pl.ds(s, rows, stride=per), :] = val[:, s * 128:(s + 1) * 128]


def _load_row_tiles(ref, rows, per):
    return jnp.concatenate([ref[pl.ds(s, rows, stride=per), :] for s in range(per)], axis=1)


def _outproj_kernel(x_ref, y1_ref, y2_ref, w_ref, mod_ref, g_ref, wrh_ref, wrl_ref, xo_ref, fpk_ref, lg_ref):
    half = w_ref.shape[0] // 2
    tm = x_ref.shape[0]
    per = fpk_ref.shape[0] // tm
    rows = ROW_TILE
    for r0 in range(0, tm, rows):
        rs = slice(r0, r0 + rows)
        delta = _dot(y1_ref[rs, :], w_ref[0:half, :]) + _dot(y2_ref[rs, :], w_ref[half:, :])
        xn = x_ref[rs, :] + mod_ref[0, 2:3, :] * delta
        xo_ref[rs, :] = xn
        f = _norm_mod(xn, g_ref[...], mod_ref[0, 3:4, :], mod_ref[0, 4:5, :])
        f_hi = f.astype(BF16)
        f_lo = (f - f_hi.astype(F32)).astype(BF16)
        lg_ref[:, rs] = _dot_nt(wrh_ref[...], f_hi) + (_dot_nt(wrl_ref[...], f_hi) + _dot_nt(wrh_ref[...], f_lo))
        _store_row_tiles(fpk_ref.at[pl.ds(r0 * per, rows * per)], _pack_bf16_pairs(f))


def _outproj(x, y1, y2, w_bf16, mod, g, wr_t, *, n_rows, n_lat, seq, bsz):
    d = x.shape[1]
    tm = 2 * ROW_TILE
    hw = y1.shape[1]
    wr_hi = wr_t.astype(BF16)
    wr_lo = (wr_t - wr_hi.astype(F32)).astype(BF16)
    return pl.pallas_call(
        _outproj_kernel,
        grid=(n_rows // tm,),
        in_specs=[pl.BlockSpec((tm, d), lambda i: (i, 0)),
                  pl.BlockSpec((tm, hw), lambda i: (i, 0)),
                  pl.BlockSpec((tm, hw), lambda i: (i, 0)),
                  pl.BlockSpec((2 * hw, d), lambda i: (0, 0)),
                  pl.BlockSpec((1, 6, d), lambda i: (_group_of_tile(i, tm, n_lat, seq, bsz), 0, 0)),
                  pl.BlockSpec((1, d), lambda i: (0, 0)),
                  pl.BlockSpec((N_EXPERTS, d), lambda i: (0, 0)),
                  pl.BlockSpec((N_EXPERTS, d), lambda i: (0, 0))],
        out_specs=[pl.BlockSpec((tm, d), lambda i: (i, 0)),
                   pl.BlockSpec((tm * (d // 256), 128), lambda i: (i, 0)),
                   pl.BlockSpec((N_EXPERTS, tm), lambda i: (0, i))],
        out_shape=[jax.ShapeDtypeStruct((n_rows, d), F32),
                   jax.ShapeDtypeStruct((n_rows * (d // 256), 128), U32),
                   jax.ShapeDtypeStruct((N_EXPERTS, n_rows), F32)],
        compiler_params=_cparams("arbitrary"),
        name="outproj_residual_moe_in",
    )(x, y1, y2, w_bf16, mod, g, wr_hi, wr_lo)


def _route_kernel(lg_ref, b_ref, e_ref, r_ref, w_ref, cnt_ref, run_ref):
    tr = lg_ref.shape[1]
    per = N_EXPERTS // N_GROUPS
    neg = -jnp.inf

    @pl.when(pl.program_id(0) == 0)
    def _():
        run_ref[...] = jnp.zeros(run_ref.shape, F32)

    scores = jax.nn.sigmoid(lg_ref[...])
    sel = scores + b_ref[...]
    sel3 = sel.reshape(N_GROUPS, per, tr)
    mem = lax.broadcasted_iota(I32, (N_GROUPS, per, tr), 1)
    m1 = jnp.max(sel3, axis=1, keepdims=True)
    i1 = jnp.min(jnp.where(sel3 == m1, mem, per), axis=1, keepdims=True)
    m2 = jnp.max(jnp.where(mem == i1, neg, sel3), axis=1, keepdims=True)
    gs = (m1 + m2).reshape(N_GROUPS, tr)
    gid = lax.broadcasted_iota(I32, (N_GROUPS, tr), 0)
    keep = jnp.zeros((N_GROUPS, tr), F32)
    for _ in range(TOPK_GROUPS):
        gm = jnp.max(gs, axis=0, keepdims=True)
        gi = jnp.min(jnp.where(gs == gm, gid, N_GROUPS), axis=0, keepdims=True)
        hit = gid == gi
        keep = jnp.where(hit, 1.0, keep)
        gs = jnp.where(hit, neg, gs)
    keep3 = jnp.broadcast_to(keep.reshape(N_GROUPS, 1, tr), (N_GROUPS, per, tr))
    cand = jnp.where(keep3 > 0.5, sel3, neg).reshape(N_EXPERTS, tr)
    eid = lax.broadcasted_iota(I32, (N_EXPERTS, tr), 0)
    idxs, ws = [], []
    sel_f = jnp.zeros((N_EXPERTS, tr), F32)
    for _ in range(TOP_K):
        cm = jnp.max(cand, axis=0, keepdims=True)
        ci = jnp.min(jnp.where(cand == cm, eid, N_EXPERTS), axis=0, keepdims=True)
        hit = eid == ci
        idxs.append(ci)
        ws.append(jnp.sum(jnp.where(hit, scores, 0.0), axis=0, keepdims=True))
        sel_f = jnp.where(hit, 1.0, sel_f)
        cand = jnp.where(hit, neg, cand)
    before = (lax.broadcasted_iota(I32, (tr, tr), 0) < lax.broadcasted_iota(I32, (tr, tr), 1))
    rank = _dot(sel_f.astype(BF16), jnp.where(before, 1.0, 0.0).astype(BF16)) + run_ref[...]
    run_ref[...] = run_ref[...] + jnp.sum(sel_f, axis=1, keepdims=True)
    cnt_ref[...] = run_ref[...]
    wsum = ws[0]
    for kk in range(1, TOP_K):
        wsum = wsum + ws[kk]
    for kk in range(TOP_K):
        e_ref[kk:kk + 1, :] = idxs[kk]
        r_ref[kk:kk + 1, :] = jnp.sum(jnp.where(eid == idxs[kk], rank, 0.0), axis=0, keepdims=True).astype(I32)
        w_ref[kk:kk + 1, :] = ws[kk] / wsum * ROUTE_SCALE


def _route(logits_t, b_col):
    n_e, t = logits_t.shape
    tr = ROW_TILE
    return pl.pallas_call(
        _route_kernel,
        grid=(t // tr,),
        in_specs=[pl.BlockSpec((n_e, tr), lambda i: (0, i)),
                  pl.BlockSpec((n_e, 1), lambda i: (0, 0))],
        out_specs=[pl.BlockSpec((TOP_K, tr), lambda i: (0, i)),
                   pl.BlockSpec((TOP_K, tr), lambda i: (0, i)),
                   pl.BlockSpec((TOP_K, tr), lambda i: (0, i)),
                   pl.BlockSpec((n_e, 1), lambda i: (0, 0))],
        out_shape=[jax.ShapeDtypeStruct((TOP_K, t), I32),
                   jax.ShapeDtypeStruct((TOP_K, t), I32),
                   jax.ShapeDtypeStruct((TOP_K, t), F32),
                   jax.ShapeDtypeStruct((n_e, 1), F32)],
        scratch_shapes=[pltpu.VMEM((n_e, 1), F32)],
        compiler_params=_cparams("arbitrary"),
        name="moe_route",
    )(logits_t, b_col)


PK_TILES = 8


def _ffn(bits, w1, w3, w2):
    half = bits.shape[1]
    lo, hi = _unpack_bf16_pairs(bits)
    h1 = _dot(lo, w1[0:half, :]) + _dot(hi, w1[half:, :])
    h3 = _dot(lo, w3[0:half, :]) + _dot(hi, w3[half:, :])
    return _dot((_silu(h1) * h3).astype(BF16), w2[...])


def _expert_kernel(be_ref, bv_ref, nu_ref, xs_ref, w1_ref, w3_ref, w2_ref, y_ref, w1b, w3b, w2b):
    i = pl.program_id(0)

    @pl.when(i < nu_ref[0])
    def _():
        e = be_ref[i]
        prev = be_ref[jnp.maximum(i - 1, 0)]

        @pl.when((i == 0) | (e != prev))
        def _():
            w1b[...] = w1_ref[0, 0].astype(BF16)
            w3b[...] = w3_ref[0, 0].astype(BF16)
            w2b[...] = w2_ref[0, 0].astype(BF16)

        tm = MOE_BLOCK
        bits = _load_row_tiles(xs_ref, tm, PK_TILES)
        rows = lax.broadcasted_iota(I32, bits.shape, 0)
        bits = jnp.where(rows < bv_ref[i], bits, jnp.uint32(0))
        _store_row_tiles(y_ref, _pack_bf16_pairs(_ffn(bits, w1b, w3b, w2b)))


def _experts(blk_expert, blk_valid, n_used, xs, w1, w3, w2, layer):
    n_slots = xs.shape[0] // PK_TILES
    d, de = w1.shape[2], w1.shape[3]
    assert d == 2 * PK_TILES * 128
    tm = MOE_BLOCK
    grid_spec = pltpu.PrefetchScalarGridSpec(
        num_scalar_prefetch=3,
        grid=(n_slots // tm,),
        in_specs=[pl.BlockSpec((tm * PK_TILES, 128), lambda i, be, bv, nu: (i, 0)),
                  pl.BlockSpec((1, 1, d, de), lambda i, be, bv, nu: (layer, be[i], 0, 0)),
                  pl.BlockSpec((1, 1, d, de), lambda i, be, bv, nu: (layer, be[i], 0, 0)),
                  pl.BlockSpec((1, 1, de, d), lambda i, be, bv, nu: (layer, be[i], 0, 0))],
        out_specs=pl.BlockSpec((tm * PK_TILES, 128), lambda i, be, bv, nu: (i, 0)),
        scratch_shapes=[pltpu.VMEM((d, de), BF16), pltpu.VMEM((d, de), BF16), pltpu.VMEM((de, d), BF16)],
    )
    return pl.pallas_call(
        _expert_kernel,
        grid_spec=grid_spec,
        out_shape=jax.ShapeDtypeStruct((n_slots * PK_TILES, 128), U32),
        compiler_params=_cparams("arbitrary"),
        name="moe_experts",
    )(blk_expert, blk_valid, n_used, xs, w1, w3, w2)


def _scatter_kernel(dest_ref, f_ref, xs_ref, sem, *, tm):
    def copy(t, kk):
        src = pl.multiple_of(t * PK_TILES, PK_TILES)
        dst = pl.multiple_of(dest_ref[0, 0, t * TOP_K + kk], PK_TILES)
        return pltpu.make_async_copy(f_ref.at[pl.ds(src, PK_TILES)], xs_ref.at[pl.ds(dst, PK_TILES)], sem)

    def issue(t, c):
        for kk in range(TOP_K):
            copy(t, kk).start()
        return c

    def drain(t, c):
        for kk in range(TOP_K):
            copy(t, kk).wait()
        return c

    lax.fori_loop(0, tm, issue, 0)
    lax.fori_loop(0, tm, drain, 0)


def _scatter_rows(dest3, fpk, n_slots):
    tm = dest3.shape[2] // TOP_K
    t = fpk.shape[0] // PK_TILES
    return pl.pallas_call(
        functools.partial(_scatter_kernel, tm=tm),
        grid=(t // tm,),
        in_specs=[pl.BlockSpec((1, 1, TOP_K * tm), lambda i: (i, 0, 0), memory_space=pltpu.SMEM),
                  pl.BlockSpec((tm * PK_TILES, 128), lambda i: (i, 0))],
        out_specs=pl.BlockSpec(memory_space=pl.ANY),
        out_shape=jax.ShapeDtypeStruct((n_slots * PK_TILES, 128), U32),
        scratch_shapes=[pltpu.SemaphoreType.DMA(())],
        compiler_params=_cparams("arbitrary"),
        name="moe_scatter_rows",
    )(dest3, fpk)


def _shared_kernel(f_ref, w1_ref, w3_ref, w2_ref, y_ref):
    y_ref[...] = _ffn(_load_row_tiles(f_ref, y_ref.shape[0], PK_TILES), w1_ref, w3_ref, w2_ref)


def _shared_expert(fpk, w1, w3, w2):
    t = fpk.shape[0] // PK_TILES
    d, de = w1.shape
    tm = 2 * ROW_TILE
    return pl.pallas_call(
        _shared_kernel,
        grid=(t // tm,),
        in_specs=[pl.BlockSpec((tm * PK_TILES, 128), lambda i: (i, 0)),
                  pl.BlockSpec((d, de), lambda i: (0, 0)),
                  pl.BlockSpec((d, de), lambda i: (0, 0)),
                  pl.BlockSpec((de, d), lambda i: (0, 0))],
        out_specs=pl.BlockSpec((tm, d), lambda i: (i, 0)),
        out_shape=jax.ShapeDtypeStruct((t, d), F32),
        compiler_params=_cparams("arbitrary"),
        name="moe_shared_expert",
    )(fpk, w1, w3, w2)


def _combine_kernel(dest_ref, dest_next_ref, w_ref, x_ref, ysh_ref, mod_ref, y_ref, o_ref, ybuf, lo_ref, hi_ref, sems):
    tm = x_ref.shape[0]
    i = pl.program_id(0)
    n = pl.num_programs(0)
    slot = i % 2

    def copy(idx_ref, s, t, kk):
        src = pl.multiple_of(idx_ref[0, 0, t * TOP_K + kk], PK_TILES)
        dst = pl.multiple_of(t * PK_TILES, PK_TILES)
        return pltpu.make_async_copy(y_ref.at[pl.ds(src, PK_TILES)], ybuf.at[s, kk, pl.ds(dst, PK_TILES)],
                                     sems.at[s])

    def issue_tile(idx_ref, s):
        def issue(t, c):
            for kk in range(TOP_K):
                copy(idx_ref, s, t, kk).start()
            return c

        lax.fori_loop(0, tm, issue, 0)

    @pl.when(i == 0)
    def _():
        issue_tile(dest_ref, 0)

    @pl.when(i + 1 < n)
    def _():
        issue_tile(dest_next_ref, 1 - slot)

    def drain(t, c):
        for kk in range(TOP_K):
            copy(dest_ref, slot, t, kk).wait()
        return c

    lax.fori_loop(0, tm, drain, 0)

    def token(t, c):
        r = pl.multiple_of(t * PK_TILES, PK_TILES)
        a_lo = a_hi = None
        for kk in range(TOP_K):
            w = w_ref[0, 0, t * TOP_K + kk]
            bits = ybuf[slot, kk, pl.ds(r, PK_TILES), :]
            lo = lax.bitcast_convert_type(lax.shift_left(bits, jnp.uint32(16)), F32)
            hi = lax.bitcast_convert_type(bits & jnp.uint32(0xFFFF0000), F32)
            a_lo = w * lo if a_lo is None else a_lo + w * lo
            a_hi = w * hi if a_hi is None else a_hi + w * hi
        lo_ref[pl.ds(r, PK_TILES), :] = a_lo
        hi_ref[pl.ds(r, PK_TILES), :] = a_hi
        return c

    lax.fori_loop(0, tm, token, 0, unroll=4)
    half = PK_TILES * 128
    for c in range(PK_TILES):
        for base, acc_ref in ((0, lo_ref), (half, hi_ref)):
            sl = slice(base + c * 128, base + (c + 1) * 128)
            routed = acc_ref[pl.ds(c, tm, stride=PK_TILES), :]
            o_ref[:, sl] = x_ref[:, sl] + mod_ref[0, 5:6, sl] * (routed + ysh_ref[:, sl])


def _combine(dest3, x, ysh, w3, mod, y, *, tm, n_lat, seq, bsz):
    t, d = x.shape
    n = t // tm
    return pl.pallas_call(
        _combine_kernel,
        grid=(n,),
        in_specs=[pl.BlockSpec((1, 1, TOP_K * tm), lambda i: (i, 0, 0), memory_space=pltpu.SMEM),
                  pl.BlockSpec((1, 1, TOP_K * tm), lambda i: (jnp.minimum(i + 1, n - 1), 0, 0),
                               memory_space=pltpu.SMEM),
                  pl.BlockSpec((1, 1, TOP_K * tm), lambda i: (i, 0, 0), memory_space=pltpu.SMEM),
                  pl.BlockSpec((tm, d), lambda i: (i, 0)),
                  pl.BlockSpec((tm, d), lambda i: (i, 0)),
                  pl.BlockSpec((1, 6, d), lambda i: (_group_of_tile(i, tm, n_lat, seq, bsz), 0, 0)),
                  pl.BlockSpec(memory_space=pl.ANY)],
        out_specs=pl.BlockSpec((tm, d), lambda i: (i, 0)),
        out_shape=jax.ShapeDtypeStruct((t, d), F32),
        scratch_shapes=[pltpu.VMEM((2, TOP_K, tm * PK_TILES, 128), U32),
                        pltpu.VMEM((tm * PK_TILES, 128), F32), pltpu.VMEM((tm * PK_TILES, 128), F32),
                        pltpu.SemaphoreType.DMA((2,))],
        compiler_params=_cparams("arbitrary"),
        name="moe_combine",
    )(dest3, dest3, w3, x, ysh, mod, y)


def _moe(x_new, fpk, logits_t, mod, b_router, w_e1, w_e3, w_e2, w_s1, w_s3, w_s2, *, layer, n_lat, seq, bsz):
    t = x_new.shape[0]
    top_e, top_r, top_w, counts = _route(logits_t, b_router.reshape(N_EXPERTS, 1))
    counts = counts[:, 0].astype(I32)
    blocks = (counts + MOE_BLOCK - 1) // MOE_BLOCK
    blk_end = jnp.cumsum(blocks)
    blk_start = blk_end - blocks
    n_blocks = -(-(t * TOP_K) // MOE_BLOCK) + N_EXPERTS
    n_slots = n_blocks * MOE_BLOCK
    slot_start = blk_start * MOE_BLOCK
    dest = jnp.sum(jnp.where(top_e[None] == jnp.arange(N_EXPERTS, dtype=I32)[:, None, None],
                             slot_start[:, None, None], 0), axis=0) + top_r
    bi = jnp.arange(n_blocks, dtype=I32)
    blk_expert = jnp.minimum(jnp.sum((bi[:, None] >= blk_end[None, :]).astype(I32), axis=1), N_EXPERTS - 1)
    blk_valid = jnp.clip(counts[blk_expert] - (bi - blk_start[blk_expert]) * MOE_BLOCK, 0, MOE_BLOCK)
    n_used = blk_end[-1:].astype(I32)

    dest_tk = dest.T
    tm_s = 2 * ROW_TILE
    dest_s = (dest_tk * PK_TILES).reshape(t // tm_s, 1, tm_s * TOP_K)
    xs = _scatter_rows(dest_s, fpk, n_slots)
    y = _experts(blk_expert.astype(I32), blk_valid.astype(I32), n_used, xs, w_e1, w_e3, w_e2, layer)
    ysh = _shared_expert(fpk, w_s1.astype(BF16), w_s3.astype(BF16), w_s2.astype(BF16))
    tm_c = ROW_TILE
    dest_c = (dest_tk * PK_TILES).reshape(t // tm_c, 1, tm_c * TOP_K)
    w_c = top_w.T.reshape(t // tm_c, 1, tm_c * TOP_K)
    return _combine(dest_c, x_new, ysh, w_c, mod, y, tm=tm_c, n_lat=n_lat, seq=seq, bsz=bsz)


def _pad_cols(w, n):
    return jnp.pad(w, ((0, 0), (0, n - w.shape[1])))


def kernel(x, c, ctx, c_ctx, w_ada, b_ada, g_mix, g_ffn, w_in_even, gq_a, gk_a, conv_w, conv_b, a_log, dt_bias,
           d_skip, g_ssm, w_in_odd, gq_c, gk_c, sink_c, gq_d, gk_d, rpb_d, w_out, w_router, b_router, w_e1, w_e3,
           w_e2, w_s1, w_s3, w_s2):
    bsz, seq, d = x.shape
    n_ctx = ctx.shape[1]
    depth = w_ada.shape[0]
    n_lat = bsz * seq
    geo = dict(n_lat=n_lat, seq=seq, n_ctx=n_ctx, bsz=bsz)
    xs = jnp.concatenate([x.reshape(n_lat, d), ctx.reshape(bsz * n_ctx, d)], axis=0)
    cvec = jnp.zeros((8, d), F32).at[:bsz].set(c).at[bsz].set(c_ctx)
    cos, sin = _rope_tables(seq, n_ctx)
    expand = jnp.zeros((128, SSM_INNER), BF16).at[:SSM_HEADS].set(
        jnp.repeat(jnp.eye(SSM_HEADS, dtype=BF16), SSM_HEAD_DIM, axis=1))

    for layer in range(depth):
        last = layer == depth - 1
        i = layer // 2
        mod = _adaln(cvec, w_ada, b_ada[layer].reshape(1, -1), layer)[:bsz + 1].reshape(bsz + 1, 6, d)
        g1 = g_mix[layer].reshape(1, d)
        if layer % 2 == 0:
            n_in = w_in_even.shape[2]
            n_pad = -(-n_in // 384) * 384
            p = _norm_mod_matmul(xs, g1, mod, _pad_cols(w_in_even[i], n_pad).astype(BF16),
                                 n_lat=n_lat, seq=seq, bsz=bsz, tn=n_pad // 3)
            qt, k, vt = _prep_rope(p, cos, sin, gq_a[i].reshape(1, -1), gk_a[i].reshape(1, -1),
                                   q_scale=HEAD_DIM ** -0.5 * LOG2E, n_q=A_HEADS, n_kv=A_KV_HEADS, **geo)
            y1 = _attention_a(qt, k, vt, need_ctx=not last, **geo)
            xbc = _ssm_conv(p, conv_w[i], conv_b[i].reshape(1, -1), col0=A_IN + SSM_INNER,
                            n_lat=n_lat, seq=seq, n_ctx=n_ctx)
            dt_row = jnp.zeros((1, 128), F32).at[0, :2 * SSM_HEADS].set(dt_bias[i].reshape(-1))
            dt2 = _ssm_dt(p, dt_row, col0=A_IN + SSM_INNER + SSM_CONV_DIM)
            a_rows = jnp.zeros((2, 1, 128), F32).at[:, 0, :SSM_HEADS].set(-jnp.exp(a_log[i]))
            ydir = _ssd_scan(xbc, dt2, a_rows, expand, **geo)
            y2 = _ssd_finish(ydir, xbc, p, jnp.repeat(d_skip[i], SSM_HEAD_DIM).reshape(1, -1),
                             g_ssm[i].reshape(1, -1), z_col0=A_IN)
        else:
            p = _norm_mod_matmul(xs, g1, mod, w_in_odd[i].astype(BF16), n_lat=n_lat, seq=seq, bsz=bsz,
                                 tn=w_in_odd.shape[2] // 3)
            qt, k, vt = _prep_rope(p, cos, sin, gq_c[i].reshape(1, -1), gk_c[i].reshape(1, -1),
                                   q_scale=HEAD_DIM ** -0.5, n_q=A_HEADS, n_kv=A_KV_HEADS, **geo)
            group = A_HEADS // A_KV_HEADS
            sink_rows = jnp.repeat(sink_c[i].reshape(A_KV_HEADS, group), Q_TILE, axis=1).reshape(
                A_KV_HEADS, 1, group * Q_TILE)
            y1 = _attention_c(qt, k, vt, sink_rows, **geo)
            qd, kd, vd = _prep_na(p, gq_d[i].reshape(1, -1), gk_d[i].reshape(1, -1), col0=A_IN, **geo)
            y2 = _attention_d(qd, kd, vd, _na_bias_tables(rpb_d[i], seq // GRID_W), **geo)
        n_rows = n_lat if last else xs.shape[0]
        x_new, fpk, logits_t = _outproj(xs, y1, y2, w_out[layer].astype(BF16), mod, g_ffn[layer].reshape(1, d),
                                        w_router[layer].T, n_rows=n_rows, n_lat=n_lat, seq=seq, bsz=bsz)
        xs = _moe(x_new, fpk, logits_t, mod, b_router[layer], w_e1, w_e3, w_e2,
                  w_s1[layer], w_s3[layer], w_s2[layer], layer=layer, n_lat=n_lat, seq=seq, bsz=bsz)
    return xs[:n_lat].reshape(bsz, seq, d)
```
